```python
import jax, jax.numpy as jnp
from jax import lax
import numpy as np

D_MODEL = 1024
BATCH = 8
SEQ = 2048
DEPTH = 2
DEC_BATCH = 128
DEC_SEQ = 1
PAST_LEN = 16384
PAGE_SIZE = 128

N_MIXERS = 2
N_RET_LAYERS = (DEPTH + 1) // 2
N_GLA_LAYERS = DEPTH // 2

RET_HEADS = 4
RET_DK = D_MODEL // RET_HEADS
RET_DV = 2 * RET_DK
RET_CHUNK = 64
ROPE_BASE = 10000.0

GLA_HEADS = 4
GLA_DK = D_MODEL // (2 * GLA_HEADS)
GLA_DV = D_MODEL // GLA_HEADS
GLA_RANK = 16
GLA_TAU = 16.0
GLA_CHUNK = 16

N_EXPERTS = 64
TOP_K = 8
N_GROUPS = 8
TOPK_GROUPS = 4
D_EXPERT = D_MODEL // 4
D_SHARED = D_EXPERT
ROUTED_SCALE = 2.5
MOE_BLOCK = 128

EPS = 1e-6

kernel_name = "hybrid_retnet_gla_moe_adaln_step"


def rmsnorm(x, g):
    xf = x.astype(jnp.float32)
    y = xf * lax.rsqrt(jnp.mean(xf * xf, axis=-1, keepdims=True) + EPS)
    return (y * g.astype(jnp.float32)).astype(x.dtype)


def modulate(x, g, shift, scale):
    return rmsnorm(x, g) * (1 + scale[:, None, :]) + shift[:, None, :]


def rotary(x, pos):
    half = x.shape[-1] // 2
    inv = ROPE_BASE ** (-jnp.arange(half, dtype=jnp.float32) / half)
    ang = pos.astype(jnp.float32)[:, None] * inv[None, :]
    cos = jnp.cos(ang)[None, :, None, :]
    sin = jnp.sin(ang)[None, :, None, :]
    xf = x.astype(jnp.float32)
    x1, x2 = xf[..., :half], xf[..., half:]
    return jnp.concatenate([x1 * cos - x2 * sin, x1 * sin + x2 * cos], axis=-1)


def pad_time(x, tp):
    return jnp.pad(x, [(0, 0), (0, tp - x.shape[1])] + [(0, 0)] * (x.ndim - 2))


def to_chunks(x, L):
    B, Tp, H, d = x.shape
    return x.reshape(B, Tp // L, L, H, d).transpose(1, 0, 3, 2, 4)


def from_chunks(o, T):
    nC, B, H, L, d = o.shape
    return o.transpose(1, 0, 3, 2, 4).reshape(B, nC * L, H, d)[:, :T]


def retention_chunked(q, k, v, log_gamma, S0):
    T = q.shape[1]
    L = min(RET_CHUNK, T)
    tp = -(-T // L) * L
    valid = jnp.arange(tp) < T
    log_a = jnp.where(valid[:, None], log_gamma[None, :], 0.0)
    cum = jnp.cumsum(log_a.reshape(tp // L, L, -1).transpose(0, 2, 1), axis=-1)
    qc, kc, vc = (to_chunks(pad_time(a, tp), L) for a in (q, k, v))
    causal = jnp.tril(jnp.ones((L, L), dtype=bool))

    def step(S, inp):
        qb, kb, vb, cb = inp
        dec = jnp.exp(jnp.where(causal, cb[:, :, None] - cb[:, None, :], -jnp.inf))
        att = jnp.einsum('bhid,bhjd->bhij', qb, kb) * dec[None]
        o = (jnp.einsum('bhij,bhjv->bhiv', att, vb)
             + jnp.einsum('bhid,bhdv->bhiv', qb * jnp.exp(cb)[..., None], S))
        tot = cb[:, -1]
        kd = kb * jnp.exp(tot[:, None] - cb)[..., None]
        S = jnp.exp(tot)[:, None, None] * S + jnp.einsum('bhjd,bhjv->bhdv', kd, vb)
        return S, o

    S, oc = lax.scan(step, S0.astype(jnp.float32), (qc, kc, vc, cum))
    return from_chunks(oc, T), S


def gla_chunked(q, k, v, log_a, S0):
    T = q.shape[1]
    L = min(GLA_CHUNK, T)
    tp = -(-T // L) * L
    qc, kc, vc, ac = (to_chunks(pad_time(a, tp), L) for a in (q, k, v, log_a))
    cum = jnp.cumsum(ac, axis=3)
    causal = jnp.tril(jnp.ones((L, L), dtype=bool))

    def step(S, inp):
        qb, kb, vb, cb = inp
        qe = qb * jnp.exp(cb)
        ke = kb * jnp.exp(-cb)
        att = jnp.where(causal, jnp.einsum('bhid,bhjd->bhij', qe, ke), 0.0)
        o = jnp.einsum('bhij,bhjv->bhiv', att, vb) + jnp.einsum('bhid,bhdv->bhiv', qe, S)
        tot = cb[:, :, -1:, :]
        kd = kb * jnp.exp(tot - cb)
        S = jnp.exp(tot)[:, :, 0, :, None] * S + jnp.einsum('bhjd,bhjv->bhdv', kd, vb)
        return S, o

    S, oc = lax.scan(step, S0.astype(jnp.float32), (qc, kc, vc, cum))
    return from_chunks(oc, T), S


def retention_mixer(h, pos, S0, w_in, norm_g, w_out):
    B, T, _ = h.shape
    nqk, nv = RET_HEADS * RET_DK, RET_HEADS * RET_DV
    proj = h @ w_in
    q, k, v, g = jnp.split(proj, [nqk, 2 * nqk, 2 * nqk + nv], axis=-1)
    q = rotary(q.reshape(B, T, RET_HEADS, RET_DK), pos)
    k = rotary(k.reshape(B, T, RET_HEADS, RET_DK), pos) * (RET_DK ** -0.5)
    v = v.reshape(B, T, RET_HEADS, RET_DV).astype(jnp.float32)
    log_gamma = jnp.log1p(-jnp.power(2.0, -5.0 - jnp.arange(RET_HEADS, dtype=jnp.float32)))
    o, S = retention_chunked(q, k, v, log_gamma, S0)
    mu = jnp.mean(o, axis=-1, keepdims=True)
    var = jnp.mean(jnp.square(o - mu), axis=-1, keepdims=True)
    o = (o - mu) * lax.rsqrt(var + EPS) * norm_g.reshape(RET_HEADS, RET_DV).astype(jnp.float32)
    o = o.reshape(B, T, nv).astype(h.dtype) * jax.nn.silu(g)
    return o @ w_out, S.astype(S0.dtype)


def gla_mixer(h, S0, w_in, w_a1, w_a2, b_a, norm_g, w_out):
    B, T, _ = h.shape
    nqk, nv = GLA_HEADS * GLA_DK, GLA_HEADS * GLA_DV
    proj = h @ w_in
    q, k, v, r = jnp.split(proj, [nqk, 2 * nqk, 2 * nqk + nv], axis=-1)
    q = q.reshape(B, T, GLA_HEADS, GLA_DK).astype(jnp.float32) * (GLA_DK ** -0.5)
    k = k.reshape(B, T, GLA_HEADS, GLA_DK).astype(jnp.float32)
    v = v.reshape(B, T, GLA_HEADS, GLA_DV).astype(jnp.float32)
    z = ((h @ w_a1) @ w_a2 + b_a).astype(jnp.float32)
    log_a = (jax.nn.log_sigmoid(z) / GLA_TAU).reshape(B, T, GLA_HEADS, GLA_DK)
    o, S = gla_chunked(q, k, v, log_a, S0)
    o = o * lax.rsqrt(jnp.mean(o * o, axis=-1, keepdims=True) + EPS)
    o = o * norm_g.reshape(GLA_HEADS, GLA_DV).astype(jnp.float32)
    o = o.reshape(B, T, nv).astype(h.dtype) * jax.nn.silu(r)
    return o @ w_out, S.astype(S0.dtype)


def swiglu(x, wg, wu, wd):
    return (jax.nn.silu(x @ wg) * (x @ wu)) @ wd


def moe_ffn(h, w_router, router_bias, w_gate, w_up, w_down, ws_gate, ws_up, ws_down):
    Bsz, T, D = h.shape
    x = h.reshape(-1, D)
    N = x.shape[0]
    scores = jax.nn.sigmoid((x @ w_router).astype(jnp.float32))
    biased = scores + router_bias.astype(jnp.float32)
    grp_score = lax.top_k(biased.reshape(N, N_GROUPS, -1), 2)[0].sum(-1)
    _, top_g = lax.top_k(grp_score, TOPK_GROUPS)
    gmask = jnp.any(top_g[..., None] == jnp.arange(N_GROUPS)[None, None, :], axis=1)
    emask = jnp.repeat(gmask, N_EXPERTS // N_GROUPS, axis=1)
    _, idx = lax.top_k(jnp.where(emask, biased, -jnp.inf), TOP_K)
    wsel = jnp.take_along_axis(scores, idx, axis=1)
    wsel = wsel / jnp.sum(wsel, axis=-1, keepdims=True) * ROUTED_SCALE
    NK = N * TOP_K
    e_flat = idx.reshape(-1)
    tok_flat = jnp.repeat(jnp.arange(N), TOP_K)
    w_flat = wsel.reshape(-1)
    order = jnp.argsort(e_flat)
    e_s, tok_s, w_s = e_flat[order], tok_flat[order], w_flat[order]
    counts = jnp.bincount(e_flat, length=N_EXPERTS)
    padded = (counts + MOE_BLOCK - 1) // MOE_BLOCK * MOE_BLOCK
    start = jnp.cumsum(counts) - counts
    pend = jnp.cumsum(padded)
    pstart = pend - padded
    dest = pstart[e_s] + jnp.arange(NK) - start[e_s]
    cap = -(-(NK + N_EXPERTS * (MOE_BLOCK - 1)) // MOE_BLOCK) * MOE_BLOCK
    nb = cap // MOE_BLOCK
    xbuf = jnp.zeros((cap, D), x.dtype).at[dest].set(x[tok_s])
    block_expert = jnp.minimum(
        jnp.searchsorted(pend, jnp.arange(nb) * MOE_BLOCK, side='right'), N_EXPERTS - 1)

    def expert_block(args):
        xb, e = args
        return swiglu(xb, w_gate[e], w_up[e], w_down[e])

    ybuf = lax.map(expert_block, (xbuf.reshape(nb, MOE_BLOCK, D), block_expert)).reshape(cap, D)
    routed = jnp.zeros_like(x).at[tok_s].add(ybuf[dest] * w_s[:, None].astype(x.dtype))
    shared = swiglu(x, ws_gate, ws_up, ws_down)
    return (routed + shared).reshape(Bsz, T, D)


def trunk(x, c, pos0, s_ret, s_gla,
          ret_w_in, ret_norm_g, ret_w_out,
          gla_w_in, gla_w_a1, gla_w_a2, gla_b_a, gla_norm_g, gla_w_out,
          ada_w, ada_b, norm_mix_g, norm_ffn_g,
          moe_w_router, moe_router_bias, moe_w_gate, moe_w_up, moe_w_down,
          sh_w_gate, sh_w_up, sh_w_down, final_norm_g):
    T = x.shape[1]
    pos = pos0 + jnp.arange(T, dtype=jnp.int32)
    new_ret, new_gla = [], []
    for i in range(DEPTH):
        mod = jax.nn.silu(c) @ ada_w[i] + ada_b[i]
        sh1, sc1, g1, sh2, sc2, g2 = jnp.split(mod, 6, axis=-1)
        h = modulate(x, norm_mix_g[i], sh1, sc1)
        j = i // N_MIXERS
        if i % N_MIXERS == 0:
            m, S = retention_mixer(h, pos, s_ret[j], ret_w_in[j], ret_norm_g[j], ret_w_out[j])
            new_ret.append(S)
        else:
            m, S = gla_mixer(h, s_gla[j], gla_w_in[j], gla_w_a1[j], gla_w_a2[j],
                             gla_b_a[j], gla_norm_g[j], gla_w_out[j])
            new_gla.append(S)
        x = x + g1[:, None, :] * m
        h = modulate(x, norm_ffn_g[i], sh2, sc2)
        f = moe_ffn(h, moe_w_router[i], moe_router_bias[i], moe_w_gate[i], moe_w_up[i],
                    moe_w_down[i], sh_w_gate[i], sh_w_up[i], sh_w_down[i])
        x = x + g2[:, None, :] * f
    return rmsnorm(x, final_norm_g), jnp.stack(new_ret), jnp.stack(new_gla)


def setup_inputs(seed: int = 0) -> dict:
    key = jax.random.key(seed)
    ks = iter(jax.random.split(key, 40))

    def nrm(shape, scale):
        return jax.random.normal(next(ks), shape, jnp.float32) * scale

    def gain(shape):
        return 1.0 + nrm(shape, 0.02)

    D = D_MODEL
    ret_w = 2 * RET_HEADS * RET_DK + 2 * RET_HEADS * RET_DV
    gla_w = 2 * GLA_HEADS * GLA_DK + 2 * GLA_HEADS * GLA_DV
    return {
        "x_prompt": nrm((BATCH, SEQ, D), 1.0),
        "x_sample": nrm((DEC_BATCH, DEC_SEQ, D), 1.0),
        "state_ret": nrm((N_RET_LAYERS, DEC_BATCH, RET_HEADS, RET_DK, RET_DV), 0.1),
        "state_gla": nrm((N_GLA_LAYERS, DEC_BATCH, GLA_HEADS, GLA_DK, GLA_DV), 0.1),
        "c_prompt": nrm((BATCH, D), 1.0),
        "c_sample": nrm((DEC_BATCH, D), 1.0),
        "ret_w_in": nrm((N_RET_LAYERS, D, ret_w), D ** -0.5),
        "ret_norm_g": gain((N_RET_LAYERS, RET_HEADS * RET_DV)),
        "ret_w_out": nrm((N_RET_LAYERS, RET_HEADS * RET_DV, D), (RET_HEADS * RET_DV) ** -0.5),
        "gla_w_in": nrm((N_GLA_LAYERS, D, gla_w), D ** -0.5),
        "gla_w_a1": nrm((N_GLA_LAYERS, D, GLA_RANK), D ** -0.5),
        "gla_w_a2": nrm((N_GLA_LAYERS, GLA_RANK, GLA_HEADS * GLA_DK), GLA_RANK ** -0.5),
        "gla_b_a": nrm((N_GLA_LAYERS, GLA_HEADS * GLA_DK), 0.1),
        "gla_norm_g": gain((N_GLA_LAYERS, GLA_HEADS * GLA_DV)),
        "gla_w_out": nrm((N_GLA_LAYERS, GLA_HEADS * GLA_DV, D), (GLA_HEADS * GLA_DV) ** -0.5),
        "ada_w": nrm((DEPTH, D, 6 * D), 0.5 * D ** -0.5),
        "ada_b": nrm((DEPTH, 6 * D), 0.02),
        "norm_mix_g": gain((DEPTH, D)),
        "norm_ffn_g": gain((DEPTH, D)),
        "moe_w_router": nrm((DEPTH, D, N_EXPERTS), D ** -0.5),
        "moe_router_bias": nrm((DEPTH, N_EXPERTS), 0.01),
        "moe_w_gate": nrm((DEPTH, N_EXPERTS, D, D_EXPERT), D ** -0.5),
        "moe_w_up": nrm((DEPTH, N_EXPERTS, D, D_EXPERT), D ** -0.5),
        "moe_w_down": nrm((DEPTH, N_EXPERTS, D_EXPERT, D), D_EXPERT ** -0.5),
        "sh_w_gate": nrm((DEPTH, D, D_SHARED), D ** -0.5),
        "sh_w_up": nrm((DEPTH, D, D_SHARED), D ** -0.5),
        "sh_w_down": nrm((DEPTH, D_SHARED, D), D_SHARED ** -0.5),
        "final_norm_g": gain((D,)),
    }


def reference(x_prompt, x_sample, state_ret, state_gla, c_prompt, c_sample,
              ret_w_in, ret_norm_g, ret_w_out,
              gla_w_in, gla_w_a1, gla_w_a2, gla_b_a, gla_norm_g, gla_w_out,
              ada_w, ada_b, norm_mix_g, norm_ffn_g,
              moe_w_router, moe_router_bias, moe_w_gate, moe_w_up, moe_w_down,
              sh_w_gate, sh_w_up, sh_w_down, final_norm_g):
    weights = (ret_w_in, ret_norm_g, ret_w_out,
               gla_w_in, gla_w_a1, gla_w_a2, gla_b_a, gla_norm_g, gla_w_out,
               ada_w, ada_b, norm_mix_g, norm_ffn_g,
               moe_w_router, moe_router_bias, moe_w_gate, moe_w_up, moe_w_down,
               sh_w_gate, sh_w_up, sh_w_down, final_norm_g)
    b = x_prompt.shape[0]
    s_ret0 = jnp.zeros((N_RET_LAYERS, b, RET_HEADS, RET_DK, RET_DV), state_ret.dtype)
    s_gla0 = jnp.zeros((N_GLA_LAYERS, b, GLA_HEADS, GLA_DK, GLA_DV), state_gla.dtype)
    y_prompt, ret_prompt, gla_prompt = trunk(x_prompt, c_prompt, 0, s_ret0, s_gla0, *weights)
    y_sample, ret_sample, gla_sample = trunk(x_sample, c_sample, PAST_LEN, state_ret, state_gla,
                                             *weights)
    return (y_prompt, y_sample, ret_prompt, gla_prompt, ret_sample, gla_sample)
```

```python
import functools

import jax
import jax.numpy as jnp
from jax import lax
from jax.experimental import pallas as pl
from jax.experimental.pallas import tpu as pltpu

F32, BF16, I32 = jnp.float32, jnp.bfloat16, jnp.int32

EPS = 1e-6
ROPE_BASE = 10000.0
PAST_LEN = 16384
RET_HEADS = 4
GLA_HEADS = 4
GLA_TAU = 16.0
N_GROUPS = 8
TOPK_GROUPS = 4
TOP_K = 8
ROUTED_SCALE = 2.5

LANES = 128
BF16_SUBLANES = 16
VMEM_LIMIT_BYTES = 56 * 1024 * 1024

ROW_TILE = 256
RET_CHUNK = 256
GLA_CHUNK = 128
GLA_KEY_BLOCK = 16
STEP_TOKENS = 8
RUN_ALIGN = BF16_SUBLANES


def _cparams(*sem):
    return pltpu.CompilerParams(dimension_semantics=sem, vmem_limit_bytes=VMEM_LIMIT_BYTES)


def _sigmoid(x):
    return 1.0 / (1.0 + jnp.exp(-x))


def _silu(x):
    return x * _sigmoid(x)


def _modulate(x, g, shift, scale):
    y = x * lax.rsqrt(jnp.mean(x * x, axis=-1, keepdims=True) + EPS) * g
    return y * (1.0 + scale) + shift


def _dot(a, b):
    return jnp.dot(a, b, preferred_element_type=F32)


def _dot_nt(a, b):
    return lax.dot_general(a, b, (((1,), (1,)), ((), ())), preferred_element_type=F32)


def _dot_tn(a, b):
    return lax.dot_general(a, b, (((0,), (0,)), ((), ())), preferred_element_type=F32)


def _resident(shape):
    zeros = (0,) * len(shape)
    return pl.BlockSpec(shape, lambda *_: zeros, pipeline_mode=pl.Buffered(1))


class _Rows:
    def __init__(self, n_rows, seq_len, tile):
        self.n, self.tile = n_rows, tile
        self.per_row = seq_len == 1
        self.tiles_per_seq = max(seq_len // tile, 1)
        assert n_rows % tile == 0 and (self.per_row or seq_len % tile == 0)
        self.grid = n_rows // tile

    def rows(self, width, col=0):
        return pl.BlockSpec((self.tile, width), lambda i, *_: (i, col))

    def mod(self, d, col):
        if self.per_row:
            return pl.BlockSpec((self.tile, d), lambda i, *_: (i, col))
        tps = self.tiles_per_seq
        return pl.BlockSpec((None, 1, d), lambda i, *_: (i // tps, 0, col))


def _ada_kernel(c_ref, w_ref, b_ref, o_ref):
    s = _silu(c_ref[...]).astype(BF16)
    o_ref[0] = _dot(s, w_ref[0].astype(BF16)) + b_ref[0]


def _ada(c_all, ada_w, ada_b):
    depth, d, d6 = ada_w.shape
    n = c_all.shape[0]
    tn = d6 // 4
    return pl.pallas_call(
        _ada_kernel,
        grid=(depth, d6 // tn),
        in_specs=[pl.BlockSpec((n, d), lambda l, j: (0, 0)),
                  pl.BlockSpec((1, d, tn), lambda l, j: (l, 0, j)),
                  pl.BlockSpec((1, 1, tn), lambda l, j: (l, 0, j))],
        out_specs=pl.BlockSpec((1, n, tn), lambda l, j: (l, 0, j)),
        out_shape=jax.ShapeDtypeStruct((depth, n, d6), F32),
        compiler_params=_cparams("parallel", "parallel"),
        name="ada_mod",
    )(c_all, ada_w, ada_b.reshape(depth, 1, d6))


def _ret_proj_kernel(x_ref, sh_ref, sc_ref, g_ref, w_ref, cos_ref, sin_ref,
                     q_ref, k_ref, v_ref, gate_ref, *, n_heads, dk, dv):
    h = _modulate(x_ref[...], g_ref[...], sh_ref[...], sc_ref[...]).astype(BF16)
    cos, sin = cos_ref[...], sin_ref[...]
    half, nqk, nv = dk // 2, n_heads * dk, n_heads * dv
    for hd in range(n_heads):
        for dst, base, scale in ((q_ref, 0, None), (k_ref, nqk, dk ** -0.5)):
            p = _dot(h, w_ref[:, base + hd * dk:base + (hd + 1) * dk])
            x1, x2 = p[:, :half], p[:, half:]
            r1, r2 = x1 * cos - x2 * sin, x1 * sin + x2 * cos
            if scale is not None:
                r1, r2 = r1 * scale, r2 * scale
            dst[:, hd * dk:hd * dk + half] = r1.astype(BF16)
            dst[:, hd * dk + half:(hd + 1) * dk] = r2.astype(BF16)
    for hd in range(n_heads):
        v_ref[:, hd * dv:(hd + 1) * dv] = _dot(
            h, w_ref[:, 2 * nqk + hd * dv:2 * nqk + (hd + 1) * dv]).astype(BF16)
        gate_ref[:, hd * dv:(hd + 1) * dv] = _dot(
            h, w_ref[:, 2 * nqk + nv + hd * dv:2 * nqk + nv + (hd + 1) * dv])


def _ret_proj(x, mod, rows, norm_g, w_in_bf16, cos, sin, n_heads, dk, dv):
    n, d = x.shape
    nqk, nv = n_heads * dk, n_heads * dv
    half = dk // 2
    tps = rows.tiles_per_seq
    trig = (pl.BlockSpec((rows.tile, half), lambda i: (0, 0)) if rows.per_row
            else pl.BlockSpec((rows.tile, half), lambda i: (i % tps, 0)))
    return pl.pallas_call(
        functools.partial(_ret_proj_kernel, n_heads=n_heads, dk=dk, dv=dv),
        grid=(rows.grid,),
        in_specs=[rows.rows(d), rows.mod(d, 0), rows.mod(d, 1), _resident((1, d)),
                  _resident(w_in_bf16.shape), trig, trig],
        out_specs=[rows.rows(nqk), rows.rows(nqk), rows.rows(nv), rows.rows(nv)],
        out_shape=[jax.ShapeDtypeStruct((n, nqk), BF16), jax.ShapeDtypeStruct((n, nqk), BF16),
                   jax.ShapeDtypeStruct((n, nv), BF16), jax.ShapeDtypeStruct((n, nv), F32)],
        compiler_params=_cparams("parallel"),
        name="ret_proj",
    )(x, mod, mod, norm_g.reshape(1, d), w_in_bf16, cos, sin)


def _ret_core_kernel(lg_ref, q_ref, k_ref, v_ref, o_ref, s_out_ref, s_scr, *, tc, n_chunks):
    hd, c = pl.program_id(1), pl.program_id(2)

    @pl.when(c == 0)
    def _():
        s_scr[...] = jnp.zeros_like(s_scr)

    lg = lg_ref[hd]
    q, k, v = q_ref[...], k_ref[...], v_ref[...]
    ii = lax.broadcasted_iota(I32, (tc, tc), 0)
    jj = lax.broadcasted_iota(I32, (tc, tc), 1)
    dec = jnp.where(ii >= jj, jnp.exp((ii - jj).astype(F32) * lg), 0.0)
    att = (_dot_nt(q, k) * dec).astype(BF16)
    row = lax.broadcasted_iota(I32, (tc, 1), 0).astype(F32)
    s_old = s_scr[...]
    o_ref[...] = _dot(att, v) + jnp.exp((row + 1.0) * lg) * _dot(q, s_old.astype(BF16))
    kd = (k.astype(F32) * jnp.exp((tc - 1.0 - row) * lg)).astype(BF16)
    s_new = jnp.exp(jnp.full((1, 1), float(tc), F32) * lg) * s_old + _dot_tn(kd, v)
    s_scr[...] = s_new

    @pl.when(c == n_chunks - 1)
    def _():
        s_out_ref[0, 0] = s_new


def _ret_core(q, k, v, log_gamma, batch, seq, n_heads, dk, dv):
    tc = min(RET_CHUNK, seq)
    nc = seq // tc
    assert seq % tc == 0
    n = batch * seq
    return pl.pallas_call(
        functools.partial(_ret_core_kernel, tc=tc, n_chunks=nc),
        grid=(batch, n_heads, nc),
        in_specs=[pl.BlockSpec(memory_space=pltpu.SMEM),
                  pl.BlockSpec((tc, dk), lambda b, h, c: (b * nc + c, h)),
                  pl.BlockSpec((tc, dk), lambda b, h, c: (b * nc + c, h)),
                  pl.BlockSpec((tc, dv), lambda b, h, c: (b * nc + c, h))],
        out_specs=[pl.BlockSpec((tc, dv), lambda b, h, c: (b * nc + c, h)),
                   pl.BlockSpec((1, 1, dk, dv), lambda b, h, c: (b, h, 0, 0))],
        out_shape=[jax.ShapeDtypeStruct((n, n_heads * dv), F32),
                   jax.ShapeDtypeStruct((batch, n_heads, dk, dv), F32)],
        scratch_shapes=[pltpu.VMEM((dk, dv), F32)],
        compiler_params=_cparams("parallel", "parallel", "arbitrary"),
        name="ret_core",
    )(log_gamma, q, k, v)


def _step_kernel(s_ref, q_ref, k_ref, a_ref, v_ref, o_ref, s_out_ref, *, tb):
    for j in range(tb):
        s_new = (a_ref[0, 0, :, j:j + 1] * s_ref[j, 0]
                 + k_ref[0, 0, :, j:j + 1] * v_ref[j:j + 1, :])
        s_out_ref[j, 0] = s_new
        o_ref[j:j + 1, :] = jnp.sum(q_ref[0, 0, :, j:j + 1] * s_new, axis=0, keepdims=True)


def _columns(x, n_heads, dk, tb):
    n = x.shape[0]
    return x.astype(F32).reshape(n // tb, tb, n_heads, dk).transpose(2, 0, 3, 1)


def _step(state, q, k, a, v, n_heads, dk, dv):
    n = state.shape[0]
    tb = STEP_TOKENS
    assert n % tb == 0
    col = pl.BlockSpec((1, 1, dk, tb), lambda i, h: (h, i, 0, 0))
    st = pl.BlockSpec((tb, 1, dk, dv), lambda i, h: (i, h, 0, 0))
    row = pl.BlockSpec((tb, dv), lambda i, h: (i, h))
    return pl.pallas_call(
        functools.partial(_step_kernel, tb=tb),
        grid=(n // tb, n_heads),
        in_specs=[st, col, col, col, row],
        out_specs=[row, st],
        out_shape=[jax.ShapeDtypeStruct((n, n_heads * dv), F32),
                   jax.ShapeDtypeStruct(state.shape, state.dtype)],
        compiler_params=_cparams("parallel", "parallel"),
        name="state_step",
    )(state, _columns(q, n_heads, dk, tb), _columns(k, n_heads, dk, tb),
      _columns(a, n_heads, dk, tb), v.astype(F32))


def _gla_proj_kernel(x_ref, sh_ref, sc_ref, g_ref, w_ref, a1_ref, a2_ref, ba_ref,
                     q_ref, k_ref, v_ref, r_ref, la_ref, *, nqk, nv):
    h = _modulate(x_ref[...], g_ref[...], sh_ref[...], sc_ref[...]).astype(BF16)
    q_ref[...] = _dot(h, w_ref[:, :nqk]) * ((nqk // GLA_HEADS) ** -0.5)
    k_ref[...] = _dot(h, w_ref[:, nqk:2 * nqk])
    v_ref[...] = _dot(h, w_ref[:, 2 * nqk:2 * nqk + nv]).astype(BF16)
    r_ref[...] = _dot(h, w_ref[:, 2 * nqk + nv:])
    z = _dot(_dot(h, a1_ref[...]).astype(BF16), a2_ref[...]) + ba_ref[...]
    la_ref[...] = (jnp.minimum(z, 0.0) - jnp.log1p(jnp.exp(-jnp.abs(z)))) / GLA_TAU


def _gla_proj(x, mod, rows, norm_g, w_in_bf16, a1_bf16, a2_bf16, b_a, nqk, nv):
    n, d = x.shape
    return pl.pallas_call(
        functools.partial(_gla_proj_kernel, nqk=nqk, nv=nv),
        grid=(rows.grid,),
        in_specs=[rows.rows(d), rows.mod(d, 0), rows.mod(d, 1), _resident((1, d)),
                  _resident(w_in_bf16.shape), _resident(a1_bf16.shape),
                  _resident(a2_bf16.shape), _resident((1, nqk))],
        out_specs=[rows.rows(nqk), rows.rows(nqk), rows.rows(nv), rows.rows(nv), rows.rows(nqk)],
        out_shape=[jax.ShapeDtypeStruct((n, nqk), F32), jax.ShapeDtypeStruct((n, nqk), F32),
                   jax.ShapeDtypeStruct((n, nv), BF16), jax.ShapeDtypeStruct((n, nv), F32),
                   jax.ShapeDtypeStruct((n, nqk), F32)],
        compiler_params=_cparams("parallel"),
        name="gla_proj",
    )(x, mod, mod, norm_g.reshape(1, d), w_in_bf16, a1_bf16, a2_bf16, b_a.reshape(1, nqk))


def _gla_core_kernel(q_ref, k_ref, v_ref, la_ref, o_ref, s_out_ref, s_scr, *, tc, n_chunks):
    c = pl.program_id(2)

    @pl.when(c == 0)
    def _():
        s_scr[...] = jnp.zeros_like(s_scr)

    ii = lax.broadcasted_iota(I32, (tc, tc), 0)
    jj = lax.broadcasted_iota(I32, (tc, tc), 1)
    la = la_ref[...]
    la_hi = la.astype(BF16)
    la_mid = (la - la_hi.astype(F32)).astype(BF16)
    la_lo = (la - la_hi.astype(F32) - la_mid.astype(F32)).astype(BF16)
    tri = jnp.where(ii >= jj, 1.0, 0.0).astype(BF16)
    b = _dot(tri, la_hi) + _dot(tri, la_mid) + _dot(tri, la_lo)
    q, k, v = q_ref[...], k_ref[...], v_ref[...]
    s_old = s_scr[...]
    o = _dot((q * jnp.exp(b)).astype(BF16), s_old.astype(BF16))
    qi = lax.broadcasted_iota(I32, (tc, GLA_KEY_BLOCK), 0)
    kj = lax.broadcasted_iota(I32, (tc, GLA_KEY_BLOCK), 1)
    for s in range(tc // GLA_KEY_BLOCK):
        lo = s * GLA_KEY_BLOCK
        blk = slice(lo, lo + GLA_KEY_BLOCK)
        ref = b[lo - 1:lo, :] if s else jnp.zeros_like(b[0:1, :])
        qs = (q * jnp.exp(jnp.minimum(b - ref, 0.0))).astype(BF16)
        ks = (k[blk, :] * jnp.exp(ref - b[blk, :])).astype(BF16)
        att = jnp.where(qi >= kj + lo, _dot_nt(qs, ks), 0.0).astype(BF16)
        o = o + _dot(att, v[blk, :])
    o_ref[...] = o
    tot = b[tc - 1:tc, :]
    kd = (k * jnp.exp(tot - b)).astype(BF16)
    ones = jnp.ones((tc, LANES), BF16)
    tot_col = (_dot_tn(la_hi, ones) + _dot_tn(la_mid, ones) + _dot_tn(la_lo, ones))[:, 0:1]
    s_new = jnp.exp(tot_col) * s_old + _dot_tn(kd, v)
    s_scr[...] = s_new

    @pl.when(c == n_chunks - 1)
    def _():
        s_out_ref[0, 0] = s_new


def _gla_core(q, k, v, la, batch, seq, n_heads, dk, dv):
    tc = min(GLA_CHUNK, seq)
    nc = seq // tc
    assert seq % tc == 0
    n = batch * seq
    qk = pl.BlockSpec((tc, dk), lambda b, h, c: (b * nc + c, h))
    vv = pl.BlockSpec((tc, dv), lambda b, h, c: (b * nc + c, h))
    return pl.pallas_call(
        functools.partial(_gla_core_kernel, tc=tc, n_chunks=nc),
        grid=(batch, n_heads, nc),
        in_specs=[qk, qk, vv, qk],
        out_specs=[vv, pl.BlockSpec((1, 1, dk, dv), lambda b, h, c: (b, h, 0, 0))],
        out_shape=[jax.ShapeDtypeStruct((n, n_heads * dv), F32),
                   jax.ShapeDtypeStruct((batch, n_heads, dk, dv), F32)],
        scratch_shapes=[pltpu.VMEM((dk, dv), F32)],
        compiler_params=_cparams("parallel", "parallel", "arbitrary"),
        name="gla_core",
    )(q, k, v, la)


def _post_kernel(o_ref, gate_ref, x_ref, g1_ref, ng_ref, w_ref, out_ref, *, n_heads, dv, center):
    acc = None
    for hd in range(n_heads):
        sl = slice(hd * dv, (hd + 1) * dv)
        o = o_ref[:, sl]
        if center:
            o = o - jnp.mean(o, axis=-1, keepdims=True)
        y = o * lax.rsqrt(jnp.mean(o * o, axis=-1, keepdims=True) + EPS) * ng_ref[:, sl]
        y = (y * _silu(gate_ref[:, sl])).astype(BF16)
        part = _dot(y, w_ref[sl, :])
        acc = part if acc is None else acc + part
    out_ref[...] = x_ref[...] + g1_ref[...] * acc


def _post(o, gate, x, mod, rows, norm_g, w_out_bf16, n_heads, dv, center):
    n, d = x.shape
    nv = n_heads * dv
    return pl.pallas_call(
        functools.partial(_post_kernel, n_heads=n_heads, dv=dv, center=center),
        grid=(rows.grid,),
        in_specs=[rows.rows(nv), rows.rows(nv), rows.rows(d), rows.mod(d, 2),
                  _resident((1, nv)), _resident(w_out_bf16.shape)],
        out_specs=rows.rows(d),
        out_shape=jax.ShapeDtypeStruct((n, d), F32),
        compiler_params=_cparams("parallel"),
        name="mixer_post",
    )(o, gate, x, mod, norm_g.reshape(1, nv), w_out_bf16)


def _first_max(vals, idx_iota, n):
    m = jnp.max(vals, axis=0, keepdims=True)
    i = jnp.min(jnp.where(vals == m, idx_iota, n), axis=0, keepdims=True)
    return m, i


def _router_kernel(x_ref, sh_ref, sc_ref, g_ref, wr_ref, b_ref,
                   h2_ref, slot_ref, wt_ref, cnt_ref):
    h2 = _modulate(x_ref[...], g_ref[...], sh_ref[...], sc_ref[...])
    h2_ref[...] = h2.astype(BF16)
    h_hi = h2.astype(BF16)
    h_lo = (h2 - h_hi.astype(F32)).astype(BF16)
    wr = wr_ref[...]
    w_hi = wr.astype(BF16)
    w_lo = (wr - w_hi.astype(F32)).astype(BF16)
    logits = _dot_nt(w_hi, h_hi) + _dot_nt(w_hi, h_lo) + _dot_nt(w_lo, h_hi)
    scores = _sigmoid(logits)
    biased = scores + b_ref[...]
    n_exp, win = biased.shape
    gsz = n_exp // N_GROUPS
    neg = -jnp.inf

    sub_g = lax.broadcasted_iota(I32, (gsz, win), 0)
    group_rows = []
    for g in range(N_GROUPS):
        blk = biased[g * gsz:(g + 1) * gsz, :]
        m1, i1 = _first_max(blk, sub_g, gsz)
        m2 = jnp.max(jnp.where(sub_g == i1, neg, blk), axis=0, keepdims=True)
        group_rows.append(m1 + m2)
    cur = jnp.concatenate(group_rows, axis=0)
    sub_n = lax.broadcasted_iota(I32, (N_GROUPS, win), 0)
    gmask = jnp.zeros((N_GROUPS, win), jnp.bool_)
    for _ in range(TOPK_GROUPS):
        _, i = _first_max(cur, sub_n, N_GROUPS)
        sel = sub_n == i
        gmask = gmask | sel
        cur = jnp.where(sel, neg, cur)
    emask = jnp.concatenate(
        [jnp.broadcast_to(gmask[g:g + 1, :], (gsz, win)) for g in range(N_GROUPS)], axis=0)
    masked = jnp.where(emask, biased, neg)

    sub_e = lax.broadcasted_iota(I32, (n_exp, win), 0)
    sels, picked = [], []
    for _ in range(TOP_K):
        _, i = _first_max(masked, sub_e, n_exp)
        sel = sub_e == i
        sels.append(sel)
        picked.append(jnp.sum(jnp.where(sel, scores, 0.0), axis=0, keepdims=True))
        masked = jnp.where(sel, neg, masked)
    total = picked[0]
    for p in picked[1:]:
        total = total + p
    wt_ref[...] = jnp.concatenate([p / total * ROUTED_SCALE for p in picked], axis=0)

    chosen = sels[0]
    for s in sels[1:]:
        chosen = chosen | s
    sel_b = jnp.where(chosen, 1.0, 0.0).astype(BF16)
    r = lax.broadcasted_iota(I32, (win, win), 0)
    c = lax.broadcasted_iota(I32, (win, win), 1)
    before = jnp.where(r < c, 1.0, 0.0).astype(BF16)
    rank = _dot(sel_b, before)
    count = _dot(sel_b, jnp.ones((win, win), BF16))
    run_len = jnp.floor((count + (RUN_ALIGN - 1.0)) * (1.0 / RUN_ALIGN)) * RUN_ALIGN
    er = lax.broadcasted_iota(I32, (n_exp, n_exp), 0)
    ec = lax.broadcasted_iota(I32, (n_exp, n_exp), 1)
    lower = jnp.where(ec < er, 1.0, 0.0).astype(BF16)
    slot_all = _dot(lower, run_len.astype(BF16)) + rank
    slot_ref[...] = jnp.concatenate(
        [jnp.sum(jnp.where(s, slot_all, 0.0), axis=0, keepdims=True) for s in sels],
        axis=0).astype(I32)
    cnt_ref[0] = _dot_nt(jnp.ones((8, win), BF16), sel_b)


def _router(x, mod, rows, norm_g, w_router, bias):
    n, d = x.shape
    n_exp = w_router.shape[1]
    win = rows.tile
    lanes = pl.BlockSpec((TOP_K, win), lambda i: (0, i))
    return pl.pallas_call(
        _router_kernel,
        grid=(rows.grid,),
        in_specs=[rows.rows(d), rows.mod(d, 3), rows.mod(d, 4), _resident((1, d)),
                  _resident((n_exp, d)), _resident((n_exp, 1))],
        out_specs=[rows.rows(d), lanes, lanes, pl.BlockSpec((1, 8, n_exp), lambda i: (i, 0, 0))],
        out_shape=[jax.ShapeDtypeStruct((n, d), BF16), jax.ShapeDtypeStruct((TOP_K, n), I32),
                   jax.ShapeDtypeStruct((TOP_K, n), F32),
                   jax.ShapeDtypeStruct((rows.grid, 8, n_exp), F32)],
        compiler_params=_cparams("parallel"),
        name="moe_router",
    )(x, mod, mod, norm_g.reshape(1, d), w_router.T, bias.reshape(n_exp, 1))


def _sorted_rows(win, n_exp):
    return -(-(win * TOP_K + n_exp * (RUN_ALIGN - 1)) // 256) * 256


def _plan(cnt, tile):
    c = cnt[:, 0, :].astype(I32)
    n_win, n_exp = c.shape
    run = (c + RUN_ALIGN - 1) // RUN_ALIGN * RUN_ALIGN
    per_exp = jnp.sum(run, axis=0)
    padded = (per_exp + tile - 1) // tile * tile
    pend = jnp.cumsum(padded)
    pstart = pend - padded
    dest = pstart[None, :] + jnp.cumsum(run, axis=0) - run
    local = jnp.cumsum(run, axis=1) - run
    return dict(
        chunks=(run // RUN_ALIGN).reshape(-1), local=local.reshape(-1), dest=dest.reshape(-1),
        pad_start=pstart + per_exp, pad_chunks=(padded - per_exp) // RUN_ALIGN,
        pend=pend, used_tiles=(pend[-1] // tile).reshape(1))


def _capacity(n, n_win, n_exp, tile):
    worst = n * TOP_K + n_win * n_exp * (RUN_ALIGN - 1) + n_exp * (tile - RUN_ALIGN)
    return -(-worst // tile) * tile


def _run_copies(chunks_ref, src_ref, dst_ref, base, n_exp, make_copy, wait):
    def per_expert(e, carry):
        src0, dst0 = src_ref[base + e], dst_ref[base + e]

        def per_chunk(j, carry2):
            off = j * RUN_ALIGN
            cp = make_copy(pl.multiple_of(src0 + off, RUN_ALIGN), pl.multiple_of(dst0 + off, RUN_ALIGN))
            if wait:
                cp.wait()
            else:
                cp.start()
            return carry2

        return lax.fori_loop(0, chunks_ref[base + e], per_chunk, carry)

    lax.fori_loop(0, n_exp, per_expert, 0)


def _dispatch_kernel(chunks_ref, local_ref, dest_ref, pad_start_ref, pad_chunks_ref, used_ref,
                     h2_ref, slot_ref, xbuf_ref, xs_scr, zero_scr, sem, tail_sem,
                     *, n_exp, n_win, tile):
    w = pl.program_id(0)
    n_tail = xbuf_ref.shape[0] // tile - used_ref[0]

    def tail_copy(j):
        dst = pl.multiple_of((used_ref[0] + j) * tile, tile)
        return pltpu.make_async_copy(zero_scr, xbuf_ref.at[pl.ds(dst, tile)], tail_sem)

    @pl.when(w == 0)
    def _():
        zero_scr[...] = jnp.zeros_like(zero_scr)
        lax.fori_loop(0, n_tail, lambda j, c: (tail_copy(j).start(), c)[1], 0)

    slot = slot_ref[...]
    n_rows, win = xs_scr.shape[0], slot.shape[1]
    rows = lax.broadcasted_iota(I32, (n_rows, win), 0)
    hit = rows == slot[0:1, :]
    for k in range(1, TOP_K):
        hit = hit | (rows == slot[k:k + 1, :])
    onehot = jnp.where(hit, 1.0, 0.0).astype(BF16)
    xs_scr[...] = _dot(onehot, h2_ref[...]).astype(BF16)

    def copy(src, dst):
        return pltpu.make_async_copy(xs_scr.at[pl.ds(src, RUN_ALIGN)],
                                     xbuf_ref.at[pl.ds(dst, RUN_ALIGN)], sem)

    _run_copies(chunks_ref, local_ref, dest_ref, w * n_exp, n_exp, copy, wait=False)
    _run_copies(chunks_ref, local_ref, dest_ref, w * n_exp, n_exp, copy, wait=True)

    @pl.when(w == n_win - 1)
    def _():
        def pad_copy(e, j):
            dst = pl.multiple_of(pad_start_ref[e] + j * RUN_ALIGN, RUN_ALIGN)
            return pltpu.make_async_copy(zero_scr.at[pl.ds(0, RUN_ALIGN)],
                                         xbuf_ref.at[pl.ds(dst, RUN_ALIGN)], sem)

        def pads(wait):
            def per_expert(e, carry):
                def per_chunk(j, carry2):
                    if wait:
                        pad_copy(e, j).wait()
                    else:
                        pad_copy(e, j).start()
                    return carry2
                return lax.fori_loop(0, pad_chunks_ref[e], per_chunk, carry)
            lax.fori_loop(0, n_exp, per_expert, 0)

        pads(False)
        pads(True)
        lax.fori_loop(0, n_tail, lambda j, c: (tail_copy(j).wait(), c)[1], 0)


def _dispatch(h2, slot_t, plan, rows, n_exp, cap, tile):
    n, d = h2.shape
    win = rows.tile
    return pl.pallas_call(
        functools.partial(_dispatch_kernel, n_exp=n_exp, n_win=rows.grid, tile=tile),
        grid_spec=pltpu.PrefetchScalarGridSpec(
            num_scalar_prefetch=6,
            grid=(rows.grid,),
            in_specs=[rows.rows(d), pl.BlockSpec((TOP_K, win), lambda i, *_: (0, i))],
            out_specs=pl.BlockSpec(memory_space=pl.ANY),
            scratch_shapes=[pltpu.VMEM((_sorted_rows(win, n_exp), d), BF16),
                            pltpu.VMEM((tile, d), BF16),
                            pltpu.SemaphoreType.DMA(()), pltpu.SemaphoreType.DMA(())]),
        out_shape=jax.ShapeDtypeStruct((cap, d), BF16),
        compiler_params=_cparams("arbitrary"),
        name="moe_dispatch",
    )(plan["chunks"], plan["local"], plan["dest"], plan["pad_start"], plan["pad_chunks"],
      plan["used_tiles"], h2, slot_t)


def _experts_kernel(texp_ref, tmap_ref, used_ref, x_ref, wg_ref, wu_ref, wd_ref, y_ref):
    live = pl.program_id(0) < used_ref[0]

    @pl.when(live)
    def _():
        x = x_ref[...]
        a = _dot(x, wg_ref[...].astype(BF16))
        b = _dot(x, wu_ref[...].astype(BF16))
        y_ref[...] = _dot((_silu(a) * b).astype(BF16), wd_ref[...].astype(BF16)).astype(BF16)

    @pl.when(jnp.logical_not(live))
    def _():
        y_ref[...] = jnp.zeros_like(y_ref)


def _experts(xbuf, plan, w_gate, w_up, w_down, tile):
    cap, d = xbuf.shape
    n_exp, _, de = w_gate.shape
    n_tiles = cap // tile
    used = plan["used_tiles"]
    tmap = jnp.minimum(jnp.arange(n_tiles, dtype=I32), used[0] - 1)
    texp = jnp.minimum(jnp.searchsorted(plan["pend"], tmap * tile, side="right"), n_exp - 1).astype(I32)
    return pl.pallas_call(
        _experts_kernel,
        grid_spec=pltpu.PrefetchScalarGridSpec(
            num_scalar_prefetch=3,
            grid=(n_tiles,),
            in_specs=[pl.BlockSpec((tile, d), lambda i, te, tm, u: (tm[i], 0)),
                      pl.BlockSpec((None, d, de), lambda i, te, tm, u: (te[i], 0, 0)),
                      pl.BlockSpec((None, d, de), lambda i, te, tm, u: (te[i], 0, 0)),
                      pl.BlockSpec((None, de, d), lambda i, te, tm, u: (te[i], 0, 0))],
            out_specs=pl.BlockSpec((tile, d), lambda i, te, tm, u: (i, 0))),
        out_shape=jax.ShapeDtypeStruct((cap, d), BF16),
        compiler_params=_cparams("arbitrary"),
        name="moe_experts",
    )(texp, tmap, used, xbuf, w_gate, w_up, w_down)


def _combine_kernel(chunks_ref, local_ref, dest_ref, ybuf_ref, slot_ref, wt_ref, h2_ref, x_ref,
                    g2_ref, sg_ref, su_ref, sd_ref, fin_ref, out_ref, ys_scr, sem,
                    *, n_exp, final_norm):
    w = pl.program_id(0)

    @pl.when(w == 0)
    def _():
        ys_scr[...] = jnp.zeros_like(ys_scr)

    def copy(loc, dst):
        return pltpu.make_async_copy(ybuf_ref.at[pl.ds(dst, RUN_ALIGN)],
                                     ys_scr.at[pl.ds(loc, RUN_ALIGN)], sem)

    _run_copies(chunks_ref, local_ref, dest_ref, w * n_exp, n_exp, copy, wait=False)
    h2 = h2_ref[...]
    shared = _dot((_silu(_dot(h2, sg_ref[...])) * _dot(h2, su_ref[...])).astype(BF16), sd_ref[...])
    _run_copies(chunks_ref, local_ref, dest_ref, w * n_exp, n_exp, copy, wait=True)

    slot, wt = slot_ref[...], wt_ref[...]
    win, n_rows = slot.shape[0], ys_scr.shape[0]
    lanes = lax.broadcasted_iota(I32, (win, n_rows), 1)
    weights = jnp.where(lanes == slot[:, 0:1], wt[:, 0:1], 0.0)
    for k in range(1, TOP_K):
        weights = weights + jnp.where(lanes == slot[:, k:k + 1], wt[:, k:k + 1], 0.0)
    routed = _dot(weights.astype(BF16), ys_scr[...])
    out = x_ref[...] + g2_ref[...] * (routed + shared)
    if final_norm:
        out = out * lax.rsqrt(jnp.mean(out * out, axis=-1, keepdims=True) + EPS) * fin_ref[...]
    out_ref[...] = out


def _combine(ybuf, plan, slot, wt, h2, x, mod, rows, sh_bf16, final_g, n_exp, final_norm):
    n, d = x.shape
    win = rows.tile
    sg, su, sd = sh_bf16
    pairs = pl.BlockSpec((win, TOP_K), lambda i, *_: (i, 0))
    return pl.pallas_call(
        functools.partial(_combine_kernel, n_exp=n_exp, final_norm=final_norm),
        grid_spec=pltpu.PrefetchScalarGridSpec(
            num_scalar_prefetch=3,
            grid=(rows.grid,),
            in_specs=[pl.BlockSpec(memory_space=pl.ANY), pairs, pairs, rows.rows(d), rows.rows(d),
                      rows.mod(d, 5), _resident(sg.shape), _resident(su.shape),
                      _resident(sd.shape), _resident((1, d))],
            out_specs=rows.rows(d),
            scratch_shapes=[pltpu.VMEM((_sorted_rows(win, n_exp), d), BF16),
                            pltpu.SemaphoreType.DMA(())]),
        out_shape=jax.ShapeDtypeStruct((n, d), F32),
        compiler_params=_cparams("arbitrary"),
        name="moe_combine",
    )(plan["chunks"], plan["local"], plan["dest"], ybuf, slot, wt, h2, x, mod,
      sg, su, sd, final_g.reshape(1, d))


def _moe(x, mod, rows, norm_g, w_router, bias, w_gate, w_up, w_down, sh_bf16, final_g,
         final_norm, tile):
    n = x.shape[0]
    n_exp = w_router.shape[1]
    h2, slot_t, wt_t, cnt = _router(x, mod, rows, norm_g, w_router, bias)
    plan = _plan(cnt, tile)
    cap = _capacity(n, rows.grid, n_exp, tile)
    xbuf = _dispatch(h2, slot_t, plan, rows, n_exp, cap, tile)
    ybuf = _experts(xbuf, plan, w_gate, w_up, w_down, tile)
    return _combine(ybuf, plan, slot_t.T, wt_t.T, h2, x, mod, rows, sh_bf16, final_g,
                    n_exp, final_norm)


def _rope_tables(pos, half):
    inv = ROPE_BASE ** (-jnp.arange(half, dtype=F32) / half)
    ang = pos.astype(F32)[:, None] * inv[None, :]
    return jnp.cos(ang), jnp.sin(ang)


def _trunk(x3, mod_all, pos0, s_ret, s_gla, wts, expert_tile):
    batch, seq, d = x3.shape
    n = batch * seq
    x = x3.reshape(n, d)
    rows = _Rows(n, seq, min(ROW_TILE, n))
    decode = seq == 1
    ret_dk = d // RET_HEADS
    ret_dv = 2 * ret_dk
    gla_dk = d // (2 * GLA_HEADS)
    gla_dv = d // GLA_HEADS
    depth = wts["ada_w"].shape[0]
    new_ret, new_gla = [], []
    for layer in range(depth):
        mod = mod_all[layer] if decode else mod_all[layer].reshape(batch, 1, 6 * d)
        j = layer // 2
        if layer % 2 == 0:
            pos = (jnp.full((rows.tile,), pos0, I32) if decode
                   else pos0 + jnp.arange(seq, dtype=I32))
            cos, sin = _rope_tables(pos, ret_dk // 2)
            q, k, v, gate = _ret_proj(x, mod, rows, wts["norm_mix_g"][layer], wts["ret_w_in"][j],
                                      cos, sin, RET_HEADS, ret_dk, ret_dv)
            log_gamma = jnp.log1p(-jnp.power(2.0, -5.0 - jnp.arange(RET_HEADS, dtype=F32)))
            if decode:
                decay = jnp.broadcast_to(jnp.repeat(jnp.exp(log_gamma), ret_dk)[None, :],
                                         (n, RET_HEADS * ret_dk))
                o, s_new = _step(s_ret[j], q, k, decay, v, RET_HEADS, ret_dk, ret_dv)
            else:
                o, s_new = _ret_core(q, k, v, log_gamma, batch, seq, RET_HEADS, ret_dk, ret_dv)
            new_ret.append(s_new)
            x = _post(o, gate, x, mod, rows, wts["ret_norm_g"][j], wts["ret_w_out"][j],
                      RET_HEADS, ret_dv, center=True)
        else:
            q, k, v, gate, la = _gla_proj(x, mod, rows, wts["norm_mix_g"][layer], wts["gla_w_in"][j],
                                          wts["gla_w_a1"][j], wts["gla_w_a2"][j], wts["gla_b_a"][j],
                                          GLA_HEADS * gla_dk, GLA_HEADS * gla_dv)
            if decode:
                o, s_new = _step(s_gla[j], q, k, jnp.exp(la), v, GLA_HEADS, gla_dk, gla_dv)
            else:
                o, s_new = _gla_core(q, k, v, la, batch, seq, GLA_HEADS, gla_dk, gla_dv)
            new_gla.append(s_new)
            x = _post(o, gate, x, mod, rows, wts["gla_norm_g"][j], wts["gla_w_out"][j],
                      GLA_HEADS, gla_dv, center=False)
        x = _moe(x, mod, rows, wts["norm_ffn_g"][layer], wts["moe_w_router"][layer],
                 wts["moe_router_bias"][layer], wts["moe_w_gate"][layer], wts["moe_w_up"][layer],
                 wts["moe_w_down"][layer], wts["shared"][layer], wts["final_norm_g"],
                 final_norm=layer == depth - 1, tile=expert_tile)
    return x.reshape(batch, seq, d), jnp.stack(new_ret), jnp.stack(new_gla)


def kernel(x_prompt, x_sample, state_ret, state_gla, c_prompt, c_sample, ret_w_in, ret_norm_g, ret_w_out, gla_w_in, gla_w_a1, gla_w_a2, gla_b_a, gla_norm_g, gla_w_out, ada_w, ada_b, norm_mix_g, norm_ffn_g, moe_w_router, moe_router_bias, moe_w_gate, moe_w_up, moe_w_down, sh_w_gate, sh_w_up, sh_w_down, final_norm_g):
    b = x_prompt.shape[0]
    depth = ada_w.shape[0]
    rank = gla_w_a1.shape[-1]
    pad = LANES - rank
    wts = dict(
        ret_w_in=ret_w_in.astype(BF16), ret_norm_g=ret_norm_g, ret_w_out=ret_w_out.astype(BF16),
        gla_w_in=gla_w_in.astype(BF16),
        gla_w_a1=jnp.pad(gla_w_a1, ((0, 0), (0, 0), (0, pad))).astype(BF16),
        gla_w_a2=jnp.pad(gla_w_a2, ((0, 0), (0, pad), (0, 0))).astype(BF16),
        gla_b_a=gla_b_a, gla_norm_g=gla_norm_g, gla_w_out=gla_w_out.astype(BF16),
        ada_w=ada_w, norm_mix_g=norm_mix_g, norm_ffn_g=norm_ffn_g,
        moe_w_router=moe_w_router, moe_router_bias=moe_router_bias,
        moe_w_gate=moe_w_gate, moe_w_up=moe_w_up, moe_w_down=moe_w_down,
        shared=[(sh_w_gate[l].astype(BF16), sh_w_up[l].astype(BF16), sh_w_down[l].astype(BF16))
                for l in range(depth)],
        final_norm_g=final_norm_g)
    mod = _ada(jnp.concatenate([c_prompt, c_sample], axis=0), ada_w, ada_b)
    y_p, ret_p, gla_p = _trunk(x_prompt, mod[:, :b], 0, None, None, wts, expert_tile=256)
    y_s, ret_s, gla_s = _trunk(x_sample, mod[:, b:], PAST_LEN, state_ret, state_gla, wts,
                               expert_tile=128)
    return (y_p, y_s, ret_p, gla_p, ret_s, gla_s)
```

```python
import functools

import jax
import jax.numpy as jnp
from jax import lax
from jax.experimental import pallas as pl
from jax.experimental.pallas import tpu as pltpu

F32, BF16, I32 = jnp.float32, jnp.bfloat16, jnp.int32

EPS = 1e-6
ROPE_BASE = 10000.0
PAST_LEN = 16384
RET_HEADS = 4
GLA_HEADS = 4
GLA_TAU = 16.0
N_GROUPS = 8
TOPK_GROUPS = 4
TOP_K = 8
ROUTED_SCALE = 2.5

LANES = 128
BF16_SUBLANES = 16
VMEM_LIMIT_BYTES = 56 * 1024 * 1024

ROW_TILE = 256
RET_CHUNK = 256
GLA_CHUNK = 128
GLA_KEY_BLOCK = 16
STEP_TOKENS = 8
RUN_ALIGN = BF16_SUBLANES
EXPERT_SUB_TILE = 256


def _cparams(*sem):
    return pltpu.CompilerParams(dimension_semantics=sem, vmem_limit_bytes=VMEM_LIMIT_BYTES)


def _sigmoid(x):
    return 1.0 / (1.0 + jnp.exp(-x))


def _silu(x):
    return x * _sigmoid(x)


def _modulate(x, g, shift, scale):
    y = x * lax.rsqrt(jnp.mean(x * x, axis=-1, keepdims=True) + EPS) * g
    return y * (1.0 + scale) + shift


def _dot(a, b):
    return jnp.dot(a, b, preferred_element_type=F32)


def _dot_nt(a, b):
    return lax.dot_general(a, b, (((1,), (1,)), ((), ())), preferred_element_type=F32)


def _dot_tn(a, b):
    return lax.dot_general(a, b, (((0,), (0,)), ((), ())), preferred_element_type=F32)


def _resident(shape):
    zeros = (0,) * len(shape)
    return pl.BlockSpec(shape, lambda *_: zeros, pipeline_mode=pl.Buffered(1))


class _Rows:
    def __init__(self, n_rows, seq_len, tile):
        self.n, self.tile = n_rows, tile
        self.per_row = seq_len == 1
        self.tiles_per_seq = max(seq_len // tile, 1)
        assert n_rows % tile == 0 and (self.per_row or seq_len % tile == 0)
        self.grid = n_rows // tile

    def rows(self, width, col=0):
        return pl.BlockSpec((self.tile, width), lambda i, *_: (i, col))

    def mod(self, d, col):
        if self.per_row:
            return pl.BlockSpec((self.tile, d), lambda i, *_: (i, col))
        tps = self.tiles_per_seq
        return pl.BlockSpec((None, 1, d), lambda i, *_: (i // tps, 0, col))


def _ada_kernel(c_ref, w_ref, b_ref, o_ref):
    s = _silu(c_ref[...]).astype(BF16)
    o_ref[0] = _dot(s, w_ref[0].astype(BF16)) + b_ref[0]


def _ada(c_all, ada_w, ada_b):
    depth, d, d6 = ada_w.shape
    n = c_all.shape[0]
    tn = d6 // 4
    return pl.pallas_call(
        _ada_kernel,
        grid=(depth, d6 // tn),
        in_specs=[pl.BlockSpec((n, d), lambda l, j: (0, 0)),
                  pl.BlockSpec((1, d, tn), lambda l, j: (l, 0, j)),
                  pl.BlockSpec((1, 1, tn), lambda l, j: (l, 0, j))],
        out_specs=pl.BlockSpec((1, n, tn), lambda l, j: (l, 0, j)),
        out_shape=jax.ShapeDtypeStruct((depth, n, d6), F32),
        compiler_params=_cparams("parallel", "parallel"),
        name="ada_mod",
    )(c_all, ada_w, ada_b.reshape(depth, 1, d6))


def _ret_proj_kernel(x_ref, sh_ref, sc_ref, g_ref, w_ref, cos_ref, sin_ref,
                     q_ref, k_ref, v_ref, gate_ref, *, n_heads, dk, dv):
    h = _modulate(x_ref[...], g_ref[...], sh_ref[...], sc_ref[...]).astype(BF16)
    cos, sin = cos_ref[...], sin_ref[...]
    half, nqk, nv = dk // 2, n_heads * dk, n_heads * dv
    for hd in range(n_heads):
        for dst, base, scale in ((q_ref, 0, None), (k_ref, nqk, dk ** -0.5)):
            p = _dot(h, w_ref[:, base + hd * dk:base + (hd + 1) * dk])
            x1, x2 = p[:, :half], p[:, half:]
            r1, r2 = x1 * cos - x2 * sin, x1 * sin + x2 * cos
            if scale is not None:
                r1, r2 = r1 * scale, r2 * scale
            dst[:, hd * dk:hd * dk + half] = r1.astype(BF16)
            dst[:, hd * dk + half:(hd + 1) * dk] = r2.astype(BF16)
    for hd in range(n_heads):
        v_ref[:, hd * dv:(hd + 1) * dv] = _dot(
            h, w_ref[:, 2 * nqk + hd * dv:2 * nqk + (hd + 1) * dv]).astype(BF16)
        gate_ref[:, hd * dv:(hd + 1) * dv] = _dot(
            h, w_ref[:, 2 * nqk + nv + hd * dv:2 * nqk + nv + (hd + 1) * dv])


def _ret_proj(x, mod, rows, norm_g, w_in_bf16, cos, sin, n_heads, dk, dv):
    n, d = x.shape
    nqk, nv = n_heads * dk, n_heads * dv
    half = dk // 2
    tps = rows.tiles_per_seq
    trig = (pl.BlockSpec((rows.tile, half), lambda i: (0, 0)) if rows.per_row
            else pl.BlockSpec((rows.tile, half), lambda i: (i % tps, 0)))
    return pl.pallas_call(
        functools.partial(_ret_proj_kernel, n_heads=n_heads, dk=dk, dv=dv),
        grid=(rows.grid,),
        in_specs=[rows.rows(d), rows.mod(d, 0), rows.mod(d, 1), _resident((1, d)),
                  _resident(w_in_bf16.shape), trig, trig],
        out_specs=[rows.rows(nqk), rows.rows(nqk), rows.rows(nv), rows.rows(nv)],
        out_shape=[jax.ShapeDtypeStruct((n, nqk), BF16), jax.ShapeDtypeStruct((n, nqk), BF16),
                   jax.ShapeDtypeStruct((n, nv), BF16), jax.ShapeDtypeStruct((n, nv), F32)],
        compiler_params=_cparams("parallel"),
        name="ret_proj",
    )(x, mod, mod, norm_g.reshape(1, d), w_in_bf16, cos, sin)


def _ret_core_kernel(lg_ref, q_ref, k_ref, v_ref, o_ref, s_out_ref, s_scr, *, tc, n_chunks):
    hd, c = pl.program_id(1), pl.program_id(2)

    @pl.when(c == 0)
    def _():
        s_scr[...] = jnp.zeros_like(s_scr)

    lg = lg_ref[hd]
    q, k, v = q_ref[...], k_ref[...], v_ref[...]
    ii = lax.broadcasted_iota(I32, (tc, tc), 0)
    jj = lax.broadcasted_iota(I32, (tc, tc), 1)
    dec = jnp.where(ii >= jj, jnp.exp((ii - jj).astype(F32) * lg), 0.0)
    att = (_dot_nt(q, k) * dec).astype(BF16)
    row = lax.broadcasted_iota(I32, (tc, 1), 0).astype(F32)
    s_old = s_scr[...]
    o_ref[...] = _dot(att, v) + jnp.exp((row + 1.0) * lg) * _dot(q, s_old.astype(BF16))
    kd = (k.astype(F32) * jnp.exp((tc - 1.0 - row) * lg)).astype(BF16)
    s_new = jnp.exp(jnp.full((1, 1), float(tc), F32) * lg) * s_old + _dot_tn(kd, v)
    s_scr[...] = s_new

    @pl.when(c == n_chunks - 1)
    def _():
        s_out_ref[0, 0] = s_new


def _ret_core(q, k, v, log_gamma, batch, seq, n_heads, dk, dv):
    tc = min(RET_CHUNK, seq)
    nc = seq // tc
    assert seq % tc == 0
    n = batch * seq
    return pl.pallas_call(
        functools.partial(_ret_core_kernel, tc=tc, n_chunks=nc),
        grid=(batch, n_heads, nc),
        in_specs=[pl.BlockSpec(memory_space=pltpu.SMEM),
                  pl.BlockSpec((tc, dk), lambda b, h, c: (b * nc + c, h)),
                  pl.BlockSpec((tc, dk), lambda b, h, c: (b * nc + c, h)),
                  pl.BlockSpec((tc, dv), lambda b, h, c: (b * nc + c, h))],
        out_specs=[pl.BlockSpec((tc, dv), lambda b, h, c: (b * nc + c, h)),
                   pl.BlockSpec((1, 1, dk, dv), lambda b, h, c: (b, h, 0, 0))],
        out_shape=[jax.ShapeDtypeStruct((n, n_heads * dv), F32),
                   jax.ShapeDtypeStruct((batch, n_heads, dk, dv), F32)],
        scratch_shapes=[pltpu.VMEM((dk, dv), F32)],
        compiler_params=_cparams("parallel", "parallel", "arbitrary"),
        name="ret_core",
    )(log_gamma, q, k, v)


def _step_kernel(s_ref, q_ref, k_ref, a_ref, v_ref, o_ref, s_out_ref, *, tb):
    for j in range(tb):
        s_new = (a_ref[0, 0, :, j:j + 1] * s_ref[j, 0]
                 + k_ref[0, 0, :, j:j + 1] * v_ref[j:j + 1, :])
        s_out_ref[j, 0] = s_new
        o_ref[j:j + 1, :] = jnp.sum(q_ref[0, 0, :, j:j + 1] * s_new, axis=0, keepdims=True)


def _columns(x, n_heads, dk, tb):
    n = x.shape[0]
    return x.astype(F32).reshape(n // tb, tb, n_heads, dk).transpose(2, 0, 3, 1)


def _step(states, layer, q, k, a, v, n_heads, dk, dv):
    n = states.shape[1]
    tb = STEP_TOKENS
    assert n % tb == 0
    col = pl.BlockSpec((1, 1, dk, tb), lambda i, h: (h, i, 0, 0))
    st_in = pl.BlockSpec((None, tb, 1, dk, dv), lambda i, h: (layer, i, h, 0, 0))
    st_out = pl.BlockSpec((tb, 1, dk, dv), lambda i, h: (i, h, 0, 0))
    row = pl.BlockSpec((tb, dv), lambda i, h: (i, h))
    return pl.pallas_call(
        functools.partial(_step_kernel, tb=tb),
        grid=(n // tb, n_heads),
        in_specs=[st_in, col, col, col, row],
        out_specs=[row, st_out],
        out_shape=[jax.ShapeDtypeStruct((n, n_heads * dv), F32),
                   jax.ShapeDtypeStruct(states.shape[1:], states.dtype)],
        compiler_params=_cparams("parallel", "parallel"),
        name="state_step",
    )(states, _columns(q, n_heads, dk, tb), _columns(k, n_heads, dk, tb),
      _columns(a, n_heads, dk, tb), v.astype(F32))


def _gla_proj_kernel(x_ref, sh_ref, sc_ref, g_ref, w_ref, a1_ref, a2_ref, ba_ref,
                     q_ref, k_ref, v_ref, r_ref, la_ref, *, nqk, nv):
    h = _modulate(x_ref[...], g_ref[...], sh_ref[...], sc_ref[...]).astype(BF16)
    q_ref[...] = _dot(h, w_ref[:, :nqk]) * ((nqk // GLA_HEADS) ** -0.5)
    k_ref[...] = _dot(h, w_ref[:, nqk:2 * nqk])
    v_ref[...] = _dot(h, w_ref[:, 2 * nqk:2 * nqk + nv]).astype(BF16)
    r_ref[...] = _dot(h, w_ref[:, 2 * nqk + nv:])
    z = _dot(_dot(h, a1_ref[...]).astype(BF16), a2_ref[...]) + ba_ref[...]
    la_ref[...] = (jnp.minimum(z, 0.0) - jnp.log1p(jnp.exp(-jnp.abs(z)))) / GLA_TAU


def _gla_proj(x, mod, rows, norm_g, w_in_bf16, a1_bf16, a2_bf16, b_a, nqk, nv):
    n, d = x.shape
    return pl.pallas_call(
        functools.partial(_gla_proj_kernel, nqk=nqk, nv=nv),
        grid=(rows.grid,),
        in_specs=[rows.rows(d), rows.mod(d, 0), rows.mod(d, 1), _resident((1, d)),
                  _resident(w_in_bf16.shape), _resident(a1_bf16.shape),
                  _resident(a2_bf16.shape), _resident((1, nqk))],
        out_specs=[rows.rows(nqk), rows.rows(nqk), rows.rows(nv), rows.rows(nv), rows.rows(nqk)],
        out_shape=[jax.ShapeDtypeStruct((n, nqk), F32), jax.ShapeDtypeStruct((n, nqk), F32),
                   jax.ShapeDtypeStruct((n, nv), BF16), jax.ShapeDtypeStruct((n, nv), F32),
                   jax.ShapeDtypeStruct((n, nqk), F32)],
        compiler_params=_cparams("parallel"),
        name="gla_proj",
    )(x, mod, mod, norm_g.reshape(1, d), w_in_bf16, a1_bf16, a2_bf16, b_a.reshape(1, nqk))


def _gla_core_kernel(q_ref, k_ref, v_ref, la_ref, o_ref, s_out_ref, s_scr, *, tc, n_chunks):
    c = pl.program_id(2)

    @pl.when(c == 0)
    def _():
        s_scr[...] = jnp.zeros_like(s_scr)

    ii = lax.broadcasted_iota(I32, (tc, tc), 0)
    jj = lax.broadcasted_iota(I32, (tc, tc), 1)
    la = la_ref[...]
    la_hi = la.astype(BF16)
    la_mid = (la - la_hi.astype(F32)).astype(BF16)
    la_lo = (la - la_hi.astype(F32) - la_mid.astype(F32)).astype(BF16)
    tri = jnp.where(ii >= jj, 1.0, 0.0).astype(BF16)
    b = _dot(tri, la_hi) + _dot(tri, la_mid) + _dot(tri, la_lo)
    q, k, v = q_ref[...], k_ref[...], v_ref[...]
    s_old = s_scr[...]
    o = _dot((q * jnp.exp(b)).astype(BF16), s_old.astype(BF16))
    qi = lax.broadcasted_iota(I32, (tc, GLA_KEY_BLOCK), 0)
    kj = lax.broadcasted_iota(I32, (tc, GLA_KEY_BLOCK), 1)
    for s in range(tc // GLA_KEY_BLOCK):
        lo = s * GLA_KEY_BLOCK
        blk = slice(lo, lo + GLA_KEY_BLOCK)
        ref = b[lo - 1:lo, :] if s else jnp.zeros_like(b[0:1, :])
        qs = (q * jnp.exp(jnp.minimum(b - ref, 0.0))).astype(BF16)
        ks = (k[blk, :] * jnp.exp(ref - b[blk, :])).astype(BF16)
        att = jnp.where(qi >= kj + lo, _dot_nt(qs, ks), 0.0).astype(BF16)
        o = o + _dot(att, v[blk, :])
    o_ref[...] = o
    tot = b[tc - 1:tc, :]
    kd = (k * jnp.exp(tot - b)).astype(BF16)
    ones = jnp.ones((tc, LANES), BF16)
    tot_col = (_dot_tn(la_hi, ones) + _dot_tn(la_mid, ones) + _dot_tn(la_lo, ones))[:, 0:1]
    s_new = jnp.exp(tot_col) * s_old + _dot_tn(kd, v)
    s_scr[...] = s_new

    @pl.when(c == n_chunks - 1)
    def _():
        s_out_ref[0, 0] = s_new


def _gla_core(q, k, v, la, batch, seq, n_heads, dk, dv):
    tc = min(GLA_CHUNK, seq)
    nc = seq // tc
    assert seq % tc == 0
    n = batch * seq
    qk = pl.BlockSpec((tc, dk), lambda b, h, c: (b * nc + c, h))
    vv = pl.BlockSpec((tc, dv), lambda b, h, c: (b * nc + c, h))
    return pl.pallas_call(
        functools.partial(_gla_core_kernel, tc=tc, n_chunks=nc),
        grid=(batch, n_heads, nc),
        in_specs=[qk, qk, vv, qk],
        out_specs=[vv, pl.BlockSpec((1, 1, dk, dv), lambda b, h, c: (b, h, 0, 0))],
        out_shape=[jax.ShapeDtypeStruct((n, n_heads * dv), F32),
                   jax.ShapeDtypeStruct((batch, n_heads, dk, dv), F32)],
        scratch_shapes=[pltpu.VMEM((dk, dv), F32)],
        compiler_params=_cparams("parallel", "parallel", "arbitrary"),
        name="gla_core",
    )(q, k, v, la)


def _post_kernel(o_ref, gate_ref, x_ref, g1_ref, ng_ref, w_ref, out_ref, *, n_heads, dv, center):
    acc = None
    for hd in range(n_heads):
        sl = slice(hd * dv, (hd + 1) * dv)
        o = o_ref[:, sl]
        if center:
            o = o - jnp.mean(o, axis=-1, keepdims=True)
        y = o * lax.rsqrt(jnp.mean(o * o, axis=-1, keepdims=True) + EPS) * ng_ref[:, sl]
        y = (y * _silu(gate_ref[:, sl])).astype(BF16)
        part = _dot(y, w_ref[sl, :])
        acc = part if acc is None else acc + part
    out_ref[...] = x_ref[...] + g1_ref[...] * acc


def _post(o, gate, x, mod, rows, norm_g, w_out_bf16, n_heads, dv, center):
    n, d = x.shape
    nv = n_heads * dv
    return pl.pallas_call(
        functools.partial(_post_kernel, n_heads=n_heads, dv=dv, center=center),
        grid=(rows.grid,),
        in_specs=[rows.rows(nv), rows.rows(nv), rows.rows(d), rows.mod(d, 2),
                  _resident((1, nv)), _resident(w_out_bf16.shape)],
        out_specs=rows.rows(d),
        out_shape=jax.ShapeDtypeStruct((n, d), F32),
        compiler_params=_cparams("parallel"),
        name="mixer_post",
    )(o, gate, x, mod, norm_g.reshape(1, nv), w_out_bf16)


def _first_max(vals, idx_iota, n):
    m = jnp.max(vals, axis=0, keepdims=True)
    i = jnp.min(jnp.where(vals == m, idx_iota, n), axis=0, keepdims=True)
    return m, i


def _router_kernel(x_ref, sh_ref, sc_ref, g_ref, wr_ref, b_ref,
                   h2_ref, slot_ref, wt_ref, cnt_ref):
    h2 = _modulate(x_ref[...], g_ref[...], sh_ref[...], sc_ref[...])
    h2_ref[...] = h2.astype(BF16)
    h_hi = h2.astype(BF16)
    h_lo = (h2 - h_hi.astype(F32)).astype(BF16)
    wr = wr_ref[...]
    w_hi = wr.astype(BF16)
    w_lo = (wr - w_hi.astype(F32)).astype(BF16)
    logits = _dot_nt(w_hi, h_hi) + _dot_nt(w_hi, h_lo) + _dot_nt(w_lo, h_hi)
    scores = _sigmoid(logits)
    biased = scores + b_ref[...]
    n_exp, win = biased.shape
    gsz = n_exp // N_GROUPS
    neg = -jnp.inf

    sub_g = lax.broadcasted_iota(I32, (gsz, win), 0)
    group_rows = []
    for g in range(N_GROUPS):
        blk = biased[g * gsz:(g + 1) * gsz, :]
        m1, i1 = _first_max(blk, sub_g, gsz)
        m2 = jnp.max(jnp.where(sub_g == i1, neg, blk), axis=0, keepdims=True)
        group_rows.append(m1 + m2)
    cur = jnp.concatenate(group_rows, axis=0)
    sub_n = lax.broadcasted_iota(I32, (N_GROUPS, win), 0)
    gmask = jnp.zeros((N_GROUPS, win), jnp.bool_)
    for _ in range(TOPK_GROUPS):
        _, i = _first_max(cur, sub_n, N_GROUPS)
        sel = sub_n == i
        gmask = gmask | sel
        cur = jnp.where(sel, neg, cur)
    emask = jnp.concatenate(
        [jnp.broadcast_to(gmask[g:g + 1, :], (gsz, win)) for g in range(N_GROUPS)], axis=0)
    masked = jnp.where(emask, biased, neg)

    sub_e = lax.broadcasted_iota(I32, (n_exp, win), 0)
    sels, picked = [], []
    for _ in range(TOP_K):
        _, i = _first_max(masked, sub_e, n_exp)
        sel = sub_e == i
        sels.append(sel)
        picked.append(jnp.sum(jnp.where(sel, scores, 0.0), axis=0, keepdims=True))
        masked = jnp.where(sel, neg, masked)
    total = picked[0]
    for p in picked[1:]:
        total = total + p
    wt_ref[...] = jnp.concatenate([p / total * ROUTED_SCALE for p in picked], axis=0)

    chosen = sels[0]
    for s in sels[1:]:
        chosen = chosen | s
    sel_b = jnp.where(chosen, 1.0, 0.0).astype(BF16)
    r = lax.broadcasted_iota(I32, (win, win), 0)
    c = lax.broadcasted_iota(I32, (win, win), 1)
    before = jnp.where(r < c, 1.0, 0.0).astype(BF16)
    rank = _dot(sel_b, before)
    count = _dot(sel_b, jnp.ones((win, win), BF16))
    run_len = jnp.floor((count + (RUN_ALIGN - 1.0)) * (1.0 / RUN_ALIGN)) * RUN_ALIGN
    er = lax.broadcasted_iota(I32, (n_exp, n_exp), 0)
    ec = lax.broadcasted_iota(I32, (n_exp, n_exp), 1)
    lower = jnp.where(ec < er, 1.0, 0.0).astype(BF16)
    slot_all = _dot(lower, run_len.astype(BF16)) + rank
    slot_ref[...] = jnp.concatenate(
        [jnp.sum(jnp.where(s, slot_all, 0.0), axis=0, keepdims=True) for s in sels],
        axis=0).astype(I32)
    cnt_ref[0] = _dot_nt(jnp.ones((8, win), BF16), sel_b)


def _router(x, mod, rows, norm_g, w_router, bias):
    n, d = x.shape
    n_exp = w_router.shape[1]
    win = rows.tile
    lanes = pl.BlockSpec((TOP_K, win), lambda i: (0, i))
    return pl.pallas_call(
        _router_kernel,
        grid=(rows.grid,),
        in_specs=[rows.rows(d), rows.mod(d, 3), rows.mod(d, 4), _resident((1, d)),
                  _resident((n_exp, d)), _resident((n_exp, 1))],
        out_specs=[rows.rows(d), lanes, lanes, pl.BlockSpec((1, 8, n_exp), lambda i: (i, 0, 0))],
        out_shape=[jax.ShapeDtypeStruct((n, d), BF16), jax.ShapeDtypeStruct((TOP_K, n), I32),
                   jax.ShapeDtypeStruct((TOP_K, n), F32),
                   jax.ShapeDtypeStruct((rows.grid, 8, n_exp), F32)],
        compiler_params=_cparams("parallel"),
        name="moe_router",
    )(x, mod, mod, norm_g.reshape(1, d), w_router.T, bias.reshape(n_exp, 1))


def _sorted_rows(win, n_exp):
    return -(-(win * TOP_K + n_exp * (RUN_ALIGN - 1)) // 256) * 256


def _plan(cnt, tile):
    c = cnt[:, 0, :].astype(I32)
    n_win, n_exp = c.shape
    run = (c + RUN_ALIGN - 1) // RUN_ALIGN * RUN_ALIGN
    per_exp = jnp.sum(run, axis=0)
    padded = (per_exp + tile - 1) // tile * tile
    pend = jnp.cumsum(padded)
    pstart = pend - padded
    dest = pstart[None, :] + jnp.cumsum(run, axis=0) - run
    local = jnp.cumsum(run, axis=1) - run
    return dict(
        chunks=(run // RUN_ALIGN).reshape(-1), local=local.reshape(-1), dest=dest.reshape(-1),
        pad_start=pstart + per_exp, pad_chunks=(padded - per_exp) // RUN_ALIGN,
        pend=pend, used_tiles=(pend[-1] // tile).reshape(1))


def _capacity(n, n_win, n_exp, tile):
    worst = n * TOP_K + n_win * n_exp * (RUN_ALIGN - 1) + n_exp * (tile - RUN_ALIGN)
    return -(-worst // tile) * tile


def _run_copies(chunks_ref, src_ref, dst_ref, base, n_exp, make_copy, wait):
    def go(src, dst, n_rows):
        cp = make_copy(pl.multiple_of(src, RUN_ALIGN), pl.multiple_of(dst, RUN_ALIGN), n_rows)
        if wait:
            cp.wait()
        else:
            cp.start()

    def per_expert(e, carry):
        src0, dst0, n = src_ref[base + e], dst_ref[base + e], chunks_ref[base + e]

        def per_pair(j, carry2):
            go(src0 + j * (2 * RUN_ALIGN), dst0 + j * (2 * RUN_ALIGN), 2 * RUN_ALIGN)
            return carry2

        lax.fori_loop(0, lax.shift_right_logical(n, 1), per_pair, 0)

        @pl.when((n & 1) == 1)
        def _():
            go(src0 + (n - 1) * RUN_ALIGN, dst0 + (n - 1) * RUN_ALIGN, RUN_ALIGN)

        return carry

    lax.fori_loop(0, n_exp, per_expert, 0)


def _dispatch_kernel(chunks_ref, local_ref, dest_ref, pad_start_ref, pad_chunks_ref, used_ref,
                     h2_ref, slot_ref, xbuf_ref, xs_scr, zero_scr, sems, tail_sem,
                     *, n_exp, n_win, tile):
    w = pl.program_id(0)
    cur = w % 2
    n_tail = xbuf_ref.shape[0] // tile - used_ref[0]

    def tail_copy(j):
        dst = pl.multiple_of((used_ref[0] + j) * tile, tile)
        return pltpu.make_async_copy(zero_scr, xbuf_ref.at[pl.ds(dst, tile)], tail_sem)

    @pl.when(w == 0)
    def _():
        zero_scr[...] = jnp.zeros_like(zero_scr)
        lax.fori_loop(0, n_tail, lambda j, c: (tail_copy(j).start(), c)[1], 0)

    slot = slot_ref[...]
    n_rows, win = xs_scr.shape[1], slot.shape[1]
    rows = lax.broadcasted_iota(I32, (n_rows, win), 0)
    onehot = jnp.where(rows == slot[0:1, :], 1.0, 0.0)
    for k in range(1, TOP_K):
        onehot = jnp.where(rows == slot[k:k + 1, :], 1.0, onehot)
    xs_scr[cur] = _dot(onehot.astype(BF16), h2_ref[...]).astype(BF16)

    def copies(window, buf, wait):
        def copy(src, dst, n):
            return pltpu.make_async_copy(xs_scr.at[buf, pl.ds(src, n)],
                                         xbuf_ref.at[pl.ds(dst, n)], sems.at[buf])
        _run_copies(chunks_ref, local_ref, dest_ref, window * n_exp, n_exp, copy, wait)

    copies(w, cur, wait=False)

    @pl.when(w > 0)
    def _():
        copies(w - 1, 1 - cur, wait=True)

    @pl.when(w == n_win - 1)
    def _():
        copies(w, cur, wait=True)
        sem = sems.at[cur]

        def pad_copy(e, j):
            dst = pl.multiple_of(pad_start_ref[e] + j * RUN_ALIGN, RUN_ALIGN)
            return pltpu.make_async_copy(zero_scr.at[pl.ds(0, RUN_ALIGN)],
                                         xbuf_ref.at[pl.ds(dst, RUN_ALIGN)], sem)

        def pads(wait):
            def per_expert(e, carry):
                def per_chunk(j, carry2):
                    if wait:
                        pad_copy(e, j).wait()
                    else:
                        pad_copy(e, j).start()
                    return carry2
                return lax.fori_loop(0, pad_chunks_ref[e], per_chunk, carry)
            lax.fori_loop(0, n_exp, per_expert, 0)

        pads(False)
        pads(True)
        lax.fori_loop(0, n_tail, lambda j, c: (tail_copy(j).wait(), c)[1], 0)


def _dispatch(h2, slot_t, plan, rows, n_exp, cap, tile):
    n, d = h2.shape
    win = rows.tile
    return pl.pallas_call(
        functools.partial(_dispatch_kernel, n_exp=n_exp, n_win=rows.grid, tile=tile),
        grid_spec=pltpu.PrefetchScalarGridSpec(
            num_scalar_prefetch=6,
            grid=(rows.grid,),
            in_specs=[rows.rows(d), pl.BlockSpec((TOP_K, win), lambda i, *_: (0, i))],
            out_specs=pl.BlockSpec(memory_space=pl.ANY),
            scratch_shapes=[pltpu.VMEM((2, _sorted_rows(win, n_exp), d), BF16),
                            pltpu.VMEM((tile, d), BF16),
                            pltpu.SemaphoreType.DMA((2,)), pltpu.SemaphoreType.DMA(())]),
        out_shape=jax.ShapeDtypeStruct((cap, d), BF16),
        compiler_params=_cparams("arbitrary"),
        name="moe_dispatch",
    )(plan["chunks"], plan["local"], plan["dest"], plan["pad_start"], plan["pad_chunks"],
      plan["used_tiles"], h2, slot_t)


def _experts_kernel(texp_ref, tmap_ref, used_ref, x_ref, wg_ref, wu_ref, wd_ref, y_ref,
                    wgu_scr, wd_scr, *, sub_tiles):
    i = pl.program_id(0)
    live = i < used_ref[0]
    de = wd_scr.shape[0]

    @pl.when(jnp.logical_or(i == 0, texp_ref[i] != texp_ref[jnp.maximum(i - 1, 0)]))
    def _():
        wgu_scr[:, :de] = wg_ref[...].astype(BF16)
        wgu_scr[:, de:] = wu_ref[...].astype(BF16)
        wd_scr[...] = wd_ref[...].astype(BF16)

    @pl.when(live)
    def _():
        sub = x_ref.shape[0] // sub_tiles
        for s in range(sub_tiles):
            rows = slice(s * sub, (s + 1) * sub)
            ab = _dot(x_ref[rows, :], wgu_scr[...])
            mid = (_silu(ab[:, :de]) * ab[:, de:]).astype(BF16)
            y_ref[rows, :] = _dot(mid, wd_scr[...]).astype(BF16)

    @pl.when(jnp.logical_not(live))
    def _():
        y_ref[...] = jnp.zeros_like(y_ref)


def _experts(xbuf, plan, w_gate, w_up, w_down, tile):
    cap, d = xbuf.shape
    n_exp, _, de = w_gate.shape
    n_tiles = cap // tile
    used = plan["used_tiles"]
    tmap = jnp.minimum(jnp.arange(n_tiles, dtype=I32), used[0] - 1)
    texp = jnp.sum((plan["pend"][None, :] <= (tmap * tile)[:, None]).astype(I32), axis=1)
    texp = jnp.minimum(texp, n_exp - 1)
    return pl.pallas_call(
        functools.partial(_experts_kernel, sub_tiles=max(tile // EXPERT_SUB_TILE, 1)),
        grid_spec=pltpu.PrefetchScalarGridSpec(
            num_scalar_prefetch=3,
            grid=(n_tiles,),
            in_specs=[pl.BlockSpec((tile, d), lambda i, te, tm, u: (tm[i], 0)),
                      pl.BlockSpec((None, d, de), lambda i, te, tm, u: (te[i], 0, 0)),
                      pl.BlockSpec((None, d, de), lambda i, te, tm, u: (te[i], 0, 0)),
                      pl.BlockSpec((None, de, d), lambda i, te, tm, u: (te[i], 0, 0))],
            out_specs=pl.BlockSpec((tile, d), lambda i, te, tm, u: (i, 0)),
            scratch_shapes=[pltpu.VMEM((d, 2 * de), BF16), pltpu.VMEM((de, d), BF16)]),
        out_shape=jax.ShapeDtypeStruct((cap, d), BF16),
        compiler_params=_cparams("arbitrary"),
        name="moe_experts",
    )(texp, tmap, used, xbuf, w_gate, w_up, w_down)


def _combine_kernel(chunks_ref, local_ref, dest_ref, ybuf_ref, slot_ref, wt_ref, h2_ref, x_ref,
                    g2_ref, sg_ref, su_ref, sd_ref, fin_ref, out_ref, ys_scr, sems,
                    *, n_exp, n_win, final_norm):
    w = pl.program_id(0)
    cur = w % 2

    def copies(window, buf, wait):
        def copy(loc, dst, n):
            return pltpu.make_async_copy(ybuf_ref.at[pl.ds(dst, n)],
                                         ys_scr.at[buf, pl.ds(loc, n)], sems.at[buf])
        _run_copies(chunks_ref, local_ref, dest_ref, window * n_exp, n_exp, copy, wait)

    @pl.when(w == 0)
    def _():
        ys_scr[...] = jnp.zeros_like(ys_scr)
        copies(w, cur, wait=False)

    @pl.when(w + 1 < n_win)
    def _():
        copies(w + 1, 1 - cur, wait=False)

    h2 = h2_ref[...]
    shared = _dot((_silu(_dot(h2, sg_ref[...])) * _dot(h2, su_ref[...])).astype(BF16), sd_ref[...])
    copies(w, cur, wait=True)

    slot, wt = slot_ref[...], wt_ref[...]
    win, n_rows = slot.shape[0], ys_scr.shape[1]
    lanes = lax.broadcasted_iota(I32, (win, n_rows), 1)
    weights = jnp.where(lanes == slot[:, 0:1], wt[:, 0:1], 0.0)
    for k in range(1, TOP_K):
        weights = jnp.where(lanes == slot[:, k:k + 1], wt[:, k:k + 1], weights)
    routed = _dot(weights.astype(BF16), ys_scr[cur])
    out = x_ref[...] + g2_ref[...] * (routed + shared)
    if final_norm:
        out = out * lax.rsqrt(jnp.mean(out * out, axis=-1, keepdims=True) + EPS) * fin_ref[...]
    out_ref[...] = out


def _combine(ybuf, plan, slot, wt, h2, x, mod, rows, sh_bf16, final_g, n_exp, final_norm):
    n, d = x.shape
    win = rows.tile
    sg, su, sd = sh_bf16
    pairs = pl.BlockSpec((win, TOP_K), lambda i, *_: (i, 0))
    return pl.pallas_call(
        functools.partial(_combine_kernel, n_exp=n_exp, n_win=rows.grid, final_norm=final_norm),
        grid_spec=pltpu.PrefetchScalarGridSpec(
            num_scalar_prefetch=3,
            grid=(rows.grid,),
            in_specs=[pl.BlockSpec(memory_space=pl.ANY), pairs, pairs, rows.rows(d), rows.rows(d),
                      rows.mod(d, 5), _resident(sg.shape), _resident(su.shape),
                      _resident(sd.shape), _resident((1, d))],
            out_specs=rows.rows(d),
            scratch_shapes=[pltpu.VMEM((2, _sorted_rows(win, n_exp), d), BF16),
                            pltpu.SemaphoreType.DMA((2,))]),
        out_shape=jax.ShapeDtypeStruct((n, d), F32),
        compiler_params=_cparams("arbitrary"),
        name="moe_combine",
    )(plan["chunks"], plan["local"], plan["dest"], ybuf, slot, wt, h2, x, mod,
      sg, su, sd, final_g.reshape(1, d))


def _moe(x, mod, rows, norm_g, w_router, bias, w_gate, w_up, w_down, sh_bf16, final_g,
         final_norm, tile):
    n = x.shape[0]
    n_exp = w_router.shape[1]
    h2, slot_t, wt_t, cnt = _router(x, mod, rows, norm_g, w_router, bias)
    plan = _plan(cnt, tile)
    cap = _capacity(n, rows.grid, n_exp, tile)
    xbuf = _dispatch(h2, slot_t, plan, rows, n_exp, cap, tile)
    ybuf = _experts(xbuf, plan, w_gate, w_up, w_down, tile)
    return _combine(ybuf, plan, slot_t.T, wt_t.T, h2, x, mod, rows, sh_bf16, final_g,
                    n_exp, final_norm)


def _rope_tables(pos, half):
    inv = ROPE_BASE ** (-jnp.arange(half, dtype=F32) / half)
    ang = pos.astype(F32)[:, None] * inv[None, :]
    return jnp.cos(ang), jnp.sin(ang)


def _trunk(x3, mod_all, pos0, s_ret, s_gla, wts, expert_tile):
    batch, seq, d = x3.shape
    n = batch * seq
    x = x3.reshape(n, d)
    rows = _Rows(n, seq, min(ROW_TILE, n))
    decode = seq == 1
    ret_dk = d // RET_HEADS
    ret_dv = 2 * ret_dk
    gla_dk = d // (2 * GLA_HEADS)
    gla_dv = d // GLA_HEADS
    depth = wts["ada_w"].shape[0]
    new_ret, new_gla = [], []
    for layer in range(depth):
        mod = mod_all[layer] if decode else mod_all[layer].reshape(batch, 1, 6 * d)
        j = layer // 2
        if layer % 2 == 0:
            pos = (jnp.full((rows.tile,), pos0, I32) if decode
                   else pos0 + jnp.arange(seq, dtype=I32))
            cos, sin = _rope_tables(pos, ret_dk // 2)
            q, k, v, gate = _ret_proj(x, mod, rows, wts["norm_mix_g"][layer], wts["ret_w_in"][j],
                                      cos, sin, RET_HEADS, ret_dk, ret_dv)
            log_gamma = jnp.log1p(-jnp.power(2.0, -5.0 - jnp.arange(RET_HEADS, dtype=F32)))
            if decode:
                decay = jnp.broadcast_to(jnp.repeat(jnp.exp(log_gamma), ret_dk)[None, :],
                                         (n, RET_HEADS * ret_dk))
                o, s_new = _step(s_ret, j, q, k, decay, v, RET_HEADS, ret_dk, ret_dv)
            else:
                o, s_new = _ret_core(q, k, v, log_gamma, batch, seq, RET_HEADS, ret_dk, ret_dv)
            new_ret.append(s_new)
            x = _post(o, gate, x, mod, rows, wts["ret_norm_g"][j], wts["ret_w_out"][j],
                      RET_HEADS, ret_dv, center=True)
        else:
            q, k, v, gate, la = _gla_proj(x, mod, rows, wts["norm_mix_g"][layer], wts["gla_w_in"][j],
                                          wts["gla_w_a1"][j], wts["gla_w_a2"][j], wts["gla_b_a"][j],
                                          GLA_HEADS * gla_dk, GLA_HEADS * gla_dv)
            if decode:
                o, s_new = _step(s_gla, j, q, k, jnp.exp(la), v, GLA_HEADS, gla_dk, gla_dv)
            else:
                o, s_new = _gla_core(q, k, v, la, batch, seq, GLA_HEADS, gla_dk, gla_dv)
            new_gla.append(s_new)
            x = _post(o, gate, x, mod, rows, wts["gla_norm_g"][j], wts["gla_w_out"][j],
                      GLA_HEADS, gla_dv, center=False)
        x = _moe(x, mod, rows, wts["norm_ffn_g"][layer], wts["moe_w_router"][layer],
                 wts["moe_router_bias"][layer], wts["moe_w_gate"][layer], wts["moe_w_up"][layer],
                 wts["moe_w_down"][layer], wts["shared"][layer], wts["final_norm_g"],
                 final_norm=layer == depth - 1, tile=expert_tile)
    return x.reshape(batch, seq, d), jnp.stack(new_ret), jnp.stack(new_gla)


def kernel(x_prompt, x_sample, state_ret, state_gla, c_prompt, c_sample, ret_w_in, ret_norm_g, ret_w_out, gla_w_in, gla_w_a1, gla_w_a2, gla_b_a, gla_norm_g, gla_w_out, ada_w, ada_b, norm_mix_g, norm_ffn_g, moe_w_router, moe_router_bias, moe_w_gate, moe_w_up, moe_w_down, sh_w_gate, sh_w_up, sh_w_down, final_norm_g):
    b = x_prompt.shape[0]
    depth = ada_w.shape[0]
    rank = gla_w_a1.shape[-1]
    pad = LANES - rank
    wts = dict(
        ret_w_in=ret_w_in.astype(BF16), ret_norm_g=ret_norm_g, ret_w_out=ret_w_out.astype(BF16),
        gla_w_in=gla_w_in.astype(BF16),
        gla_w_a1=jnp.pad(gla_w_a1, ((0, 0), (0, 0), (0, pad))).astype(BF16),
        gla_w_a2=jnp.pad(gla_w_a2, ((0, 0), (0, pad), (0, 0))).astype(BF16),
        gla_b_a=gla_b_a, gla_norm_g=gla_norm_g, gla_w_out=gla_w_out.astype(BF16),
        ada_w=ada_w, norm_mix_g=norm_mix_g, norm_ffn_g=norm_ffn_g,
        moe_w_router=moe_w_router, moe_router_bias=moe_router_bias,
        moe_w_gate=moe_w_gate, moe_w_up=moe_w_up, moe_w_down=moe_w_down,
        shared=[(sh_w_gate[l].astype(BF16), sh_w_up[l].astype(BF16), sh_w_down[l].astype(BF16))
                for l in range(depth)],
        final_norm_g=final_norm_g)
    mod = _ada(jnp.concatenate([c_prompt, c_sample], axis=0), ada_w, ada_b)
    y_p, ret_p, gla_p = _trunk(x_prompt, mod[:, :b], 0, None, None, wts, expert_tile=512)
    y_s, ret_s, gla_s = _trunk(x_sample, mod[:, b:], PAST_LEN, state_ret, state_gla, wts,
                               expert_tile=128)
    return (y_p, y_s, ret_p, gla_p, ret_s, gla_s)
```

```python
import functools

import jax
import jax.numpy as jnp
from jax import lax
from jax.experimental import pallas as pl
from jax.experimental.pallas import tpu as pltpu

F32, BF16, I32 = jnp.float32, jnp.bfloat16, jnp.int32

EPS = 1e-6
ROPE_BASE = 10000.0
PAST_LEN = 16384
RET_HEADS = 4
GLA_HEADS = 4
GLA_TAU = 16.0
N_GROUPS = 8
TOPK_GROUPS = 4
TOP_K = 8
ROUTED_SCALE = 2.5

LANES = 128
BF16_SUBLANES = 16
VMEM_LIMIT_BYTES = 56 * 1024 * 1024

ROW_TILE = 256
RET_CHUNK = 256
GLA_CHUNK = 128
GLA_KEY_BLOCK = 16
STEP_TOKENS = 8
RUN_ALIGN = BF16_SUBLANES
EXPERT_SUB_TILE = 256
SORT_BLOCK = 512


def _cparams(*sem):
    return pltpu.CompilerParams(dimension_semantics=sem, vmem_limit_bytes=VMEM_LIMIT_BYTES)


def _sigmoid(x):
    return 1.0 / (1.0 + jnp.exp(-x))


def _silu(x):
    return x * _sigmoid(x)


def _modulate(x, g, shift, scale):
    y = x * lax.rsqrt(jnp.mean(x * x, axis=-1, keepdims=True) + EPS) * g
    return y * (1.0 + scale) + shift


def _dot(a, b):
    return jnp.dot(a, b, preferred_element_type=F32)


def _dot_nt(a, b):
    return lax.dot_general(a, b, (((1,), (1,)), ((), ())), preferred_element_type=F32)


def _dot_tn(a, b):
    return lax.dot_general(a, b, (((0,), (0,)), ((), ())), preferred_element_type=F32)


def _resident(shape):
    zeros = (0,) * len(shape)
    return pl.BlockSpec(shape, lambda *_: zeros, pipeline_mode=pl.Buffered(1))


class _Rows:
    def __init__(self, n_rows, seq_len, tile):
        self.n, self.tile = n_rows, tile
        self.per_row = seq_len == 1
        self.tiles_per_seq = max(seq_len // tile, 1)
        assert n_rows % tile == 0 and (self.per_row or seq_len % tile == 0)
        self.grid = n_rows // tile

    def rows(self, width, col=0):
        return pl.BlockSpec((self.tile, width), lambda i, *_: (i, col))

    def mod(self, d, col):
        if self.per_row:
            return pl.BlockSpec((self.tile, d), lambda i, *_: (i, col))
        tps = self.tiles_per_seq
        return pl.BlockSpec((None, 1, d), lambda i, *_: (i // tps, 0, col))


def _ada_kernel(c_ref, w_ref, b_ref, o_ref):
    s = _silu(c_ref[...]).astype(BF16)
    o_ref[0] = _dot(s, w_ref[0].astype(BF16)) + b_ref[0]


def _ada(c_all, ada_w, ada_b):
    depth, d, d6 = ada_w.shape
    n = c_all.shape[0]
    tn = d6 // 4
    return pl.pallas_call(
        _ada_kernel,
        grid=(depth, d6 // tn),
        in_specs=[pl.BlockSpec((n, d), lambda l, j: (0, 0)),
                  pl.BlockSpec((1, d, tn), lambda l, j: (l, 0, j)),
                  pl.BlockSpec((1, 1, tn), lambda l, j: (l, 0, j))],
        out_specs=pl.BlockSpec((1, n, tn), lambda l, j: (l, 0, j)),
        out_shape=jax.ShapeDtypeStruct((depth, n, d6), F32),
        compiler_params=_cparams("parallel", "parallel"),
        name="ada_mod",
    )(c_all, ada_w, ada_b.reshape(depth, 1, d6))


def _ret_proj_kernel(x_ref, sh_ref, sc_ref, g_ref, w_ref, cos_ref, sin_ref,
                     q_ref, k_ref, v_ref, gate_ref, *, n_heads, dk, dv):
    h = _modulate(x_ref[...], g_ref[...], sh_ref[...], sc_ref[...]).astype(BF16)
    cos, sin = cos_ref[...], sin_ref[...]
    half, nqk, nv = dk // 2, n_heads * dk, n_heads * dv
    for hd in range(n_heads):
        for dst, base, scale in ((q_ref, 0, None), (k_ref, nqk, dk ** -0.5)):
            p = _dot(h, w_ref[:, base + hd * dk:base + (hd + 1) * dk])
            x1, x2 = p[:, :half], p[:, half:]
            r1, r2 = x1 * cos - x2 * sin, x1 * sin + x2 * cos
            if scale is not None:
                r1, r2 = r1 * scale, r2 * scale
            dst[:, hd * dk:hd * dk + half] = r1.astype(BF16)
            dst[:, hd * dk + half:(hd + 1) * dk] = r2.astype(BF16)
    for hd in range(n_heads):
        v_ref[:, hd * dv:(hd + 1) * dv] = _dot(
            h, w_ref[:, 2 * nqk + hd * dv:2 * nqk + (hd + 1) * dv]).astype(BF16)
        gate_ref[:, hd * dv:(hd + 1) * dv] = _dot(
            h, w_ref[:, 2 * nqk + nv + hd * dv:2 * nqk + nv + (hd + 1) * dv])


def _ret_proj(x, mod, rows, norm_g, w_in_bf16, cos, sin, n_heads, dk, dv):
    n, d = x.shape
    nqk, nv = n_heads * dk, n_heads * dv
    half = dk // 2
    tps = rows.tiles_per_seq
    trig = (pl.BlockSpec((rows.tile, half), lambda i: (0, 0)) if rows.per_row
            else pl.BlockSpec((rows.tile, half), lambda i: (i % tps, 0)))
    return pl.pallas_call(
        functools.partial(_ret_proj_kernel, n_heads=n_heads, dk=dk, dv=dv),
        grid=(rows.grid,),
        in_specs=[rows.rows(d), rows.mod(d, 0), rows.mod(d, 1), _resident((1, d)),
                  _resident(w_in_bf16.shape), trig, trig],
        out_specs=[rows.rows(nqk), rows.rows(nqk), rows.rows(nv), rows.rows(nv)],
        out_shape=[jax.ShapeDtypeStruct((n, nqk), BF16), jax.ShapeDtypeStruct((n, nqk), BF16),
                   jax.ShapeDtypeStruct((n, nv), BF16), jax.ShapeDtypeStruct((n, nv), F32)],
        compiler_params=_cparams("parallel"),
        name="ret_proj",
    )(x, mod, mod, norm_g.reshape(1, d), w_in_bf16, cos, sin)


def _ret_core_kernel(lg_ref, q_ref, k_ref, v_ref, o_ref, s_out_ref, s_scr,
                     *, tc, n_chunks, n_heads, dk, dv):
    c = pl.program_id(1)

    @pl.when(c == 0)
    def _():
        s_scr[...] = jnp.zeros_like(s_scr)

    ii = lax.broadcasted_iota(I32, (tc, tc), 0)
    jj = lax.broadcasted_iota(I32, (tc, tc), 1)
    causal = ii >= jj
    lag = (ii - jj).astype(F32)
    row = lax.broadcasted_iota(I32, (tc, 1), 0).astype(F32)
    for hd in range(n_heads):
        lg = lg_ref[hd]
        q, k = q_ref[:, hd * dk:(hd + 1) * dk], k_ref[:, hd * dk:(hd + 1) * dk]
        v = v_ref[:, hd * dv:(hd + 1) * dv]
        dec = jnp.where(causal, jnp.exp(lag * lg), 0.0)
        att = (_dot_nt(q, k) * dec).astype(BF16)
        s_old = s_scr[hd]
        o_ref[:, hd * dv:(hd + 1) * dv] = (
            _dot(att, v) + jnp.exp((row + 1.0) * lg) * _dot(q, s_old.astype(BF16)))
        kd = (k.astype(F32) * jnp.exp((tc - 1.0 - row) * lg)).astype(BF16)
        s_scr[hd] = jnp.exp(jnp.full((1, 1), float(tc), F32) * lg) * s_old + _dot_tn(kd, v)

    @pl.when(c == n_chunks - 1)
    def _():
        s_out_ref[0] = s_scr[...]


def _ret_core(q, k, v, log_gamma, batch, seq, n_heads, dk, dv):
    tc = min(RET_CHUNK, seq)
    nc = seq // tc
    assert seq % tc == 0
    n = batch * seq
    qk = pl.BlockSpec((tc, n_heads * dk), lambda b, c: (b * nc + c, 0))
    vv = pl.BlockSpec((tc, n_heads * dv), lambda b, c: (b * nc + c, 0))
    return pl.pallas_call(
        functools.partial(_ret_core_kernel, tc=tc, n_chunks=nc, n_heads=n_heads, dk=dk, dv=dv),
        grid=(batch, nc),
        in_specs=[pl.BlockSpec(memory_space=pltpu.SMEM), qk, qk, vv],
        out_specs=[vv, pl.BlockSpec((1, n_heads, dk, dv), lambda b, c: (b, 0, 0, 0))],
        out_shape=[jax.ShapeDtypeStruct((n, n_heads * dv), F32),
                   jax.ShapeDtypeStruct((batch, n_heads, dk, dv), F32)],
        scratch_shapes=[pltpu.VMEM((n_heads, dk, dv), F32)],
        compiler_params=_cparams("parallel", "arbitrary"),
        name="ret_core",
    )(log_gamma, q, k, v)


def _step_kernel(s_ref, q_ref, k_ref, a_ref, v_ref, o_ref, s_out_ref, *, tb):
    for j in range(tb):
        s_new = (a_ref[0, 0, :, j:j + 1] * s_ref[j, 0]
                 + k_ref[0, 0, :, j:j + 1] * v_ref[j:j + 1, :])
        s_out_ref[j, 0] = s_new
        o_ref[j:j + 1, :] = jnp.sum(q_ref[0, 0, :, j:j + 1] * s_new, axis=0, keepdims=True)


def _columns(x, n_heads, dk, tb):
    n = x.shape[0]
    return x.astype(F32).reshape(n // tb, tb, n_heads, dk).transpose(2, 0, 3, 1)


def _step(states, layer, q, k, a, v, n_heads, dk, dv):
    n = states.shape[1]
    tb = STEP_TOKENS
    assert n % tb == 0
    col = pl.BlockSpec((1, 1, dk, tb), lambda i, h: (h, i, 0, 0))
    st_in = pl.BlockSpec((None, tb, 1, dk, dv), lambda i, h: (layer, i, h, 0, 0))
    st_out = pl.BlockSpec((tb, 1, dk, dv), lambda i, h: (i, h, 0, 0))
    row = pl.BlockSpec((tb, dv), lambda i, h: (i, h))
    return pl.pallas_call(
        functools.partial(_step_kernel, tb=tb),
        grid=(n // tb, n_heads),
        in_specs=[st_in, col, col, col, row],
        out_specs=[row, st_out],
        out_shape=[jax.ShapeDtypeStruct((n, n_heads * dv), F32),
                   jax.ShapeDtypeStruct(states.shape[1:], states.dtype)],
        compiler_params=_cparams("parallel", "parallel"),
        name="state_step",
    )(states, _columns(q, n_heads, dk, tb), _columns(k, n_heads, dk, tb),
      _columns(a, n_heads, dk, tb), v.astype(F32))


def _gla_proj_kernel(x_ref, sh_ref, sc_ref, g_ref, w_ref, a1_ref, a2_ref, ba_ref,
                     q_ref, k_ref, v_ref, r_ref, la_ref, *, nqk, nv):
    h = _modulate(x_ref[...], g_ref[...], sh_ref[...], sc_ref[...]).astype(BF16)
    q_ref[...] = _dot(h, w_ref[:, :nqk]) * ((nqk // GLA_HEADS) ** -0.5)
    k_ref[...] = _dot(h, w_ref[:, nqk:2 * nqk])
    v_ref[...] = _dot(h, w_ref[:, 2 * nqk:2 * nqk + nv]).astype(BF16)
    r_ref[...] = _dot(h, w_ref[:, 2 * nqk + nv:])
    z = _dot(_dot(h, a1_ref[...]).astype(BF16), a2_ref[...]) + ba_ref[...]
    la_ref[...] = (jnp.minimum(z, 0.0) - jnp.log1p(jnp.exp(-jnp.abs(z)))) / GLA_TAU


def _gla_proj(x, mod, rows, norm_g, w_in_bf16, a1_bf16, a2_bf16, b_a, nqk, nv):
    n, d = x.shape
    return pl.pallas_call(
        functools.partial(_gla_proj_kernel, nqk=nqk, nv=nv),
        grid=(rows.grid,),
        in_specs=[rows.rows(d), rows.mod(d, 0), rows.mod(d, 1), _resident((1, d)),
                  _resident(w_in_bf16.shape), _resident(a1_bf16.shape),
                  _resident(a2_bf16.shape), _resident((1, nqk))],
        out_specs=[rows.rows(nqk), rows.rows(nqk), rows.rows(nv), rows.rows(nv), rows.rows(nqk)],
        out_shape=[jax.ShapeDtypeStruct((n, nqk), F32), jax.ShapeDtypeStruct((n, nqk), F32),
                   jax.ShapeDtypeStruct((n, nv), BF16), jax.ShapeDtypeStruct((n, nv), F32),
                   jax.ShapeDtypeStruct((n, nqk), F32)],
        compiler_params=_cparams("parallel"),
        name="gla_proj",
    )(x, mod, mod, norm_g.reshape(1, d), w_in_bf16, a1_bf16, a2_bf16, b_a.reshape(1, nqk))


def _gla_core_kernel(q_ref, k_ref, v_ref, la_ref, o_ref, s_out_ref, s_scr,
                     *, tc, n_chunks, n_heads, dk, dv):
    c = pl.program_id(1)

    @pl.when(c == 0)
    def _():
        s_scr[...] = jnp.zeros_like(s_scr)

    ii = lax.broadcasted_iota(I32, (tc, tc), 0)
    jj = lax.broadcasted_iota(I32, (tc, tc), 1)
    la = la_ref[...]
    la_hi = la.astype(BF16)
    la_mid = (la - la_hi.astype(F32)).astype(BF16)
    la_lo = (la - la_hi.astype(F32) - la_mid.astype(F32)).astype(BF16)
    tri = jnp.where(ii >= jj, 1.0, 0.0).astype(BF16)
    b_all = _dot(tri, la_hi) + _dot(tri, la_mid) + _dot(tri, la_lo)
    ones = jnp.ones((tc, LANES), BF16)
    tot_col_all = (_dot_tn(la_hi, ones) + _dot_tn(la_mid, ones) + _dot_tn(la_lo, ones))[:, 0:1]
    qi = lax.broadcasted_iota(I32, (tc, GLA_KEY_BLOCK), 0)
    kj = lax.broadcasted_iota(I32, (tc, GLA_KEY_BLOCK), 1)
    for hd in range(n_heads):
        cols = slice(hd * dk, (hd + 1) * dk)
        b = b_all[:, cols]
        q, k, v = q_ref[:, cols], k_ref[:, cols], v_ref[:, hd * dv:(hd + 1) * dv]
        s_old = s_scr[hd]
        o = _dot((q * jnp.exp(b)).astype(BF16), s_old.astype(BF16))
        for s in range(tc // GLA_KEY_BLOCK):
            lo = s * GLA_KEY_BLOCK
            blk = slice(lo, lo + GLA_KEY_BLOCK)
            ref = b[lo - 1:lo, :] if s else jnp.zeros_like(b[0:1, :])
            qs = (q * jnp.exp(jnp.minimum(b - ref, 0.0))).astype(BF16)
            ks = (k[blk, :] * jnp.exp(ref - b[blk, :])).astype(BF16)
            att = jnp.where(qi >= kj + lo, _dot_nt(qs, ks), 0.0).astype(BF16)
            o = o + _dot(att, v[blk, :])
        o_ref[:, hd * dv:(hd + 1) * dv] = o
        kd = (k * jnp.exp(b[tc - 1:tc, :] - b)).astype(BF16)
        s_scr[hd] = jnp.exp(tot_col_all[cols, :]) * s_old + _dot_tn(kd, v)

    @pl.when(c == n_chunks - 1)
    def _():
        s_out_ref[0] = s_scr[...]


def _gla_core(q, k, v, la, batch, seq, n_heads, dk, dv):
    tc = min(GLA_CHUNK, seq)
    nc = seq // tc
    assert seq % tc == 0
    n = batch * seq
    qk = pl.BlockSpec((tc, n_heads * dk), lambda b, c: (b * nc + c, 0))
    vv = pl.BlockSpec((tc, n_heads * dv), lambda b, c: (b * nc + c, 0))
    return pl.pallas_call(
        functools.partial(_gla_core_kernel, tc=tc, n_chunks=nc, n_heads=n_heads, dk=dk, dv=dv),
        grid=(batch, nc),
        in_specs=[qk, qk, vv, qk],
        out_specs=[vv, pl.BlockSpec((1, n_heads, dk, dv), lambda b, c: (b, 0, 0, 0))],
        out_shape=[jax.ShapeDtypeStruct((n, n_heads * dv), F32),
                   jax.ShapeDtypeStruct((batch, n_heads, dk, dv), F32)],
        scratch_shapes=[pltpu.VMEM((n_heads, dk, dv), F32)],
        compiler_params=_cparams("parallel", "arbitrary"),
        name="gla_core",
    )(q, k, v, la)


def _post_kernel(o_ref, gate_ref, x_ref, g1_ref, ng_ref, w_ref, out_ref, *, n_heads, dv, center):
    acc = None
    for hd in range(n_heads):
        sl = slice(hd * dv, (hd + 1) * dv)
        o = o_ref[:, sl]
        if center:
            o = o - jnp.mean(o, axis=-1, keepdims=True)
        y = o * lax.rsqrt(jnp.mean(o * o, axis=-1, keepdims=True) + EPS) * ng_ref[:, sl]
        y = (y * _silu(gate_ref[:, sl])).astype(BF16)
        part = _dot(y, w_ref[sl, :])
        acc = part if acc is None else acc + part
    out_ref[...] = x_ref[...] + g1_ref[...] * acc


def _post(o, gate, x, mod, rows, norm_g, w_out_bf16, n_heads, dv, center):
    n, d = x.shape
    nv = n_heads * dv
    return pl.pallas_call(
        functools.partial(_post_kernel, n_heads=n_heads, dv=dv, center=center),
        grid=(rows.grid,),
        in_specs=[rows.rows(nv), rows.rows(nv), rows.rows(d), rows.mod(d, 2),
                  _resident((1, nv)), _resident(w_out_bf16.shape)],
        out_specs=rows.rows(d),
        out_shape=jax.ShapeDtypeStruct((n, d), F32),
        compiler_params=_cparams("parallel"),
        name="mixer_post",
    )(o, gate, x, mod, norm_g.reshape(1, nv), w_out_bf16)


def _first_max(vals, idx_iota, n):
    m = jnp.max(vals, axis=0, keepdims=True)
    i = jnp.min(jnp.where(vals == m, idx_iota, n), axis=0, keepdims=True)
    return m, i


def _router_kernel(x_ref, sh_ref, sc_ref, g_ref, wr_ref, b_ref,
                   h2_ref, slot_ref, wt_ref, cnt_ref):
    h2 = _modulate(x_ref[...], g_ref[...], sh_ref[...], sc_ref[...])
    h2_ref[...] = h2.astype(BF16)
    h_hi = h2.astype(BF16)
    h_lo = (h2 - h_hi.astype(F32)).astype(BF16)
    wr = wr_ref[...]
    w_hi = wr.astype(BF16)
    w_lo = (wr - w_hi.astype(F32)).astype(BF16)
    logits = _dot_nt(w_hi, h_hi) + _dot_nt(w_hi, h_lo) + _dot_nt(w_lo, h_hi)
    scores = _sigmoid(logits)
    biased = scores + b_ref[...]
    n_exp, win = biased.shape
    gsz = n_exp // N_GROUPS
    neg = -jnp.inf

    sub_g = lax.broadcasted_iota(I32, (gsz, win), 0)
    group_rows = []
    for g in range(N_GROUPS):
        blk = biased[g * gsz:(g + 1) * gsz, :]
        m1, i1 = _first_max(blk, sub_g, gsz)
        m2 = jnp.max(jnp.where(sub_g == i1, neg, blk), axis=0, keepdims=True)
        group_rows.append(m1 + m2)
    cur = jnp.concatenate(group_rows, axis=0)
    sub_n = lax.broadcasted_iota(I32, (N_GROUPS, win), 0)
    gmask = jnp.zeros((N_GROUPS, win), jnp.bool_)
    for _ in range(TOPK_GROUPS):
        _, i = _first_max(cur, sub_n, N_GROUPS)
        sel = sub_n == i
        gmask = gmask | sel
        cur = jnp.where(sel, neg, cur)
    emask = jnp.concatenate(
        [jnp.broadcast_to(gmask[g:g + 1, :], (gsz, win)) for g in range(N_GROUPS)], axis=0)
    masked = jnp.where(emask, biased, neg)

    sub_e = lax.broadcasted_iota(I32, (n_exp, win), 0)
    sels, picked = [], []
    for _ in range(TOP_K):
        _, i = _first_max(masked, sub_e, n_exp)
        sel = sub_e == i
        sels.append(sel)
        picked.append(jnp.sum(jnp.where(sel, scores, 0.0), axis=0, keepdims=True))
        masked = jnp.where(sel, neg, masked)
    total = picked[0]
    for p in picked[1:]:
        total = total + p
    wt_ref[...] = jnp.concatenate([p / total * ROUTED_SCALE for p in picked], axis=0)

    chosen = sels[0]
    for s in sels[1:]:
        chosen = chosen | s
    sel_b = jnp.where(chosen, 1.0, 0.0).astype(BF16)
    r = lax.broadcasted_iota(I32, (win, win), 0)
    c = lax.broadcasted_iota(I32, (win, win), 1)
    before = jnp.where(r < c, 1.0, 0.0).astype(BF16)
    rank = _dot(sel_b, before)
    count = _dot(sel_b, jnp.ones((win, win), BF16))
    run_len = jnp.floor((count + (RUN_ALIGN - 1.0)) * (1.0 / RUN_ALIGN)) * RUN_ALIGN
    er = lax.broadcasted_iota(I32, (n_exp, n_exp), 0)
    ec = lax.broadcasted_iota(I32, (n_exp, n_exp), 1)
    lower = jnp.where(ec < er, 1.0, 0.0).astype(BF16)
    slot_all = _dot(lower, run_len.astype(BF16)) + rank
    slot_ref[...] = jnp.concatenate(
        [jnp.sum(jnp.where(s, slot_all, 0.0), axis=0, keepdims=True) for s in sels],
        axis=0).astype(I32)
    cnt_ref[0] = _dot_nt(jnp.ones((8, win), BF16), sel_b)


def _router(x, mod, rows, norm_g, w_router, bias):
    n, d = x.shape
    n_exp = w_router.shape[1]
    win = rows.tile
    lanes = pl.BlockSpec((TOP_K, win), lambda i: (0, i))
    return pl.pallas_call(
        _router_kernel,
        grid=(rows.grid,),
        in_specs=[rows.rows(d), rows.mod(d, 3), rows.mod(d, 4), _resident((1, d)),
                  _resident((n_exp, d)), _resident((n_exp, 1))],
        out_specs=[rows.rows(d), lanes, lanes, pl.BlockSpec((1, 8, n_exp), lambda i: (i, 0, 0))],
        out_shape=[jax.ShapeDtypeStruct((n, d), BF16), jax.ShapeDtypeStruct((TOP_K, n), I32),
                   jax.ShapeDtypeStruct((TOP_K, n), F32),
                   jax.ShapeDtypeStruct((rows.grid, 8, n_exp), F32)],
        compiler_params=_cparams("parallel"),
        name="moe_router",
    )(x, mod, mod, norm_g.reshape(1, d), w_router.T, bias.reshape(n_exp, 1))


def _sorted_rows(win, n_exp):
    return -(-(win * TOP_K + n_exp * (RUN_ALIGN - 1)) // SORT_BLOCK) * SORT_BLOCK


def _plan(cnt, tile, sorted_rows):
    c = cnt[:, 0, :].astype(I32)
    n_win, n_exp = c.shape
    run = (c + RUN_ALIGN - 1) // RUN_ALIGN * RUN_ALIGN
    per_exp = jnp.sum(run, axis=0)
    padded = (per_exp + tile - 1) // tile * tile
    pend = jnp.cumsum(padded)
    pstart = pend - padded
    dest = pstart[None, :] + jnp.cumsum(run, axis=0) - run
    local_end = jnp.cumsum(run, axis=1)
    chunk_row = jnp.arange(sorted_rows // RUN_ALIGN, dtype=I32) * RUN_ALIGN
    owner = jnp.sum((local_end[:, None, :] <= chunk_row[None, :, None]).astype(I32), axis=2)
    own = (owner[:, :, None] == jnp.arange(n_exp, dtype=I32)[None, None, :]).astype(I32)
    run_dest = jnp.sum(own * dest[:, None, :], axis=2)
    run_local = jnp.sum(own * (local_end - run)[:, None, :], axis=2)
    chunk_dst = run_dest + chunk_row[None, :] - run_local
    return dict(
        chunk_dst=chunk_dst.reshape(-1), n_chunks=local_end[:, -1] // RUN_ALIGN,
        pad_start=pstart + per_exp, pad_chunks=(padded - per_exp) // RUN_ALIGN,
        pend=pend, used_tiles=(pend[-1] // tile).reshape(1))


def _capacity(n, n_win, n_exp, tile):
    worst = n * TOP_K + n_win * n_exp * (RUN_ALIGN - 1) + n_exp * (tile - RUN_ALIGN)
    return -(-worst // tile) * tile


def _chunk_copies(chunk_dst_ref, n_chunks_ref, window, chunks_per_window, make_copy, wait):
    def per_chunk(c, carry):
        cp = make_copy(pl.multiple_of(c * RUN_ALIGN, RUN_ALIGN),
                       pl.multiple_of(chunk_dst_ref[window * chunks_per_window + c], RUN_ALIGN))
        if wait:
            cp.wait()
        else:
            cp.start()
        return carry

    lax.fori_loop(0, n_chunks_ref[window], per_chunk, 0)


def _dispatch_kernel(chunk_dst_ref, n_chunks_ref, pad_start_ref, pad_chunks_ref, used_ref,
                     h2_ref, slot_ref, xbuf_ref, xs_scr, zero_scr, sems, tail_sem,
                     *, n_exp, n_win, tile):
    w = pl.program_id(0)
    cur = w % 2
    n_tail = xbuf_ref.shape[0] // tile - used_ref[0]

    def tail_copy(j):
        dst = pl.multiple_of((used_ref[0] + j) * tile, tile)
        return pltpu.make_async_copy(zero_scr, xbuf_ref.at[pl.ds(dst, tile)], tail_sem)

    @pl.when(w == 0)
    def _():
        zero_scr[...] = jnp.zeros_like(zero_scr)
        lax.fori_loop(0, n_tail, lambda j, c: (tail_copy(j).start(), c)[1], 0)

    slot = slot_ref[...]
    n_rows, win = xs_scr.shape[1], slot.shape[1]
    h2 = h2_ref[...]
    for lo in range(0, n_rows, SORT_BLOCK):
        rows = lax.broadcasted_iota(I32, (SORT_BLOCK, win), 0) + lo
        onehot = jnp.where(rows == slot[0:1, :], 1.0, 0.0)
        for k in range(1, TOP_K):
            onehot = jnp.where(rows == slot[k:k + 1, :], 1.0, onehot)
        xs_scr[cur, lo:lo + SORT_BLOCK, :] = _dot(onehot.astype(BF16), h2).astype(BF16)

    def copies(window, buf, wait):
        def copy(src, dst):
            return pltpu.make_async_copy(xs_scr.at[buf, pl.ds(src, RUN_ALIGN)],
                                         xbuf_ref.at[pl.ds(dst, RUN_ALIGN)], sems.at[buf])
        _chunk_copies(chunk_dst_ref, n_chunks_ref, window, n_rows // RUN_ALIGN, copy, wait)

    copies(w, cur, wait=False)

    @pl.when(w > 0)
    def _():
        copies(w - 1, 1 - cur, wait=True)

    @pl.when(w == n_win - 1)
    def _():
        copies(w, cur, wait=True)
        sem = sems.at[cur]

        def pad_copy(e, j):
            dst = pl.multiple_of(pad_start_ref[e] + j * RUN_ALIGN, RUN_ALIGN)
            return pltpu.make_async_copy(zero_scr.at[pl.ds(0, RUN_ALIGN)],
                                         xbuf_ref.at[pl.ds(dst, RUN_ALIGN)], sem)

        def pads(wait):
            def per_expert(e, carry):
                def per_chunk(j, carry2):
                    if wait:
                        pad_copy(e, j).wait()
                    else:
                        pad_copy(e, j).start()
                    return carry2
                return lax.fori_loop(0, pad_chunks_ref[e], per_chunk, carry)
            lax.fori_loop(0, n_exp, per_expert, 0)

        pads(False)
        pads(True)
        lax.fori_loop(0, n_tail, lambda j, c: (tail_copy(j).wait(), c)[1], 0)


def _dispatch(h2, slot_t, plan, rows, n_exp, cap, tile):
    n, d = h2.shape
    win = rows.tile
    return pl.pallas_call(
        functools.partial(_dispatch_kernel, n_exp=n_exp, n_win=rows.grid, tile=tile),
        grid_spec=pltpu.PrefetchScalarGridSpec(
            num_scalar_prefetch=5,
            grid=(rows.grid,),
            in_specs=[rows.rows(d), pl.BlockSpec((TOP_K, win), lambda i, *_: (0, i))],
            out_specs=pl.BlockSpec(memory_space=pl.ANY),
            scratch_shapes=[pltpu.VMEM((2, _sorted_rows(win, n_exp), d), BF16),
                            pltpu.VMEM((tile, d), BF16),
                            pltpu.SemaphoreType.DMA((2,)), pltpu.SemaphoreType.DMA(())]),
        out_shape=jax.ShapeDtypeStruct((cap, d), BF16),
        compiler_params=_cparams("arbitrary"),
        name="moe_dispatch",
    )(plan["chunk_dst"], plan["n_chunks"], plan["pad_start"], plan["pad_chunks"],
      plan["used_tiles"], h2, slot_t)


def _experts_kernel(texp_ref, tmap_ref, used_ref, x_ref, wg_ref, wu_ref, wd_ref, y_ref,
                    wgu_scr, wd_scr, *, sub_tiles):
    i = pl.program_id(0)
    live = i < used_ref[0]
    de = wd_scr.shape[0]

    @pl.when(jnp.logical_or(i == 0, texp_ref[i] != texp_ref[jnp.maximum(i - 1, 0)]))
    def _():
        wgu_scr[:, :de] = wg_ref[...].astype(BF16)
        wgu_scr[:, de:] = wu_ref[...].astype(BF16)
        wd_scr[...] = wd_ref[...].astype(BF16)

    @pl.when(live)
    def _():
        sub = x_ref.shape[0] // sub_tiles
        for s in range(sub_tiles):
            rows = slice(s * sub, (s + 1) * sub)
            ab = _dot(x_ref[rows, :], wgu_scr[...])
            mid = (_silu(ab[:, :de]) * ab[:, de:]).astype(BF16)
            y_ref[rows, :] = _dot(mid, wd_scr[...]).astype(BF16)

    @pl.when(jnp.logical_not(live))
    def _():
        y_ref[...] = jnp.zeros_like(y_ref)


def _experts(xbuf, plan, layer, w_gate, w_up, w_down, tile):
    cap, d = xbuf.shape
    _, n_exp, _, de = w_gate.shape
    n_tiles = cap // tile
    used = plan["used_tiles"]
    tmap = jnp.minimum(jnp.arange(n_tiles, dtype=I32), used[0] - 1)
    texp = jnp.sum((plan["pend"][None, :] <= (tmap * tile)[:, None]).astype(I32), axis=1)
    texp = jnp.minimum(texp, n_exp - 1)
    return pl.pallas_call(
        functools.partial(_experts_kernel, sub_tiles=max(tile // EXPERT_SUB_TILE, 1)),
        grid_spec=pltpu.PrefetchScalarGridSpec(
            num_scalar_prefetch=3,
            grid=(n_tiles,),
            in_specs=[pl.BlockSpec((tile, d), lambda i, te, tm, u: (tm[i], 0)),
                      pl.BlockSpec((None, None, d, de), lambda i, te, tm, u: (layer, te[i], 0, 0)),
                      pl.BlockSpec((None, None, d, de), lambda i, te, tm, u: (layer, te[i], 0, 0)),
                      pl.BlockSpec((None, None, de, d), lambda i, te, tm, u: (layer, te[i], 0, 0))],
            out_specs=pl.BlockSpec((tile, d), lambda i, te, tm, u: (i, 0)),
            scratch_shapes=[pltpu.VMEM((d, 2 * de), BF16), pltpu.VMEM((de, d), BF16)]),
        out_shape=jax.ShapeDtypeStruct((cap, d), BF16),
        compiler_params=_cparams("arbitrary"),
        name="moe_experts",
    )(texp, tmap, used, xbuf, w_gate, w_up, w_down)


def _combine_kernel(chunk_dst_ref, n_chunks_ref, ybuf_ref, slot_ref, wt_ref, h2_ref, x_ref,
                    g2_ref, sg_ref, su_ref, sd_ref, fin_ref, out_ref, ys_scr, sems,
                    *, n_win, final_norm):
    w = pl.program_id(0)
    cur = w % 2
    n_rows = ys_scr.shape[1]

    def copies(window, buf, wait):
        def copy(loc, dst):
            return pltpu.make_async_copy(ybuf_ref.at[pl.ds(dst, RUN_ALIGN)],
                                         ys_scr.at[buf, pl.ds(loc, RUN_ALIGN)], sems.at[buf])
        _chunk_copies(chunk_dst_ref, n_chunks_ref, window, n_rows // RUN_ALIGN, copy, wait)

    @pl.when(w == 0)
    def _():
        ys_scr[...] = jnp.zeros_like(ys_scr)
        copies(w, cur, wait=False)

    @pl.when(w + 1 < n_win)
    def _():
        copies(w + 1, 1 - cur, wait=False)

    h2 = h2_ref[...]
    shared = _dot((_silu(_dot(h2, sg_ref[...])) * _dot(h2, su_ref[...])).astype(BF16), sd_ref[...])
    copies(w, cur, wait=True)

    slot, wt = slot_ref[...], wt_ref[...]
    win = slot.shape[0]
    routed = shared
    for lo in range(0, n_rows, SORT_BLOCK):
        lanes = lax.broadcasted_iota(I32, (win, SORT_BLOCK), 1) + lo
        weights = jnp.where(lanes == slot[:, 0:1], wt[:, 0:1], 0.0)
        for k in range(1, TOP_K):
            weights = jnp.where(lanes == slot[:, k:k + 1], wt[:, k:k + 1], weights)
        routed = routed + _dot(weights.astype(BF16), ys_scr[cur, lo:lo + SORT_BLOCK, :])
    out = x_ref[...] + g2_ref[...] * routed
    if final_norm:
        out = out * lax.rsqrt(jnp.mean(out * out, axis=-1, keepdims=True) + EPS) * fin_ref[...]
    out_ref[...] = out


def _combine(ybuf, plan, slot, wt, h2, x, mod, rows, sh_bf16, final_g, n_exp, final_norm):
    n, d = x.shape
    win = rows.tile
    sg, su, sd = sh_bf16
    pairs = pl.BlockSpec((win, TOP_K), lambda i, *_: (i, 0))
    return pl.pallas_call(
        functools.partial(_combine_kernel, n_win=rows.grid, final_norm=final_norm),
        grid_spec=pltpu.PrefetchScalarGridSpec(
            num_scalar_prefetch=2,
            grid=(rows.grid,),
            in_specs=[pl.BlockSpec(memory_space=pl.ANY), pairs, pairs, rows.rows(d), rows.rows(d),
                      rows.mod(d, 5), _resident(sg.shape), _resident(su.shape),
                      _resident(sd.shape), _resident((1, d))],
            out_specs=rows.rows(d),
            scratch_shapes=[pltpu.VMEM((2, _sorted_rows(win, n_exp), d), BF16),
                            pltpu.SemaphoreType.DMA((2,))]),
        out_shape=jax.ShapeDtypeStruct((n, d), F32),
        compiler_params=_cparams("arbitrary"),
        name="moe_combine",
    )(plan["chunk_dst"], plan["n_chunks"], ybuf, slot, wt, h2, x, mod,
      sg, su, sd, final_g.reshape(1, d))


def _moe(x, mod, rows, layer, norm_g, w_router, bias, w_gate, w_up, w_down, sh_bf16, final_g,
         final_norm, tile):
    n = x.shape[0]
    n_exp = w_router.shape[1]
    h2, slot_t, wt_t, cnt = _router(x, mod, rows, norm_g, w_router, bias)
    plan = _plan(cnt, tile, _sorted_rows(rows.tile, n_exp))
    cap = _capacity(n, rows.grid, n_exp, tile)
    xbuf = _dispatch(h2, slot_t, plan, rows, n_exp, cap, tile)
    ybuf = _experts(xbuf, plan, layer, w_gate, w_up, w_down, tile)
    return _combine(ybuf, plan, slot_t.T, wt_t.T, h2, x, mod, rows, sh_bf16, final_g,
                    n_exp, final_norm)


def _rope_tables(pos, half):
    inv = ROPE_BASE ** (-jnp.arange(half, dtype=F32) / half)
    ang = pos.astype(F32)[:, None] * inv[None, :]
    return jnp.cos(ang), jnp.sin(ang)


def _trunk(x3, mod_all, pos0, s_ret, s_gla, wts, expert_tile):
    batch, seq, d = x3.shape
    n = batch * seq
    x = x3.reshape(n, d)
    rows = _Rows(n, seq, min(ROW_TILE, n))
    decode = seq == 1
    ret_dk = d // RET_HEADS
    ret_dv = 2 * ret_dk
    gla_dk = d // (2 * GLA_HEADS)
    gla_dv = d // GLA_HEADS
    depth = wts["ada_w"].shape[0]
    new_ret, new_gla = [], []
    for layer in range(depth):
        mod = mod_all[layer] if decode else mod_all[layer].reshape(batch, 1, 6 * d)
        j = layer // 2
        if layer % 2 == 0:
            pos = (jnp.full((rows.tile,), pos0, I32) if decode
                   else pos0 + jnp.arange(seq, dtype=I32))
            cos, sin = _rope_tables(pos, ret_dk // 2)
            q, k, v, gate = _ret_proj(x, mod, rows, wts["norm_mix_g"][layer], wts["ret_w_in"][j],
                                      cos, sin, RET_HEADS, ret_dk, ret_dv)
            log_gamma = jnp.log1p(-jnp.power(2.0, -5.0 - jnp.arange(RET_HEADS, dtype=F32)))
            if decode:
                decay = jnp.broadcast_to(jnp.repeat(jnp.exp(log_gamma), ret_dk)[None, :],
                                         (n, RET_HEADS * ret_dk))
                o, s_new = _step(s_ret, j, q, k, decay, v, RET_HEADS, ret_dk, ret_dv)
            else:
                o, s_new = _ret_core(q, k, v, log_gamma, batch, seq, RET_HEADS, ret_dk, ret_dv)
            new_ret.append(s_new)
            x = _post(o, gate, x, mod, rows, wts["ret_norm_g"][j], wts["ret_w_out"][j],
                      RET_HEADS, ret_dv, center=True)
        else:
            q, k, v, gate, la = _gla_proj(x, mod, rows, wts["norm_mix_g"][layer], wts["gla_w_in"][j],
                                          wts["gla_w_a1"][j], wts["gla_w_a2"][j], wts["gla_b_a"][j],
                                          GLA_HEADS * gla_dk, GLA_HEADS * gla_dv)
            if decode:
                o, s_new = _step(s_gla, j, q, k, jnp.exp(la), v, GLA_HEADS, gla_dk, gla_dv)
            else:
                o, s_new = _gla_core(q, k, v, la, batch, seq, GLA_HEADS, gla_dk, gla_dv)
            new_gla.append(s_new)
            x = _post(o, gate, x, mod, rows, wts["gla_norm_g"][j], wts["gla_w_out"][j],
                      GLA_HEADS, gla_dv, center=False)
        x = _moe(x, mod, rows, layer, wts["norm_ffn_g"][layer], wts["moe_w_router"][layer],
                 wts["moe_router_bias"][layer], wts["moe_w_gate"], wts["moe_w_up"],
                 wts["moe_w_down"], wts["shared"][layer], wts["final_norm_g"],
                 final_norm=layer == depth - 1, tile=expert_tile)
    return x.reshape(batch, seq, d), jnp.stack(new_ret), jnp.stack(new_gla)


def kernel(x_prompt, x_sample, state_ret, state_gla, c_prompt, c_sample, ret_w_in, ret_norm_g, ret_w_out, gla_w_in, gla_w_a1, gla_w_a2, gla_b_a, gla_norm_g, gla_w_out, ada_w, ada_b, norm_mix_g, norm_ffn_g, moe_w_router, moe_router_bias, moe_w_gate, moe_w_up, moe_w_down, sh_w_gate, sh_w_up, sh_w_down, final_norm_g):
    b = x_prompt.shape[0]
    depth = ada_w.shape[0]
    rank = gla_w_a1.shape[-1]
    pad = LANES - rank
    wts = dict(
        ret_w_in=ret_w_in.astype(BF16), ret_norm_g=ret_norm_g, ret_w_out=ret_w_out.astype(BF16),
        gla_w_in=gla_w_in.astype(BF16),
        gla_w_a1=jnp.pad(gla_w_a1, ((0, 0), (0, 0), (0, pad))).astype(BF16),
        gla_w_a2=jnp.pad(gla_w_a2, ((0, 0), (0, pad), (0, 0))).astype(BF16),
        gla_b_a=gla_b_a, gla_norm_g=gla_norm_g, gla_w_out=gla_w_out.astype(BF16),
        ada_w=ada_w, norm_mix_g=norm_mix_g, norm_ffn_g=norm_ffn_g,
        moe_w_router=moe_w_router, moe_router_bias=moe_router_bias,
        moe_w_gate=moe_w_gate, moe_w_up=moe_w_up, moe_w_down=moe_w_down,
        shared=[(sh_w_gate[l].astype(BF16), sh_w_up[l].astype(BF16), sh_w_down[l].astype(BF16))
                for l in range(depth)],
        final_norm_g=final_norm_g)
    mod = _ada(jnp.concatenate([c_prompt, c_sample], axis=0), ada_w, ada_b)
    y_p, ret_p, gla_p = _trunk(x_prompt, mod[:, :b], 0, None, None, wts, expert_tile=512)
    y_s, ret_s, gla_s = _trunk(x_sample, mod[:, b:], PAST_LEN, state_ret, state_gla, wts,
                               expert_tile=128)
    return (y_p, y_s, ret_p, gla_p, ret_s, gla_s)
```

```python
import functools

import jax
import jax.numpy as jnp
from jax import lax
from jax.experimental import pallas as pl
from jax.experimental.pallas import tpu as pltpu

F32, BF16, I32 = jnp.float32, jnp.bfloat16, jnp.int32

EPS = 1e-6
ROPE_BASE = 10000.0
PAST_LEN = 16384
RET_HEADS = 4
GLA_HEADS = 4
GLA_TAU = 16.0
N_GROUPS = 8
TOPK_GROUPS = 4
TOP_K = 8
ROUTED_SCALE = 2.5

LANES = 128
BF16_SUBLANES = 16
VMEM_LIMIT_BYTES = 56 * 1024 * 1024

ROW_TILE = 256
RET_CHUNK = 256
GLA_CHUNK = 128
GLA_KEY_BLOCK = 16
STEP_TOKENS = 8
RUN_ALIGN = BF16_SUBLANES
EXPERT_SUB_TILE = 256
SORT_BLOCK = 512
EXPERT_RING = 3
ONEHOT_BLOCK = 256


def _cparams(*sem):
    return pltpu.CompilerParams(dimension_semantics=sem, vmem_limit_bytes=VMEM_LIMIT_BYTES)


def _sigmoid(x):
    return 1.0 / (1.0 + jnp.exp(-x))


def _silu(x):
    return x * _sigmoid(x)


def _modulate(x, g, shift, scale):
    y = x * lax.rsqrt(jnp.mean(x * x, axis=-1, keepdims=True) + EPS) * g
    return y * (1.0 + scale) + shift


def _dot(a, b):
    return jnp.dot(a, b, preferred_element_type=F32)


def _dot_nt(a, b):
    return lax.dot_general(a, b, (((1,), (1,)), ((), ())), preferred_element_type=F32)


def _dot_tn(a, b):
    return lax.dot_general(a, b, (((0,), (0,)), ((), ())), preferred_element_type=F32)


def _resident(shape):
    zeros = (0,) * len(shape)
    return pl.BlockSpec(shape, lambda *_: zeros, pipeline_mode=pl.Buffered(1))


class _Rows:
    def __init__(self, n_rows, seq_len, tile):
        self.n, self.tile = n_rows, tile
        self.per_row = seq_len == 1
        self.tiles_per_seq = max(seq_len // tile, 1)
        assert n_rows % tile == 0 and (self.per_row or seq_len % tile == 0)
        self.grid = n_rows // tile

    def rows(self, width, col=0):
        return pl.BlockSpec((self.tile, width), lambda i, *_: (i, col))

    def mod(self, d, col):
        if self.per_row:
            return pl.BlockSpec((self.tile, d), lambda i, *_: (i, col))
        tps = self.tiles_per_seq
        return pl.BlockSpec((None, 1, d), lambda i, *_: (i // tps, 0, col))


def _ada_kernel(c_ref, w_ref, b_ref, o_ref):
    s = _silu(c_ref[...]).astype(BF16)
    o_ref[0] = _dot(s, w_ref[0].astype(BF16)) + b_ref[0]


def _ada(c_all, ada_w, ada_b):
    depth, d, d6 = ada_w.shape
    n = c_all.shape[0]
    tn = d6 // 4
    return pl.pallas_call(
        _ada_kernel,
        grid=(depth, d6 // tn),
        in_specs=[pl.BlockSpec((n, d), lambda l, j: (0, 0)),
                  pl.BlockSpec((1, d, tn), lambda l, j: (l, 0, j)),
                  pl.BlockSpec((1, 1, tn), lambda l, j: (l, 0, j))],
        out_specs=pl.BlockSpec((1, n, tn), lambda l, j: (l, 0, j)),
        out_shape=jax.ShapeDtypeStruct((depth, n, d6), F32),
        compiler_params=_cparams("parallel", "parallel"),
        name="ada_mod",
    )(c_all, ada_w, ada_b.reshape(depth, 1, d6))


def _ret_proj_kernel(x_ref, sh_ref, sc_ref, g_ref, w_ref, cos_ref, sin_ref,
                     q_ref, k_ref, v_ref, gate_ref, *, n_heads, dk, dv):
    h = _modulate(x_ref[...], g_ref[...], sh_ref[...], sc_ref[...]).astype(BF16)
    cos, sin = cos_ref[...], sin_ref[...]
    half, nqk, nv = dk // 2, n_heads * dk, n_heads * dv
    for hd in range(n_heads):
        for dst, base, scale in ((q_ref, 0, None), (k_ref, nqk, dk ** -0.5)):
            p = _dot(h, w_ref[:, base + hd * dk:base + (hd + 1) * dk])
            x1, x2 = p[:, :half], p[:, half:]
            r1, r2 = x1 * cos - x2 * sin, x1 * sin + x2 * cos
            if scale is not None:
                r1, r2 = r1 * scale, r2 * scale
            dst[:, hd * dk:hd * dk + half] = r1.astype(BF16)
            dst[:, hd * dk + half:(hd + 1) * dk] = r2.astype(BF16)
    for hd in range(n_heads):
        v_ref[:, hd * dv:(hd + 1) * dv] = _dot(
            h, w_ref[:, 2 * nqk + hd * dv:2 * nqk + (hd + 1) * dv]).astype(BF16)
        gate_ref[:, hd * dv:(hd + 1) * dv] = _dot(
            h, w_ref[:, 2 * nqk + nv + hd * dv:2 * nqk + nv + (hd + 1) * dv])


def _ret_proj(x, mod, rows, norm_g, w_in_bf16, cos, sin, n_heads, dk, dv):
    n, d = x.shape
    nqk, nv = n_heads * dk, n_heads * dv
    half = dk // 2
    tps = rows.tiles_per_seq
    trig = (pl.BlockSpec((rows.tile, half), lambda i: (0, 0)) if rows.per_row
            else pl.BlockSpec((rows.tile, half), lambda i: (i % tps, 0)))
    return pl.pallas_call(
        functools.partial(_ret_proj_kernel, n_heads=n_heads, dk=dk, dv=dv),
        grid=(rows.grid,),
        in_specs=[rows.rows(d), rows.mod(d, 0), rows.mod(d, 1), _resident((1, d)),
                  _resident(w_in_bf16.shape), trig, trig],
        out_specs=[rows.rows(nqk), rows.rows(nqk), rows.rows(nv), rows.rows(nv)],
        out_shape=[jax.ShapeDtypeStruct((n, nqk), BF16), jax.ShapeDtypeStruct((n, nqk), BF16),
                   jax.ShapeDtypeStruct((n, nv), BF16), jax.ShapeDtypeStruct((n, nv), F32)],
        compiler_params=_cparams("parallel"),
        name="ret_proj",
    )(x, mod, mod, norm_g.reshape(1, d), w_in_bf16, cos, sin)


def _ret_core_kernel(lg_ref, q_ref, k_ref, v_ref, o_ref, s_out_ref, s_scr,
                     *, tc, n_chunks, n_heads, dk, dv):
    c = pl.program_id(1)

    @pl.when(c == 0)
    def _():
        s_scr[...] = jnp.zeros_like(s_scr)

    ii = lax.broadcasted_iota(I32, (tc, tc), 0)
    jj = lax.broadcasted_iota(I32, (tc, tc), 1)
    causal = ii >= jj
    lag = (ii - jj).astype(F32)
    row = lax.broadcasted_iota(I32, (tc, 1), 0).astype(F32)
    for hd in range(n_heads):
        lg = lg_ref[hd]
        q, k = q_ref[:, hd * dk:(hd + 1) * dk], k_ref[:, hd * dk:(hd + 1) * dk]
        v = v_ref[:, hd * dv:(hd + 1) * dv]
        dec = jnp.where(causal, jnp.exp(lag * lg), 0.0)
        att = (_dot_nt(q, k) * dec).astype(BF16)
        s_old = s_scr[hd]
        o_ref[:, hd * dv:(hd + 1) * dv] = (
            _dot(att, v) + jnp.exp((row + 1.0) * lg) * _dot(q, s_old.astype(BF16)))
        kd = (k.astype(F32) * jnp.exp((tc - 1.0 - row) * lg)).astype(BF16)
        s_scr[hd] = jnp.exp(jnp.full((1, 1), float(tc), F32) * lg) * s_old + _dot_tn(kd, v)

    @pl.when(c == n_chunks - 1)
    def _():
        s_out_ref[0] = s_scr[...]


def _ret_core(q, k, v, log_gamma, batch, seq, n_heads, dk, dv):
    tc = min(RET_CHUNK, seq)
    nc = seq // tc
    assert seq % tc == 0
    n = batch * seq
    qk = pl.BlockSpec((tc, n_heads * dk), lambda b, c: (b * nc + c, 0))
    vv = pl.BlockSpec((tc, n_heads * dv), lambda b, c: (b * nc + c, 0))
    return pl.pallas_call(
        functools.partial(_ret_core_kernel, tc=tc, n_chunks=nc, n_heads=n_heads, dk=dk, dv=dv),
        grid=(batch, nc),
        in_specs=[pl.BlockSpec(memory_space=pltpu.SMEM), qk, qk, vv],
        out_specs=[vv, pl.BlockSpec((1, n_heads, dk, dv), lambda b, c: (b, 0, 0, 0))],
        out_shape=[jax.ShapeDtypeStruct((n, n_heads * dv), F32),
                   jax.ShapeDtypeStruct((batch, n_heads, dk, dv), F32)],
        scratch_shapes=[pltpu.VMEM((n_heads, dk, dv), F32)],
        compiler_params=_cparams("parallel", "arbitrary"),
        name="ret_core",
    )(log_gamma, q, k, v)


def _step_kernel(s_ref, q_ref, k_ref, a_ref, v_ref, o_ref, s_out_ref, *, tb):
    for j in range(tb):
        s_new = (a_ref[0, 0, :, j:j + 1] * s_ref[j, 0]
                 + k_ref[0, 0, :, j:j + 1] * v_ref[j:j + 1, :])
        s_out_ref[j, 0] = s_new
        o_ref[j:j + 1, :] = jnp.sum(q_ref[0, 0, :, j:j + 1] * s_new, axis=0, keepdims=True)


def _columns(x, n_heads, dk, tb):
    n = x.shape[0]
    return x.astype(F32).reshape(n // tb, tb, n_heads, dk).transpose(2, 0, 3, 1)


def _step(states, layer, q, k, a, v, n_heads, dk, dv):
    n = states.shape[1]
    tb = STEP_TOKENS
    assert n % tb == 0
    col = pl.BlockSpec((1, 1, dk, tb), lambda i, h: (h, i, 0, 0))
    st_in = pl.BlockSpec((None, tb, 1, dk, dv), lambda i, h: (layer, i, h, 0, 0))
    st_out = pl.BlockSpec((tb, 1, dk, dv), lambda i, h: (i, h, 0, 0))
    row = pl.BlockSpec((tb, dv), lambda i, h: (i, h))
    return pl.pallas_call(
        functools.partial(_step_kernel, tb=tb),
        grid=(n // tb, n_heads),
        in_specs=[st_in, col, col, col, row],
        out_specs=[row, st_out],
        out_shape=[jax.ShapeDtypeStruct((n, n_heads * dv), F32),
                   jax.ShapeDtypeStruct(states.shape[1:], states.dtype)],
        compiler_params=_cparams("parallel", "parallel"),
        name="state_step",
    )(states, _columns(q, n_heads, dk, tb), _columns(k, n_heads, dk, tb),
      _columns(a, n_heads, dk, tb), v.astype(F32))


def _gla_proj_kernel(x_ref, sh_ref, sc_ref, g_ref, w_ref, a1_ref, a2_ref, ba_ref,
                     q_ref, k_ref, v_ref, r_ref, la_ref, *, nqk, nv):
    h = _modulate(x_ref[...], g_ref[...], sh_ref[...], sc_ref[...]).astype(BF16)
    q_ref[...] = _dot(h, w_ref[:, :nqk]) * ((nqk // GLA_HEADS) ** -0.5)
    k_ref[...] = _dot(h, w_ref[:, nqk:2 * nqk])
    v_ref[...] = _dot(h, w_ref[:, 2 * nqk:2 * nqk + nv]).astype(BF16)
    r_ref[...] = _dot(h, w_ref[:, 2 * nqk + nv:])
    z = _dot(_dot(h, a1_ref[...]).astype(BF16), a2_ref[...]) + ba_ref[...]
    la_ref[...] = (jnp.minimum(z, 0.0) - jnp.log1p(jnp.exp(-jnp.abs(z)))) / GLA_TAU


def _gla_proj(x, mod, rows, norm_g, w_in_bf16, a1_bf16, a2_bf16, b_a, nqk, nv):
    n, d = x.shape
    return pl.pallas_call(
        functools.partial(_gla_proj_kernel, nqk=nqk, nv=nv),
        grid=(rows.grid,),
        in_specs=[rows.rows(d), rows.mod(d, 0), rows.mod(d, 1), _resident((1, d)),
                  _resident(w_in_bf16.shape), _resident(a1_bf16.shape),
                  _resident(a2_bf16.shape), _resident((1, nqk))],
        out_specs=[rows.rows(nqk), rows.rows(nqk), rows.rows(nv), rows.rows(nv), rows.rows(nqk)],
        out_shape=[jax.ShapeDtypeStruct((n, nqk), F32), jax.ShapeDtypeStruct((n, nqk), F32),
                   jax.ShapeDtypeStruct((n, nv), BF16), jax.ShapeDtypeStruct((n, nv), F32),
                   jax.ShapeDtypeStruct((n, nqk), F32)],
        compiler_params=_cparams("parallel"),
        name="gla_proj",
    )(x, mod, mod, norm_g.reshape(1, d), w_in_bf16, a1_bf16, a2_bf16, b_a.reshape(1, nqk))


def _gla_core_kernel(q_ref, k_ref, v_ref, la_ref, o_ref, s_out_ref, s_scr,
                     *, tc, n_chunks, n_heads, dk, dv):
    c = pl.program_id(1)

    @pl.when(c == 0)
    def _():
        s_scr[...] = jnp.zeros_like(s_scr)

    ii = lax.broadcasted_iota(I32, (tc, tc), 0)
    jj = lax.broadcasted_iota(I32, (tc, tc), 1)
    la = la_ref[...]
    la_hi = la.astype(BF16)
    la_mid = (la - la_hi.astype(F32)).astype(BF16)
    la_lo = (la - la_hi.astype(F32) - la_mid.astype(F32)).astype(BF16)
    tri = jnp.where(ii >= jj, 1.0, 0.0).astype(BF16)
    b_all = _dot(tri, la_hi) + _dot(tri, la_mid) + _dot(tri, la_lo)
    ones = jnp.ones((tc, LANES), BF16)
    tot_col_all = (_dot_tn(la_hi, ones) + _dot_tn(la_mid, ones) + _dot_tn(la_lo, ones))[:, 0:1]
    qi = lax.broadcasted_iota(I32, (tc, GLA_KEY_BLOCK), 0)
    kj = lax.broadcasted_iota(I32, (tc, GLA_KEY_BLOCK), 1)
    for hd in range(n_heads):
        cols = slice(hd * dk, (hd + 1) * dk)
        b = b_all[:, cols]
        q, k, v = q_ref[:, cols], k_ref[:, cols], v_ref[:, hd * dv:(hd + 1) * dv]
        s_old = s_scr[hd]
        o = _dot((q * jnp.exp(b)).astype(BF16), s_old.astype(BF16))
        for s in range(tc // GLA_KEY_BLOCK):
            lo = s * GLA_KEY_BLOCK
            blk = slice(lo, lo + GLA_KEY_BLOCK)
            ref = b[lo - 1:lo, :] if s else jnp.zeros_like(b[0:1, :])
            qs = (q * jnp.exp(jnp.minimum(b - ref, 0.0))).astype(BF16)
            ks = (k[blk, :] * jnp.exp(ref - b[blk, :])).astype(BF16)
            att = jnp.where(qi >= kj + lo, _dot_nt(qs, ks), 0.0).astype(BF16)
            o = o + _dot(att, v[blk, :])
        o_ref[:, hd * dv:(hd + 1) * dv] = o
        kd = (k * jnp.exp(b[tc - 1:tc, :] - b)).astype(BF16)
        s_scr[hd] = jnp.exp(tot_col_all[cols, :]) * s_old + _dot_tn(kd, v)

    @pl.when(c == n_chunks - 1)
    def _():
        s_out_ref[0] = s_scr[...]


def _gla_core(q, k, v, la, batch, seq, n_heads, dk, dv):
    tc = min(GLA_CHUNK, seq)
    nc = seq // tc
    assert seq % tc == 0
    n = batch * seq
    qk = pl.BlockSpec((tc, n_heads * dk), lambda b, c: (b * nc + c, 0))
    vv = pl.BlockSpec((tc, n_heads * dv), lambda b, c: (b * nc + c, 0))
    return pl.pallas_call(
        functools.partial(_gla_core_kernel, tc=tc, n_chunks=nc, n_heads=n_heads, dk=dk, dv=dv),
        grid=(batch, nc),
        in_specs=[qk, qk, vv, qk],
        out_specs=[vv, pl.BlockSpec((1, n_heads, dk, dv), lambda b, c: (b, 0, 0, 0))],
        out_shape=[jax.ShapeDtypeStruct((n, n_heads * dv), F32),
                   jax.ShapeDtypeStruct((batch, n_heads, dk, dv), F32)],
        scratch_shapes=[pltpu.VMEM((n_heads, dk, dv), F32)],
        compiler_params=_cparams("parallel", "arbitrary"),
        name="gla_core",
    )(q, k, v, la)


def _post_kernel(o_ref, gate_ref, x_ref, g1_ref, ng_ref, w_ref, out_ref, *, n_heads, dv, center):
    acc = None
    for hd in range(n_heads):
        sl = slice(hd * dv, (hd + 1) * dv)
        o = o_ref[:, sl]
        if center:
            o = o - jnp.mean(o, axis=-1, keepdims=True)
        y = o * lax.rsqrt(jnp.mean(o * o, axis=-1, keepdims=True) + EPS) * ng_ref[:, sl]
        y = (y * _silu(gate_ref[:, sl])).astype(BF16)
        part = _dot(y, w_ref[sl, :])
        acc = part if acc is None else acc + part
    out_ref[...] = x_ref[...] + g1_ref[...] * acc


def _post(o, gate, x, mod, rows, norm_g, w_out_bf16, n_heads, dv, center):
    n, d = x.shape
    nv = n_heads * dv
    return pl.pallas_call(
        functools.partial(_post_kernel, n_heads=n_heads, dv=dv, center=center),
        grid=(rows.grid,),
        in_specs=[rows.rows(nv), rows.rows(nv), rows.rows(d), rows.mod(d, 2),
                  _resident((1, nv)), _resident(w_out_bf16.shape)],
        out_specs=rows.rows(d),
        out_shape=jax.ShapeDtypeStruct((n, d), F32),
        compiler_params=_cparams("parallel"),
        name="mixer_post",
    )(o, gate, x, mod, norm_g.reshape(1, nv), w_out_bf16)


def _first_max(vals, idx_iota, n):
    m = jnp.max(vals, axis=0, keepdims=True)
    i = jnp.min(jnp.where(vals == m, idx_iota, n), axis=0, keepdims=True)
    return m, i


def _router_kernel(x_ref, sh_ref, sc_ref, g_ref, wr_ref, b_ref,
                   h2_ref, slot_ref, wt_ref, cnt_ref):
    h2 = _modulate(x_ref[...], g_ref[...], sh_ref[...], sc_ref[...])
    h2_ref[...] = h2.astype(BF16)
    h_hi = h2.astype(BF16)
    h_lo = (h2 - h_hi.astype(F32)).astype(BF16)
    wr = wr_ref[...]
    w_hi = wr.astype(BF16)
    w_lo = (wr - w_hi.astype(F32)).astype(BF16)
    logits = _dot_nt(w_hi, h_hi) + _dot_nt(w_hi, h_lo) + _dot_nt(w_lo, h_hi)
    scores = _sigmoid(logits)
    biased = scores + b_ref[...]
    n_exp, win = biased.shape
    gsz = n_exp // N_GROUPS
    neg = -jnp.inf

    sub_g = lax.broadcasted_iota(I32, (gsz, win), 0)
    group_rows = []
    for g in range(N_GROUPS):
        blk = biased[g * gsz:(g + 1) * gsz, :]
        m1, i1 = _first_max(blk, sub_g, gsz)
        m2 = jnp.max(jnp.where(sub_g == i1, neg, blk), axis=0, keepdims=True)
        group_rows.append(m1 + m2)
    cur = jnp.concatenate(group_rows, axis=0)
    sub_n = lax.broadcasted_iota(I32, (N_GROUPS, win), 0)
    gmask = jnp.zeros((N_GROUPS, win), jnp.bool_)
    for _ in range(TOPK_GROUPS):
        _, i = _first_max(cur, sub_n, N_GROUPS)
        sel = sub_n == i
        gmask = gmask | sel
        cur = jnp.where(sel, neg, cur)
    emask = jnp.concatenate(
        [jnp.broadcast_to(gmask[g:g + 1, :], (gsz, win)) for g in range(N_GROUPS)], axis=0)
    masked = jnp.where(emask, biased, neg)

    sub_e = lax.broadcasted_iota(I32, (n_exp, win), 0)
    sels, picked = [], []
    for _ in range(TOP_K):
        _, i = _first_max(masked, sub_e, n_exp)
        sel = sub_e == i
        sels.append(sel)
        picked.append(jnp.sum(jnp.where(sel, scores, 0.0), axis=0, keepdims=True))
        masked = jnp.where(sel, neg, masked)
    total = picked[0]
    for p in picked[1:]:
        total = total + p
    wt_ref[...] = jnp.concatenate([p / total * ROUTED_SCALE for p in picked], axis=0)

    chosen = sels[0]
    for s in sels[1:]:
        chosen = chosen | s
    sel_b = jnp.where(chosen, 1.0, 0.0).astype(BF16)
    r = lax.broadcasted_iota(I32, (win, win), 0)
    c = lax.broadcasted_iota(I32, (win, win), 1)
    before = jnp.where(r < c, 1.0, 0.0).astype(BF16)
    rank = _dot(sel_b, before)
    count = _dot(sel_b, jnp.ones((win, win), BF16))
    run_len = jnp.floor((count + (RUN_ALIGN - 1.0)) * (1.0 / RUN_ALIGN)) * RUN_ALIGN
    er = lax.broadcasted_iota(I32, (n_exp, n_exp), 0)
    ec = lax.broadcasted_iota(I32, (n_exp, n_exp), 1)
    lower = jnp.where(ec < er, 1.0, 0.0).astype(BF16)
    slot_all = _dot(lower, run_len.astype(BF16)) + rank
    slot_ref[...] = jnp.concatenate(
        [jnp.sum(jnp.where(s, slot_all, 0.0), axis=0, keepdims=True) for s in sels],
        axis=0).astype(I32)
    cnt_ref[0] = _dot_nt(jnp.ones((8, win), BF16), sel_b)


def _router(x, mod, rows, norm_g, w_router, bias):
    n, d = x.shape
    n_exp = w_router.shape[1]
    win = rows.tile
    lanes = pl.BlockSpec((TOP_K, win), lambda i: (0, i))
    return pl.pallas_call(
        _router_kernel,
        grid=(rows.grid,),
        in_specs=[rows.rows(d), rows.mod(d, 3), rows.mod(d, 4), _resident((1, d)),
                  _resident((n_exp, d)), _resident((n_exp, 1))],
        out_specs=[rows.rows(d), lanes, lanes, pl.BlockSpec((1, 8, n_exp), lambda i: (i, 0, 0))],
        out_shape=[jax.ShapeDtypeStruct((n, d), BF16), jax.ShapeDtypeStruct((TOP_K, n), I32),
                   jax.ShapeDtypeStruct((TOP_K, n), F32),
                   jax.ShapeDtypeStruct((rows.grid, 8, n_exp), F32)],
        compiler_params=_cparams("parallel"),
        name="moe_router",
    )(x, mod, mod, norm_g.reshape(1, d), w_router.T, bias.reshape(n_exp, 1))


def _sorted_rows(win, n_exp):
    return -(-(win * TOP_K + n_exp * (RUN_ALIGN - 1)) // SORT_BLOCK) * SORT_BLOCK


def _plan(cnt, tile, sorted_rows):
    c = cnt[:, 0, :].astype(I32)
    n_win, n_exp = c.shape
    run = (c + RUN_ALIGN - 1) // RUN_ALIGN * RUN_ALIGN
    per_exp = jnp.sum(run, axis=0)
    padded = (per_exp + tile - 1) // tile * tile
    pend = jnp.cumsum(padded)
    pstart = pend - padded
    dest = pstart[None, :] + jnp.cumsum(run, axis=0) - run
    local_end = jnp.cumsum(run, axis=1)
    chunk_row = jnp.arange(sorted_rows // RUN_ALIGN, dtype=I32) * RUN_ALIGN
    owner = jnp.sum((local_end[:, None, :] <= chunk_row[None, :, None]).astype(I32), axis=2)
    own = (owner[:, :, None] == jnp.arange(n_exp, dtype=I32)[None, None, :]).astype(I32)
    run_dest = jnp.sum(own * dest[:, None, :], axis=2)
    run_local = jnp.sum(own * (local_end - run)[:, None, :], axis=2)
    chunk_dst = run_dest + chunk_row[None, :] - run_local
    return dict(
        chunk_dst=chunk_dst.reshape(-1), n_chunks=local_end[:, -1] // RUN_ALIGN,
        pad_start=pstart + per_exp, pad_chunks=(padded - per_exp) // RUN_ALIGN,
        pend=pend, used_tiles=(pend[-1] // tile).reshape(1))


def _capacity(n, n_win, n_exp, tile):
    worst = n * TOP_K + n_win * n_exp * (RUN_ALIGN - 1) + n_exp * (tile - RUN_ALIGN)
    return -(-worst // tile) * tile


def _chunk_copies(chunk_dst_ref, n_chunks_ref, window, chunks_per_window, make_copy, wait):
    def per_chunk(c, carry):
        cp = make_copy(pl.multiple_of(c * RUN_ALIGN, RUN_ALIGN),
                       pl.multiple_of(chunk_dst_ref[window * chunks_per_window + c], RUN_ALIGN))
        if wait:
            cp.wait()
        else:
            cp.start()
        return carry

    lax.fori_loop(0, n_chunks_ref[window], per_chunk, 0)


def _dispatch_kernel(chunk_dst_ref, n_chunks_ref, pad_start_ref, pad_chunks_ref, used_ref,
                     h2_ref, slot_ref, xbuf_ref, xs_scr, zero_scr, sems, tail_sem,
                     *, n_exp, n_win, tile):
    w = pl.program_id(0)
    cur = w % 2
    n_tail = xbuf_ref.shape[0] // tile - used_ref[0]

    def tail_copy(j):
        dst = pl.multiple_of((used_ref[0] + j) * tile, tile)
        return pltpu.make_async_copy(zero_scr, xbuf_ref.at[pl.ds(dst, tile)], tail_sem)

    @pl.when(w == 0)
    def _():
        zero_scr[...] = jnp.zeros_like(zero_scr)
        lax.fori_loop(0, n_tail, lambda j, c: (tail_copy(j).start(), c)[1], 0)

    slot = slot_ref[...]
    n_rows, win = xs_scr.shape[1], slot.shape[1]
    h2 = h2_ref[...]
    local = lax.broadcasted_iota(I32, (ONEHOT_BLOCK, win), 0).astype(F32).astype(BF16)
    slot_block = slot // ONEHOT_BLOCK
    slot_local = (slot % ONEHOT_BLOCK).astype(F32)
    one, zero = jnp.ones((), BF16), jnp.zeros((), BF16)
    for blk in range(n_rows // ONEHOT_BLOCK):
        key = jnp.where(slot_block == blk, slot_local, -1.0).astype(BF16)
        onehot = jnp.where(local == key[0:1, :], one, zero)
        for k in range(1, TOP_K):
            onehot = jnp.where(local == key[k:k + 1, :], one, onehot)
        lo = blk * ONEHOT_BLOCK
        xs_scr[cur, lo:lo + ONEHOT_BLOCK, :] = _dot(onehot, h2).astype(BF16)

    def copies(window, buf, wait):
        def copy(src, dst):
            return pltpu.make_async_copy(xs_scr.at[buf, pl.ds(src, RUN_ALIGN)],
                                         xbuf_ref.at[pl.ds(dst, RUN_ALIGN)], sems.at[buf])
        _chunk_copies(chunk_dst_ref, n_chunks_ref, window, n_rows // RUN_ALIGN, copy, wait)

    copies(w, cur, wait=False)

    @pl.when(w > 0)
    def _():
        copies(w - 1, 1 - cur, wait=True)

    @pl.when(w == n_win - 1)
    def _():
        copies(w, cur, wait=True)
        sem = sems.at[cur]

        def pad_copy(e, j):
            dst = pl.multiple_of(pad_start_ref[e] + j * RUN_ALIGN, RUN_ALIGN)
            return pltpu.make_async_copy(zero_scr.at[pl.ds(0, RUN_ALIGN)],
                                         xbuf_ref.at[pl.ds(dst, RUN_ALIGN)], sem)

        def pads(wait):
            def per_expert(e, carry):
                def per_chunk(j, carry2):
                    if wait:
                        pad_copy(e, j).wait()
                    else:
                        pad_copy(e, j).start()
                    return carry2
                return lax.fori_loop(0, pad_chunks_ref[e], per_chunk, carry)
            lax.fori_loop(0, n_exp, per_expert, 0)

        pads(False)
        pads(True)
        lax.fori_loop(0, n_tail, lambda j, c: (tail_copy(j).wait(), c)[1], 0)


def _dispatch(h2, slot_t, plan, rows, n_exp, cap, tile):
    n, d = h2.shape
    win = rows.tile
    return pl.pallas_call(
        functools.partial(_dispatch_kernel, n_exp=n_exp, n_win=rows.grid, tile=tile),
        grid_spec=pltpu.PrefetchScalarGridSpec(
            num_scalar_prefetch=5,
            grid=(rows.grid,),
            in_specs=[rows.rows(d), pl.BlockSpec((TOP_K, win), lambda i, *_: (0, i))],
            out_specs=pl.BlockSpec(memory_space=pl.ANY),
            scratch_shapes=[pltpu.VMEM((2, _sorted_rows(win, n_exp), d), BF16),
                            pltpu.VMEM((tile, d), BF16),
                            pltpu.SemaphoreType.DMA((2,)), pltpu.SemaphoreType.DMA(())]),
        out_shape=jax.ShapeDtypeStruct((cap, d), BF16),
        compiler_params=_cparams("arbitrary"),
        name="moe_dispatch",
    )(plan["chunk_dst"], plan["n_chunks"], plan["pad_start"], plan["pad_chunks"],
      plan["used_tiles"], h2, slot_t)


def _experts_kernel(texp_ref, used_ref, xbuf_ref, wg_ref, wu_ref, wd_ref, y_ref,
                    x_ring, sems, wgu_scr, wd_scr, *, sub_tiles):
    i = pl.program_id(0)
    used = used_ref[0]
    live = i < used
    de = wd_scr.shape[0]
    depth, tile = x_ring.shape[0], x_ring.shape[1]

    def fetch(j):
        slot = j % depth
        return pltpu.make_async_copy(xbuf_ref.at[pl.ds(pl.multiple_of(j * tile, tile), tile)],
                                     x_ring.at[slot], sems.at[slot])

    @pl.when(i == 0)
    def _():
        for j in range(depth - 1):
            @pl.when(j < used)
            def _():
                fetch(j).start()

    @pl.when(i + (depth - 1) < used)
    def _():
        fetch(i + (depth - 1)).start()

    @pl.when(jnp.logical_or(i == 0, texp_ref[i] != texp_ref[jnp.maximum(i - 1, 0)]))
    def _():
        wgu_scr[:, :de] = wg_ref[...].astype(BF16)
        wgu_scr[:, de:] = wu_ref[...].astype(BF16)
        wd_scr[...] = wd_ref[...].astype(BF16)

    @pl.when(live)
    def _():
        fetch(i).wait()
        x_ref = x_ring.at[i % depth]
        sub = tile // sub_tiles
        for s in range(sub_tiles):
            rows = slice(s * sub, (s + 1) * sub)
            ab = _dot(x_ref[rows, :], wgu_scr[...])
            mid = (_silu(ab[:, :de]) * ab[:, de:]).astype(BF16)
            y_ref[rows, :] = _dot(mid, wd_scr[...]).astype(BF16)

    @pl.when(jnp.logical_not(live))
    def _():
        y_ref[...] = jnp.zeros_like(y_ref)


def _experts(xbuf, plan, layer, w_gate, w_up, w_down, tile):
    cap, d = xbuf.shape
    _, n_exp, _, de = w_gate.shape
    n_tiles = cap // tile
    used = plan["used_tiles"]
    tmap = jnp.minimum(jnp.arange(n_tiles, dtype=I32), used[0] - 1)
    texp = jnp.sum((plan["pend"][None, :] <= (tmap * tile)[:, None]).astype(I32), axis=1)
    texp = jnp.minimum(texp, n_exp - 1)
    return pl.pallas_call(
        functools.partial(_experts_kernel, sub_tiles=max(tile // EXPERT_SUB_TILE, 1)),
        grid_spec=pltpu.PrefetchScalarGridSpec(
            num_scalar_prefetch=2,
            grid=(n_tiles,),
            in_specs=[pl.BlockSpec(memory_space=pl.ANY),
                      pl.BlockSpec((None, None, d, de), lambda i, te, u: (layer, te[i], 0, 0)),
                      pl.BlockSpec((None, None, d, de), lambda i, te, u: (layer, te[i], 0, 0)),
                      pl.BlockSpec((None, None, de, d), lambda i, te, u: (layer, te[i], 0, 0))],
            out_specs=pl.BlockSpec((tile, d), lambda i, te, u: (i, 0)),
            scratch_shapes=[pltpu.VMEM((EXPERT_RING, tile, d), BF16),
                            pltpu.SemaphoreType.DMA((EXPERT_RING,)),
                            pltpu.VMEM((d, 2 * de), BF16), pltpu.VMEM((de, d), BF16)]),
        out_shape=jax.ShapeDtypeStruct((cap, d), BF16),
        compiler_params=_cparams("arbitrary"),
        name="moe_experts",
    )(texp, used, xbuf, w_gate, w_up, w_down)


def _combine_kernel(chunk_dst_ref, n_chunks_ref, ybuf_ref, slot_ref, wt_ref, h2_ref, x_ref,
                    g2_ref, sg_ref, su_ref, sd_ref, fin_ref, out_ref, ys_scr, sems,
                    *, n_win, final_norm):
    w = pl.program_id(0)
    cur = w % 2
    n_rows = ys_scr.shape[1]

    def copies(window, buf, wait):
        def copy(loc, dst):
            return pltpu.make_async_copy(ybuf_ref.at[pl.ds(dst, RUN_ALIGN)],
                                         ys_scr.at[buf, pl.ds(loc, RUN_ALIGN)], sems.at[buf])
        _chunk_copies(chunk_dst_ref, n_chunks_ref, window, n_rows // RUN_ALIGN, copy, wait)

    @pl.when(w == 0)
    def _():
        ys_scr[...] = jnp.zeros_like(ys_scr)
        copies(w, cur, wait=False)

    @pl.when(w + 1 < n_win)
    def _():
        copies(w + 1, 1 - cur, wait=False)

    h2 = h2_ref[...]
    shared = _dot((_silu(_dot(h2, sg_ref[...])) * _dot(h2, su_ref[...])).astype(BF16), sd_ref[...])
    copies(w, cur, wait=True)

    slot, wt = slot_ref[...], wt_ref[...].astype(BF16)
    win = slot.shape[1]
    local = lax.broadcasted_iota(I32, (ONEHOT_BLOCK, win), 0).astype(F32).astype(BF16)
    slot_block = slot // ONEHOT_BLOCK
    slot_local = (slot % ONEHOT_BLOCK).astype(F32)
    routed = shared
    for blk in range(n_rows // ONEHOT_BLOCK):
        key = jnp.where(slot_block == blk, slot_local, -1.0).astype(BF16)
        weights = jnp.where(local == key[0:1, :], wt[0:1, :], jnp.zeros((), BF16))
        for k in range(1, TOP_K):
            weights = jnp.where(local == key[k:k + 1, :], wt[k:k + 1, :], weights)
        lo = blk * ONEHOT_BLOCK
        routed = routed + _dot_tn(weights, ys_scr[cur, lo:lo + ONEHOT_BLOCK, :])
    out = x_ref[...] + g2_ref[...] * routed
    if final_norm:
        out = out * lax.rsqrt(jnp.mean(out * out, axis=-1, keepdims=True) + EPS) * fin_ref[...]
    out_ref[...] = out


def _combine(ybuf, plan, slot, wt, h2, x, mod, rows, sh_bf16, final_g, n_exp, final_norm):
    n, d = x.shape
    win = rows.tile
    sg, su, sd = sh_bf16
    pairs = pl.BlockSpec((TOP_K, win), lambda i, *_: (0, i))
    return pl.pallas_call(
        functools.partial(_combine_kernel, n_win=rows.grid, final_norm=final_norm),
        grid_spec=pltpu.PrefetchScalarGridSpec(
            num_scalar_prefetch=2,
            grid=(rows.grid,),
            in_specs=[pl.BlockSpec(memory_space=pl.ANY), pairs, pairs, rows.rows(d), rows.rows(d),
                      rows.mod(d, 5), _resident(sg.shape), _resident(su.shape),
                      _resident(sd.shape), _resident((1, d))],
            out_specs=rows.rows(d),
            scratch_shapes=[pltpu.VMEM((2, _sorted_rows(win, n_exp), d), BF16),
                            pltpu.SemaphoreType.DMA((2,))]),
        out_shape=jax.ShapeDtypeStruct((n, d), F32),
        compiler_params=_cparams("arbitrary"),
        name="moe_combine",
    )(plan["chunk_dst"], plan["n_chunks"], ybuf, slot, wt, h2, x, mod,
      sg, su, sd, final_g.reshape(1, d))


def _moe(x, mod, rows, layer, norm_g, w_router, bias, w_gate, w_up, w_down, sh_bf16, final_g,
         final_norm, tile):
    n = x.shape[0]
    n_exp = w_router.shape[1]
    h2, slot_t, wt_t, cnt = _router(x, mod, rows, norm_g, w_router, bias)
    plan = _plan(cnt, tile, _sorted_rows(rows.tile, n_exp))
    cap = _capacity(n, rows.grid, n_exp, tile)
    xbuf = _dispatch(h2, slot_t, plan, rows, n_exp, cap, tile)
    ybuf = _experts(xbuf, plan, layer, w_gate, w_up, w_down, tile)
    return _combine(ybuf, plan, slot_t, wt_t, h2, x, mod, rows, sh_bf16, final_g,
                    n_exp, final_norm)


def _rope_tables(pos, half):
    inv = ROPE_BASE ** (-jnp.arange(half, dtype=F32) / half)
    ang = pos.astype(F32)[:, None] * inv[None, :]
    return jnp.cos(ang), jnp.sin(ang)


def _trunk(x3, mod_all, pos0, s_ret, s_gla, wts, expert_tile):
    batch, seq, d = x3.shape
    n = batch * seq
    x = x3.reshape(n, d)
    rows = _Rows(n, seq, min(ROW_TILE, n))
    decode = seq == 1
    ret_dk = d // RET_HEADS
    ret_dv = 2 * ret_dk
    gla_dk = d // (2 * GLA_HEADS)
    gla_dv = d // GLA_HEADS
    depth = wts["ada_w"].shape[0]
    new_ret, new_gla = [], []
    for layer in range(depth):
        mod = mod_all[layer] if decode else mod_all[layer].reshape(batch, 1, 6 * d)
        j = layer // 2
        if layer % 2 == 0:
            pos = (jnp.full((rows.tile,), pos0, I32) if decode
                   else pos0 + jnp.arange(seq, dtype=I32))
            cos, sin = _rope_tables(pos, ret_dk // 2)
            q, k, v, gate = _ret_proj(x, mod, rows, wts["norm_mix_g"][layer], wts["ret_w_in"][j],
                                      cos, sin, RET_HEADS, ret_dk, ret_dv)
            log_gamma = jnp.log1p(-jnp.power(2.0, -5.0 - jnp.arange(RET_HEADS, dtype=F32)))
            if decode:
                decay = jnp.broadcast_to(jnp.repeat(jnp.exp(log_gamma), ret_dk)[None, :],
                                         (n, RET_HEADS * ret_dk))
                o, s_new = _step(s_ret, j, q, k, decay, v, RET_HEADS, ret_dk, ret_dv)
            else:
                o, s_new = _ret_core(q, k, v, log_gamma, batch, seq, RET_HEADS, ret_dk, ret_dv)
            new_ret.append(s_new)
            x = _post(o, gate, x, mod, rows, wts["ret_norm_g"][j], wts["ret_w_out"][j],
                      RET_HEADS, ret_dv, center=True)
        else:
            q, k, v, gate, la = _gla_proj(x, mod, rows, wts["norm_mix_g"][layer], wts["gla_w_in"][j],
                                          wts["gla_w_a1"][j], wts["gla_w_a2"][j], wts["gla_b_a"][j],
                                          GLA_HEADS * gla_dk, GLA_HEADS * gla_dv)
            if decode:
                o, s_new = _step(s_gla, j, q, k, jnp.exp(la), v, GLA_HEADS, gla_dk, gla_dv)
            else:
                o, s_new = _gla_core(q, k, v, la, batch, seq, GLA_HEADS, gla_dk, gla_dv)
            new_gla.append(s_new)
            x = _post(o, gate, x, mod, rows, wts["gla_norm_g"][j], wts["gla_w_out"][j],
                      GLA_HEADS, gla_dv, center=False)
        x = _moe(x, mod, rows, layer, wts["norm_ffn_g"][layer], wts["moe_w_router"][layer],
                 wts["moe_router_bias"][layer], wts["moe_w_gate"], wts["moe_w_up"],
                 wts["moe_w_down"], wts["shared"][layer], wts["final_norm_g"],
                 final_norm=layer == depth - 1, tile=expert_tile)
    return x.reshape(batch, seq, d), jnp.stack(new_ret), jnp.stack(new_gla)


def kernel(x_prompt, x_sample, state_ret, state_gla, c_prompt, c_sample, ret_w_in, ret_norm_g, ret_w_out, gla_w_in, gla_w_a1, gla_w_a2, gla_b_a, gla_norm_g, gla_w_out, ada_w, ada_b, norm_mix_g, norm_ffn_g, moe_w_router, moe_router_bias, moe_w_gate, moe_w_up, moe_w_down, sh_w_gate, sh_w_up, sh_w_down, final_norm_g):
    b = x_prompt.shape[0]
    depth = ada_w.shape[0]
    rank = gla_w_a1.shape[-1]
    pad = LANES - rank
    wts = dict(
        ret_w_in=ret_w_in.astype(BF16), ret_norm_g=ret_norm_g, ret_w_out=ret_w_out.astype(BF16),
        gla_w_in=gla_w_in.astype(BF16),
        gla_w_a1=jnp.pad(gla_w_a1, ((0, 0), (0, 0), (0, pad))).astype(BF16),
        gla_w_a2=jnp.pad(gla_w_a2, ((0, 0), (0, pad), (0, 0))).astype(BF16),
        gla_b_a=gla_b_a, gla_norm_g=gla_norm_g, gla_w_out=gla_w_out.astype(BF16),
        ada_w=ada_w, norm_mix_g=norm_mix_g, norm_ffn_g=norm_ffn_g,
        moe_w_router=moe_w_router, moe_router_bias=moe_router_bias,
        moe_w_gate=moe_w_gate, moe_w_up=moe_w_up, moe_w_down=moe_w_down,
        shared=[(sh_w_gate[l].astype(BF16), sh_w_up[l].astype(BF16), sh_w_down[l].astype(BF16))
                for l in range(depth)],
        final_norm_g=final_norm_g)
    mod = _ada(jnp.concatenate([c_prompt, c_sample], axis=0), ada_w, ada_b)
    y_p, ret_p, gla_p = _trunk(x_prompt, mod[:, :b], 0, None, None, wts, expert_tile=512)
    y_s, ret_s, gla_s = _trunk(x_sample, mod[:, b:], PAST_LEN, state_ret, state_gla, wts,
                               expert_tile=128)
    return (y_p, y_s, ret_p, gla_p, ret_s, gla_s)
```

```python
import functools

import jax
import jax.numpy as jnp
from jax import lax
from jax.experimental import pallas as pl
from jax.experimental.pallas import tpu as pltpu

F32, BF16, I32 = jnp.float32, jnp.bfloat16, jnp.int32

EPS = 1e-6
ROPE_BASE = 10000.0
PAST_LEN = 16384
RET_HEADS = 4
GLA_HEADS = 4
GLA_TAU = 16.0
N_GROUPS = 8
TOPK_GROUPS = 4
TOP_K = 8
ROUTED_SCALE = 2.5

LANES = 128
BF16_SUBLANES = 16
VMEM_LIMIT_BYTES = 56 * 1024 * 1024

ROW_TILE = 256
RET_CHUNK = 256
GLA_CHUNK = 128
GLA_KEY_BLOCK = 32
STEP_TOKENS = 8
RUN_ALIGN = BF16_SUBLANES
EXPERT_SUB_TILE = 256
SORT_BLOCK = 512
EXPERT_RING = 3
ONEHOT_BLOCK = 256


def _cparams(*sem):
    return pltpu.CompilerParams(dimension_semantics=sem, vmem_limit_bytes=VMEM_LIMIT_BYTES)


def _sigmoid(x):
    return 1.0 / (1.0 + jnp.exp(-x))


def _silu(x):
    return x * _sigmoid(x)


def _modulate(x, g, shift, scale):
    y = x * lax.rsqrt(jnp.mean(x * x, axis=-1, keepdims=True) + EPS) * g
    return y * (1.0 + scale) + shift


def _dot(a, b):
    return jnp.dot(a, b, preferred_element_type=F32)


def _dot_nt(a, b):
    return lax.dot_general(a, b, (((1,), (1,)), ((), ())), preferred_element_type=F32)


def _dot_tn(a, b):
    return lax.dot_general(a, b, (((0,), (0,)), ((), ())), preferred_element_type=F32)


def _resident(shape):
    zeros = (0,) * len(shape)
    return pl.BlockSpec(shape, lambda *_: zeros, pipeline_mode=pl.Buffered(1))


class _Rows:
    def __init__(self, n_rows, seq_len, tile):
        self.n, self.tile = n_rows, tile
        self.per_row = seq_len == 1
        self.tiles_per_seq = max(seq_len // tile, 1)
        assert n_rows % tile == 0 and (self.per_row or seq_len % tile == 0)
        self.grid = n_rows // tile

    def rows(self, width, col=0):
        return pl.BlockSpec((self.tile, width), lambda i, *_: (i, col))

    def mod(self, d, col):
        if self.per_row:
            return pl.BlockSpec((self.tile, d), lambda i, *_: (i, col))
        tps = self.tiles_per_seq
        return pl.BlockSpec((None, 1, d), lambda i, *_: (i // tps, 0, col))


def _ada_kernel(c_ref, w_ref, b_ref, o_ref):
    s = _silu(c_ref[...]).astype(BF16)
    o_ref[0] = _dot(s, w_ref[0].astype(BF16)) + b_ref[0]


def _ada(c_all, ada_w, ada_b):
    depth, d, d6 = ada_w.shape
    n = c_all.shape[0]
    tn = d6 // 4
    return pl.pallas_call(
        _ada_kernel,
        grid=(depth, d6 // tn),
        in_specs=[pl.BlockSpec((n, d), lambda l, j: (0, 0)),
                  pl.BlockSpec((1, d, tn), lambda l, j: (l, 0, j)),
                  pl.BlockSpec((1, 1, tn), lambda l, j: (l, 0, j))],
        out_specs=pl.BlockSpec((1, n, tn), lambda l, j: (l, 0, j)),
        out_shape=jax.ShapeDtypeStruct((depth, n, d6), F32),
        compiler_params=_cparams("parallel", "parallel"),
        name="ada_mod",
    )(c_all, ada_w, ada_b.reshape(depth, 1, d6))


def _ret_proj_kernel(x_ref, sh_ref, sc_ref, g_ref, w_ref, cos_ref, sin_ref,
                     q_ref, k_ref, v_ref, gate_ref, *, n_heads, dk, dv):
    h = _modulate(x_ref[...], g_ref[...], sh_ref[...], sc_ref[...]).astype(BF16)
    cos, sin = cos_ref[...], sin_ref[...]
    half, nqk, nv = dk // 2, n_heads * dk, n_heads * dv
    for hd in range(n_heads):
        for dst, base, scale in ((q_ref, 0, None), (k_ref, nqk, dk ** -0.5)):
            p = _dot(h, w_ref[:, base + hd * dk:base + (hd + 1) * dk])
            x1, x2 = p[:, :half], p[:, half:]
            r1, r2 = x1 * cos - x2 * sin, x1 * sin + x2 * cos
            if scale is not None:
                r1, r2 = r1 * scale, r2 * scale
            dst[:, hd * dk:hd * dk + half] = r1.astype(BF16)
            dst[:, hd * dk + half:(hd + 1) * dk] = r2.astype(BF16)
    for hd in range(n_heads):
        v_ref[:, hd * dv:(hd + 1) * dv] = _dot(
            h, w_ref[:, 2 * nqk + hd * dv:2 * nqk + (hd + 1) * dv]).astype(BF16)
        gate_ref[:, hd * dv:(hd + 1) * dv] = _dot(
            h, w_ref[:, 2 * nqk + nv + hd * dv:2 * nqk + nv + (hd + 1) * dv])


def _ret_proj(x, mod, rows, norm_g, w_in_bf16, cos, sin, n_heads, dk, dv):
    n, d = x.shape
    nqk, nv = n_heads * dk, n_heads * dv
    half = dk // 2
    tps = rows.tiles_per_seq
    trig = (pl.BlockSpec((rows.tile, half), lambda i: (0, 0)) if rows.per_row
            else pl.BlockSpec((rows.tile, half), lambda i: (i % tps, 0)))
    return pl.pallas_call(
        functools.partial(_ret_proj_kernel, n_heads=n_heads, dk=dk, dv=dv),
        grid=(rows.grid,),
        in_specs=[rows.rows(d), rows.mod(d, 0), rows.mod(d, 1), _resident((1, d)),
                  _resident(w_in_bf16.shape), trig, trig],
        out_specs=[rows.rows(nqk), rows.rows(nqk), rows.rows(nv), rows.rows(nv)],
        out_shape=[jax.ShapeDtypeStruct((n, nqk), BF16), jax.ShapeDtypeStruct((n, nqk), BF16),
                   jax.ShapeDtypeStruct((n, nv), BF16), jax.ShapeDtypeStruct((n, nv), F32)],
        compiler_params=_cparams("parallel"),
        name="ret_proj",
    )(x, mod, mod, norm_g.reshape(1, d), w_in_bf16, cos, sin)


def _ret_core_kernel(lg_ref, q_ref, k_ref, v_ref, o_ref, s_out_ref, s_scr,
                     *, tc, n_chunks, n_heads, dk, dv):
    c = pl.program_id(1)

    @pl.when(c == 0)
    def _():
        s_scr[...] = jnp.zeros_like(s_scr)

    ii = lax.broadcasted_iota(I32, (tc, tc), 0)
    jj = lax.broadcasted_iota(I32, (tc, tc), 1)
    causal = ii >= jj
    lag = (ii - jj).astype(F32)
    row = lax.broadcasted_iota(I32, (tc, 1), 0).astype(F32)
    for hd in range(n_heads):
        lg = lg_ref[hd]
        q, k = q_ref[:, hd * dk:(hd + 1) * dk], k_ref[:, hd * dk:(hd + 1) * dk]
        v = v_ref[:, hd * dv:(hd + 1) * dv]
        dec = jnp.where(causal, jnp.exp(lag * lg), 0.0)
        att = (_dot_nt(q, k) * dec).astype(BF16)
        s_old = s_scr[hd]
        o_ref[:, hd * dv:(hd + 1) * dv] = (
            _dot(att, v) + jnp.exp((row + 1.0) * lg) * _dot(q, s_old.astype(BF16)))
        kd = (k.astype(F32) * jnp.exp((tc - 1.0 - row) * lg)).astype(BF16)
        s_scr[hd] = jnp.exp(jnp.full((1, 1), float(tc), F32) * lg) * s_old + _dot_tn(kd, v)

    @pl.when(c == n_chunks - 1)
    def _():
        s_out_ref[0] = s_scr[...]


def _ret_core(q, k, v, log_gamma, batch, seq, n_heads, dk, dv):
    tc = min(RET_CHUNK, seq)
    nc = seq // tc
    assert seq % tc == 0
    n = batch * seq
    qk = pl.BlockSpec((tc, n_heads * dk), lambda b, c: (b * nc + c, 0))
    vv = pl.BlockSpec((tc, n_heads * dv), lambda b, c: (b * nc + c, 0))
    return pl.pallas_call(
        functools.partial(_ret_core_kernel, tc=tc, n_chunks=nc, n_heads=n_heads, dk=dk, dv=dv),
        grid=(batch, nc),
        in_specs=[pl.BlockSpec(memory_space=pltpu.SMEM), qk, qk, vv],
        out_specs=[vv, pl.BlockSpec((1, n_heads, dk, dv), lambda b, c: (b, 0, 0, 0))],
        out_shape=[jax.ShapeDtypeStruct((n, n_heads * dv), F32),
                   jax.ShapeDtypeStruct((batch, n_heads, dk, dv), F32)],
        scratch_shapes=[pltpu.VMEM((n_heads, dk, dv), F32)],
        compiler_params=_cparams("parallel", "arbitrary"),
        name="ret_core",
    )(log_gamma, q, k, v)


def _step_kernel(s_ref, q_ref, k_ref, a_ref, v_ref, o_ref, s_out_ref, *, tb):
    for j in range(tb):
        s_new = (a_ref[0, 0, :, j:j + 1] * s_ref[j, 0]
                 + k_ref[0, 0, :, j:j + 1] * v_ref[j:j + 1, :])
        s_out_ref[j, 0] = s_new
        o_ref[j:j + 1, :] = jnp.sum(q_ref[0, 0, :, j:j + 1] * s_new, axis=0, keepdims=True)


def _columns(x, n_heads, dk, tb):
    n = x.shape[0]
    return x.astype(F32).reshape(n // tb, tb, n_heads, dk).transpose(2, 0, 3, 1)


def _step(states, layer, q, k, a, v, n_heads, dk, dv):
    n = states.shape[1]
    tb = STEP_TOKENS
    assert n % tb == 0
    col = pl.BlockSpec((1, 1, dk, tb), lambda i, h: (h, i, 0, 0))
    st_in = pl.BlockSpec((None, tb, 1, dk, dv), lambda i, h: (layer, i, h, 0, 0))
    st_out = pl.BlockSpec((tb, 1, dk, dv), lambda i, h: (i, h, 0, 0))
    row = pl.BlockSpec((tb, dv), lambda i, h: (i, h))
    return pl.pallas_call(
        functools.partial(_step_kernel, tb=tb),
        grid=(n // tb, n_heads),
        in_specs=[st_in, col, col, col, row],
        out_specs=[row, st_out],
        out_shape=[jax.ShapeDtypeStruct((n, n_heads * dv), F32),
                   jax.ShapeDtypeStruct(states.shape[1:], states.dtype)],
        compiler_params=_cparams("parallel", "parallel"),
        name="state_step",
    )(states, _columns(q, n_heads, dk, tb), _columns(k, n_heads, dk, tb),
      _columns(a, n_heads, dk, tb), v.astype(F32))


def _gla_proj_kernel(x_ref, sh_ref, sc_ref, g_ref, w_ref, a1_ref, a2_ref, ba_ref,
                     q_ref, k_ref, v_ref, r_ref, la_ref, *, nqk, nv):
    h = _modulate(x_ref[...], g_ref[...], sh_ref[...], sc_ref[...]).astype(BF16)
    q_ref[...] = _dot(h, w_ref[:, :nqk]) * ((nqk // GLA_HEADS) ** -0.5)
    k_ref[...] = _dot(h, w_ref[:, nqk:2 * nqk])
    v_ref[...] = _dot(h, w_ref[:, 2 * nqk:2 * nqk + nv]).astype(BF16)
    r_ref[...] = _dot(h, w_ref[:, 2 * nqk + nv:])
    z = _dot(_dot(h, a1_ref[...]).astype(BF16), a2_ref[...]) + ba_ref[...]
    la_ref[...] = (jnp.minimum(z, 0.0) - jnp.log1p(jnp.exp(-jnp.abs(z)))) / GLA_TAU


def _gla_proj(x, mod, rows, norm_g, w_in_bf16, a1_bf16, a2_bf16, b_a, nqk, nv):
    n, d = x.shape
    return pl.pallas_call(
        functools.partial(_gla_proj_kernel, nqk=nqk, nv=nv),
        grid=(rows.grid,),
        in_specs=[rows.rows(d), rows.mod(d, 0), rows.mod(d, 1), _resident((1, d)),
                  _resident(w_in_bf16.shape), _resident(a1_bf16.shape),
                  _resident(a2_bf16.shape), _resident((1, nqk))],
        out_specs=[rows.rows(nqk), rows.rows(nqk), rows.rows(nv), rows.rows(nv), rows.rows(nqk)],
        out_shape=[jax.ShapeDtypeStruct((n, nqk), F32), jax.ShapeDtypeStruct((n, nqk), F32),
                   jax.ShapeDtypeStruct((n, nv), BF16), jax.ShapeDtypeStruct((n, nv), F32),
                   jax.ShapeDtypeStruct((n, nqk), F32)],
        compiler_params=_cparams("parallel"),
        name="gla_proj",
    )(x, mod, mod, norm_g.reshape(1, d), w_in_bf16, a1_bf16, a2_bf16, b_a.reshape(1, nqk))


def _gla_core_kernel(q_ref, k_ref, v_ref, la_ref, o_ref, s_out_ref, s_scr,
                     *, tc, n_chunks, n_heads, dk, dv):
    c = pl.program_id(1)

    @pl.when(c == 0)
    def _():
        s_scr[...] = jnp.zeros_like(s_scr)

    ii = lax.broadcasted_iota(I32, (tc, tc), 0)
    jj = lax.broadcasted_iota(I32, (tc, tc), 1)
    la = la_ref[...]
    la_hi = la.astype(BF16)
    la_mid = (la - la_hi.astype(F32)).astype(BF16)
    la_lo = (la - la_hi.astype(F32) - la_mid.astype(F32)).astype(BF16)
    tri = jnp.where(ii >= jj, 1.0, 0.0).astype(BF16)
    b_all = _dot(tri, la_hi) + _dot(tri, la_mid) + _dot(tri, la_lo)
    ones = jnp.ones((tc, LANES), BF16)
    tot_col_all = (_dot_tn(la_hi, ones) + _dot_tn(la_mid, ones) + _dot_tn(la_lo, ones))[:, 0:1]
    qi = lax.broadcasted_iota(I32, (tc, GLA_KEY_BLOCK), 0)
    kj = lax.broadcasted_iota(I32, (tc, GLA_KEY_BLOCK), 1)
    for hd in range(n_heads):
        cols = slice(hd * dk, (hd + 1) * dk)
        b = b_all[:, cols]
        q, k, v = q_ref[:, cols], k_ref[:, cols], v_ref[:, hd * dv:(hd + 1) * dv]
        s_old = s_scr[hd]
        o = _dot((q * jnp.exp(b)).astype(BF16), s_old.astype(BF16))
        for s in range(tc // GLA_KEY_BLOCK):
            lo = s * GLA_KEY_BLOCK
            blk = slice(lo, lo + GLA_KEY_BLOCK)
            mid = lo + GLA_KEY_BLOCK // 2 - 1
            ref = b[mid:mid + 1, :]
            rel = b - ref
            if s:
                rel = jnp.where(qi[:, 0:1] < lo, 0.0, rel)
            qs = (q * jnp.exp(rel)).astype(BF16)
            ks = (k[blk, :] * jnp.exp(ref - b[blk, :])).astype(BF16)
            att = jnp.where(qi >= kj + lo, _dot_nt(qs, ks), 0.0).astype(BF16)
            o = o + _dot(att, v[blk, :])
        o_ref[:, hd * dv:(hd + 1) * dv] = o
        kd = (k * jnp.exp(b[tc - 1:tc, :] - b)).astype(BF16)
        s_scr[hd] = jnp.exp(tot_col_all[cols, :]) * s_old + _dot_tn(kd, v)

    @pl.when(c == n_chunks - 1)
    def _():
        s_out_ref[0] = s_scr[...]


def _gla_core(q, k, v, la, batch, seq, n_heads, dk, dv):
    tc = min(GLA_CHUNK, seq)
    nc = seq // tc
    assert seq % tc == 0
    n = batch * seq
    qk = pl.BlockSpec((tc, n_heads * dk), lambda b, c: (b * nc + c, 0))
    vv = pl.BlockSpec((tc, n_heads * dv), lambda b, c: (b * nc + c, 0))
    return pl.pallas_call(
        functools.partial(_gla_core_kernel, tc=tc, n_chunks=nc, n_heads=n_heads, dk=dk, dv=dv),
        grid=(batch, nc),
        in_specs=[qk, qk, vv, qk],
        out_specs=[vv, pl.BlockSpec((1, n_heads, dk, dv), lambda b, c: (b, 0, 0, 0))],
        out_shape=[jax.ShapeDtypeStruct((n, n_heads * dv), F32),
                   jax.ShapeDtypeStruct((batch, n_heads, dk, dv), F32)],
        scratch_shapes=[pltpu.VMEM((n_heads, dk, dv), F32)],
        compiler_params=_cparams("parallel", "arbitrary"),
        name="gla_core",
    )(q, k, v, la)


def _post_kernel(o_ref, gate_ref, x_ref, g1_ref, ng_ref, w_ref, out_ref, *, n_heads, dv, center):
    acc = None
    for hd in range(n_heads):
        sl = slice(hd * dv, (hd + 1) * dv)
        o = o_ref[:, sl]
        if center:
            o = o - jnp.mean(o, axis=-1, keepdims=True)
        y = o * lax.rsqrt(jnp.mean(o * o, axis=-1, keepdims=True) + EPS) * ng_ref[:, sl]
        y = (y * _silu(gate_ref[:, sl])).astype(BF16)
        part = _dot(y, w_ref[sl, :])
        acc = part if acc is None else acc + part
    out_ref[...] = x_ref[...] + g1_ref[...] * acc


def _post(o, gate, x, mod, rows, norm_g, w_out_bf16, n_heads, dv, center):
    n, d = x.shape
    nv = n_heads * dv
    return pl.pallas_call(
        functools.partial(_post_kernel, n_heads=n_heads, dv=dv, center=center),
        grid=(rows.grid,),
        in_specs=[rows.rows(nv), rows.rows(nv), rows.rows(d), rows.mod(d, 2),
                  _resident((1, nv)), _resident(w_out_bf16.shape)],
        out_specs=rows.rows(d),
        out_shape=jax.ShapeDtypeStruct((n, d), F32),
        compiler_params=_cparams("parallel"),
        name="mixer_post",
    )(o, gate, x, mod, norm_g.reshape(1, nv), w_out_bf16)


def _first_max(vals, idx_iota, n):
    m = jnp.max(vals, axis=0, keepdims=True)
    i = jnp.min(jnp.where(vals == m, idx_iota, n), axis=0, keepdims=True)
    return m, i


def _router_kernel(x_ref, sh_ref, sc_ref, g_ref, wr_ref, b_ref,
                   h2_ref, slot_ref, wt_ref, cnt_ref):
    h2 = _modulate(x_ref[...], g_ref[...], sh_ref[...], sc_ref[...])
    h2_ref[...] = h2.astype(BF16)
    h_hi = h2.astype(BF16)
    h_lo = (h2 - h_hi.astype(F32)).astype(BF16)
    wr = wr_ref[...]
    w_hi = wr.astype(BF16)
    w_lo = (wr - w_hi.astype(F32)).astype(BF16)
    logits = _dot_nt(w_hi, h_hi) + _dot_nt(w_hi, h_lo) + _dot_nt(w_lo, h_hi)
    scores = _sigmoid(logits)
    biased = scores + b_ref[...]
    n_exp, win = biased.shape
    gsz = n_exp // N_GROUPS
    neg = -jnp.inf

    sub_g = lax.broadcasted_iota(I32, (gsz, win), 0)
    group_rows = []
    for g in range(N_GROUPS):
        blk = biased[g * gsz:(g + 1) * gsz, :]
        m1, i1 = _first_max(blk, sub_g, gsz)
        m2 = jnp.max(jnp.where(sub_g == i1, neg, blk), axis=0, keepdims=True)
        group_rows.append(m1 + m2)
    cur = jnp.concatenate(group_rows, axis=0)
    sub_n = lax.broadcasted_iota(I32, (N_GROUPS, win), 0)
    gmask = jnp.zeros((N_GROUPS, win), jnp.bool_)
    for _ in range(TOPK_GROUPS):
        _, i = _first_max(cur, sub_n, N_GROUPS)
        sel = sub_n == i
        gmask = gmask | sel
        cur = jnp.where(sel, neg, cur)
    emask = jnp.concatenate(
        [jnp.broadcast_to(gmask[g:g + 1, :], (gsz, win)) for g in range(N_GROUPS)], axis=0)
    masked = jnp.where(emask, biased, neg)

    sub_e = lax.broadcasted_iota(I32, (n_exp, win), 0)
    sels, picked = [], []
    for _ in range(TOP_K):
        _, i = _first_max(masked, sub_e, n_exp)
        sel = sub_e == i
        sels.append(sel)
        picked.append(jnp.sum(jnp.where(sel, scores, 0.0), axis=0, keepdims=True))
        masked = jnp.where(sel, neg, masked)
    total = picked[0]
    for p in picked[1:]:
        total = total + p
    wt_ref[...] = jnp.concatenate([p / total * ROUTED_SCALE for p in picked], axis=0)

    chosen = sels[0]
    for s in sels[1:]:
        chosen = chosen | s
    sel_b = jnp.where(chosen, 1.0, 0.0).astype(BF16)
    r = lax.broadcasted_iota(I32, (win, win), 0)
    c = lax.broadcasted_iota(I32, (win, win), 1)
    before = jnp.where(r < c, 1.0, 0.0).astype(BF16)
    rank = _dot(sel_b, before)
    count = _dot(sel_b, jnp.ones((win, win), BF16))
    run_len = jnp.floor((count + (RUN_ALIGN - 1.0)) * (1.0 / RUN_ALIGN)) * RUN_ALIGN
    er = lax.broadcasted_iota(I32, (n_exp, n_exp), 0)
    ec = lax.broadcasted_iota(I32, (n_exp, n_exp), 1)
    lower = jnp.where(ec < er, 1.0, 0.0).astype(BF16)
    slot_all = _dot(lower, run_len.astype(BF16)) + rank
    slot_ref[...] = jnp.concatenate(
        [jnp.sum(jnp.where(s, slot_all, 0.0), axis=0, keepdims=True) for s in sels],
        axis=0).astype(I32)
    cnt_ref[0] = _dot_nt(jnp.ones((8, win), BF16), sel_b)


def _router(x, mod, rows, norm_g, w_router, bias):
    n, d = x.shape
    n_exp = w_router.shape[1]
    win = rows.tile
    lanes = pl.BlockSpec((TOP_K, win), lambda i: (0, i))
    return pl.pallas_call(
        _router_kernel,
        grid=(rows.grid,),
        in_specs=[rows.rows(d), rows.mod(d, 3), rows.mod(d, 4), _resident((1, d)),
                  _resident((n_exp, d)), _resident((n_exp, 1))],
        out_specs=[rows.rows(d), lanes, lanes, pl.BlockSpec((1, 8, n_exp), lambda i: (i, 0, 0))],
        out_shape=[jax.ShapeDtypeStruct((n, d), BF16), jax.ShapeDtypeStruct((TOP_K, n), I32),
                   jax.ShapeDtypeStruct((TOP_K, n), F32),
                   jax.ShapeDtypeStruct((rows.grid, 8, n_exp), F32)],
        compiler_params=_cparams("parallel"),
        name="moe_router",
    )(x, mod, mod, norm_g.reshape(1, d), w_router.T, bias.reshape(n_exp, 1))


def _sorted_rows(win, n_exp):
    return -(-(win * TOP_K + n_exp * (RUN_ALIGN - 1)) // SORT_BLOCK) * SORT_BLOCK


def _copy_list(count, src0, dst0, length, step):
    n_exp = count.shape[1]
    end = jnp.cumsum(count, axis=1)
    idx = jnp.arange(length, dtype=I32)
    owner = jnp.sum((end[:, None, :] <= idx[None, :, None]).astype(I32), axis=2)
    own = (owner[:, :, None] == jnp.arange(n_exp, dtype=I32)[None, None, :]).astype(I32)
    pick = lambda a: jnp.sum(own * a[:, None, :], axis=2)
    off = (idx[None, :] - pick(end - count)) * step
    return (pick(src0) + off).reshape(-1), (pick(dst0) + off).reshape(-1), end[:, -1]


def _plan(cnt, tile, sorted_rows):
    c = cnt[:, 0, :].astype(I32)
    run = (c + RUN_ALIGN - 1) // RUN_ALIGN * RUN_ALIGN
    per_exp = jnp.sum(run, axis=0)
    padded = (per_exp + tile - 1) // tile * tile
    pend = jnp.cumsum(padded)
    pstart = pend - padded
    dest = pstart[None, :] + jnp.cumsum(run, axis=0) - run
    local = jnp.cumsum(run, axis=1) - run
    n_double = run // (2 * RUN_ALIGN)
    covered = n_double * (2 * RUN_ALIGN)
    doubles = _copy_list(n_double, local, dest, sorted_rows // (2 * RUN_ALIGN), 2 * RUN_ALIGN)
    singles = _copy_list((run - covered) // RUN_ALIGN, local + covered, dest + covered,
                         c.shape[1], RUN_ALIGN)
    return dict(
        copies=doubles + singles,
        pad_start=pstart + per_exp, pad_chunks=(padded - per_exp) // RUN_ALIGN,
        pend=pend, used_tiles=(pend[-1] // tile).reshape(1))


def _capacity(n, n_win, n_exp, tile):
    worst = n * TOP_K + n_win * n_exp * (RUN_ALIGN - 1) + n_exp * (tile - RUN_ALIGN)
    return -(-worst // tile) * tile


def _window_copies(copy_refs, window, sorted_rows, n_exp, make_copy, wait):
    d_src, d_dst, d_n, s_src, s_dst, s_n = copy_refs
    for src_ref, dst_ref, n_ref, length, n_rows in (
            (d_src, d_dst, d_n, sorted_rows // (2 * RUN_ALIGN), 2 * RUN_ALIGN),
            (s_src, s_dst, s_n, n_exp, RUN_ALIGN)):
        def per_copy(i, carry, src_ref=src_ref, dst_ref=dst_ref, length=length, n_rows=n_rows):
            cp = make_copy(pl.multiple_of(src_ref[window * length + i], RUN_ALIGN),
                           pl.multiple_of(dst_ref[window * length + i], RUN_ALIGN), n_rows)
            if wait:
                cp.wait()
            else:
                cp.start()
            return carry

        lax.fori_loop(0, n_ref[window], per_copy, 0)


def _dispatch_kernel(d_src, d_dst, d_n, s_src, s_dst, s_n, pad_start_ref, pad_chunks_ref, used_ref,
                     h2_ref, slot_ref, xbuf_ref, xs_scr, zero_scr, sems, tail_sem,
                     *, n_exp, n_win, tile):
    w = pl.program_id(0)
    cur = w % 2
    n_tail = xbuf_ref.shape[0] // tile - used_ref[0]

    def tail_copy(j):
        dst = pl.multiple_of((used_ref[0] + j) * tile, tile)
        return pltpu.make_async_copy(zero_scr, xbuf_ref.at[pl.ds(dst, tile)], tail_sem)

    @pl.when(w == 0)
    def _():
        zero_scr[...] = jnp.zeros_like(zero_scr)
        lax.fori_loop(0, n_tail, lambda j, c: (tail_copy(j).start(), c)[1], 0)

    slot = slot_ref[...]
    n_rows, win = xs_scr.shape[1], slot.shape[1]
    h2 = h2_ref[...]
    local = lax.broadcasted_iota(I32, (ONEHOT_BLOCK, win), 0).astype(F32).astype(BF16)
    slot_block = slot // ONEHOT_BLOCK
    slot_local = (slot % ONEHOT_BLOCK).astype(F32)
    one, zero = jnp.ones((), BF16), jnp.zeros((), BF16)
    for blk in range(n_rows // ONEHOT_BLOCK):
        key = jnp.where(slot_block == blk, slot_local, -1.0).astype(BF16)
        onehot = jnp.where(local == key[0:1, :], one, zero)
        for k in range(1, TOP_K):
            onehot = jnp.where(local == key[k:k + 1, :], one, onehot)
        lo = blk * ONEHOT_BLOCK
        xs_scr[cur, lo:lo + ONEHOT_BLOCK, :] = _dot(onehot, h2).astype(BF16)

    def copies(window, buf, wait):
        def copy(src, dst, n):
            return pltpu.make_async_copy(xs_scr.at[buf, pl.ds(src, n)],
                                         xbuf_ref.at[pl.ds(dst, n)], sems.at[buf])
        _window_copies((d_src, d_dst, d_n, s_src, s_dst, s_n), window, n_rows, n_exp, copy, wait)

    copies(w, cur, wait=False)

    @pl.when(w > 0)
    def _():
        copies(w - 1, 1 - cur, wait=True)

    @pl.when(w == n_win - 1)
    def _():
        copies(w, cur, wait=True)
        sem = sems.at[cur]

        def pad_copy(e, j):
            dst = pl.multiple_of(pad_start_ref[e] + j * RUN_ALIGN, RUN_ALIGN)
            return pltpu.make_async_copy(zero_scr.at[pl.ds(0, RUN_ALIGN)],
                                         xbuf_ref.at[pl.ds(dst, RUN_ALIGN)], sem)

        def pads(wait):
            def per_expert(e, carry):
                def per_chunk(j, carry2):
                    if wait:
                        pad_copy(e, j).wait()
                    else:
                        pad_copy(e, j).start()
                    return carry2
                return lax.fori_loop(0, pad_chunks_ref[e], per_chunk, carry)
            lax.fori_loop(0, n_exp, per_expert, 0)

        pads(False)
        pads(True)
        lax.fori_loop(0, n_tail, lambda j, c: (tail_copy(j).wait(), c)[1], 0)


def _dispatch(h2, slot_t, plan, rows, n_exp, cap, tile):
    n, d = h2.shape
    win = rows.tile
    return pl.pallas_call(
        functools.partial(_dispatch_kernel, n_exp=n_exp, n_win=rows.grid, tile=tile),
        grid_spec=pltpu.PrefetchScalarGridSpec(
            num_scalar_prefetch=9,
            grid=(rows.grid,),
            in_specs=[rows.rows(d), pl.BlockSpec((TOP_K, win), lambda i, *_: (0, i))],
            out_specs=pl.BlockSpec(memory_space=pl.ANY),
            scratch_shapes=[pltpu.VMEM((2, _sorted_rows(win, n_exp), d), BF16),
                            pltpu.VMEM((tile, d), BF16),
                            pltpu.SemaphoreType.DMA((2,)), pltpu.SemaphoreType.DMA(())]),
        out_shape=jax.ShapeDtypeStruct((cap, d), BF16),
        compiler_params=_cparams("arbitrary"),
        name="moe_dispatch",
    )(*plan["copies"], plan["pad_start"], plan["pad_chunks"], plan["used_tiles"], h2, slot_t)


def _experts_kernel(texp_ref, used_ref, xbuf_ref, wg_ref, wu_ref, wd_ref, y_ref,
                    x_ring, sems, wgu_scr, wd_scr, *, sub_tiles):
    i = pl.program_id(0)
    used = used_ref[0]
    live = i < used
    de = wd_scr.shape[0]
    depth, tile = x_ring.shape[0], x_ring.shape[1]

    def fetch(j):
        slot = j % depth
        return pltpu.make_async_copy(xbuf_ref.at[pl.ds(pl.multiple_of(j * tile, tile), tile)],
                                     x_ring.at[slot], sems.at[slot])

    @pl.when(i == 0)
    def _():
        for j in range(depth - 1):
            @pl.when(j < used)
            def _():
                fetch(j).start()

    @pl.when(i + (depth - 1) < used)
    def _():
        fetch(i + (depth - 1)).start()

    @pl.when(jnp.logical_or(i == 0, texp_ref[i] != texp_ref[jnp.maximum(i - 1, 0)]))
    def _():
        wgu_scr[:, :de] = wg_ref[...].astype(BF16)
        wgu_scr[:, de:] = wu_ref[...].astype(BF16)
        wd_scr[...] = wd_ref[...].astype(BF16)

    @pl.when(live)
    def _():
        fetch(i).wait()
        x_ref = x_ring.at[i % depth]
        sub = tile // sub_tiles
        for s in range(sub_tiles):
            rows = slice(s * sub, (s + 1) * sub)
            ab = _dot(x_ref[rows, :], wgu_scr[...])
            mid = (_silu(ab[:, :de]) * ab[:, de:]).astype(BF16)
            y_ref[rows, :] = _dot(mid, wd_scr[...]).astype(BF16)

    @pl.when(jnp.logical_not(live))
    def _():
        y_ref[...] = jnp.zeros_like(y_ref)


def _experts(xbuf, plan, layer, w_gate, w_up, w_down, tile):
    cap, d = xbuf.shape
    _, n_exp, _, de = w_gate.shape
    n_tiles = cap // tile
    used = plan["used_tiles"]
    tmap = jnp.minimum(jnp.arange(n_tiles, dtype=I32), used[0] - 1)
    texp = jnp.sum((plan["pend"][None, :] <= (tmap * tile)[:, None]).astype(I32), axis=1)
    texp = jnp.minimum(texp, n_exp - 1)
    return pl.pallas_call(
        functools.partial(_experts_kernel, sub_tiles=max(tile // EXPERT_SUB_TILE, 1)),
        grid_spec=pltpu.PrefetchScalarGridSpec(
            num_scalar_prefetch=2,
            grid=(n_tiles,),
            in_specs=[pl.BlockSpec(memory_space=pl.ANY),
                      pl.BlockSpec((None, None, d, de), lambda i, te, u: (layer, te[i], 0, 0)),
                      pl.BlockSpec((None, None, d, de), lambda i, te, u: (layer, te[i], 0, 0)),
                      pl.BlockSpec((None, None, de, d), lambda i, te, u: (layer, te[i], 0, 0))],
            out_specs=pl.BlockSpec((tile, d), lambda i, te, u: (i, 0)),
            scratch_shapes=[pltpu.VMEM((EXPERT_RING, tile, d), BF16),
                            pltpu.SemaphoreType.DMA((EXPERT_RING,)),
                            pltpu.VMEM((d, 2 * de), BF16), pltpu.VMEM((de, d), BF16)]),
        out_shape=jax.ShapeDtypeStruct((cap, d), BF16),
        compiler_params=_cparams("arbitrary"),
        name="moe_experts",
    )(texp, used, xbuf, w_gate, w_up, w_down)


def _combine_kernel(d_src, d_dst, d_n, s_src, s_dst, s_n, ybuf_ref, slot_ref, wt_ref, h2_ref,
                    x_ref, g2_ref, sg_ref, su_ref, sd_ref, fin_ref, out_ref, ys_scr, sems,
                    *, n_exp, n_win, final_norm):
    w = pl.program_id(0)
    cur = w % 2
    n_rows = ys_scr.shape[1]

    def copies(window, buf, wait):
        def copy(loc, dst, n):
            return pltpu.make_async_copy(ybuf_ref.at[pl.ds(dst, n)],
                                         ys_scr.at[buf, pl.ds(loc, n)], sems.at[buf])
        _window_copies((d_src, d_dst, d_n, s_src, s_dst, s_n), window, n_rows, n_exp, copy, wait)

    @pl.when(w == 0)
    def _():
        ys_scr[...] = jnp.zeros_like(ys_scr)
        copies(w, cur, wait=False)

    @pl.when(w + 1 < n_win)
    def _():
        copies(w + 1, 1 - cur, wait=False)

    h2 = h2_ref[...]
    shared = _dot((_silu(_dot(h2, sg_ref[...])) * _dot(h2, su_ref[...])).astype(BF16), sd_ref[...])
    copies(w, cur, wait=True)

    slot, wt = slot_ref[...], wt_ref[...].astype(BF16)
    win = slot.shape[1]
    local = lax.broadcasted_iota(I32, (ONEHOT_BLOCK, win), 0).astype(F32).astype(BF16)
    slot_block = slot // ONEHOT_BLOCK
    slot_local = (slot % ONEHOT_BLOCK).astype(F32)
    routed = shared
    for blk in range(n_rows // ONEHOT_BLOCK):
        key = jnp.where(slot_block == blk, slot_local, -1.0).astype(BF16)
        weights = jnp.where(local == key[0:1, :], wt[0:1, :], jnp.zeros((), BF16))
        for k in range(1, TOP_K):
            weights = jnp.where(local == key[k:k + 1, :], wt[k:k + 1, :], weights)
        lo = blk * ONEHOT_BLOCK
        routed = routed + _dot_tn(weights, ys_scr[cur, lo:lo + ONEHOT_BLOCK, :])
    out = x_ref[...] + g2_ref[...] * routed
    if final_norm:
        out = out * lax.rsqrt(jnp.mean(out * out, axis=-1, keepdims=True) + EPS) * fin_ref[...]
    out_ref[...] = out


def _combine(ybuf, plan, slot, wt, h2, x, mod, rows, sh_bf16, final_g, n_exp, final_norm):
    n, d = x.shape
    win = rows.tile
    sg, su, sd = sh_bf16
    pairs = pl.BlockSpec((TOP_K, win), lambda i, *_: (0, i))
    return pl.pallas_call(
        functools.partial(_combine_kernel, n_exp=n_exp, n_win=rows.grid, final_norm=final_norm),
        grid_spec=pltpu.PrefetchScalarGridSpec(
            num_scalar_prefetch=6,
            grid=(rows.grid,),
            in_specs=[pl.BlockSpec(memory_space=pl.ANY), pairs, pairs, rows.rows(d), rows.rows(d),
                      rows.mod(d, 5), _resident(sg.shape), _resident(su.shape),
                      _resident(sd.shape), _resident((1, d))],
            out_specs=rows.rows(d),
            scratch_shapes=[pltpu.VMEM((2, _sorted_rows(win, n_exp), d), BF16),
                            pltpu.SemaphoreType.DMA((2,))]),
        out_shape=jax.ShapeDtypeStruct((n, d), F32),
        compiler_params=_cparams("arbitrary"),
        name="moe_combine",
    )(*plan["copies"], ybuf, slot, wt, h2, x, mod, sg, su, sd, final_g.reshape(1, d))


def _moe(x, mod, rows, layer, norm_g, w_router, bias, w_gate, w_up, w_down, sh_bf16, final_g,
         final_norm, tile):
    n = x.shape[0]
    n_exp = w_router.shape[1]
    h2, slot_t, wt_t, cnt = _router(x, mod, rows, norm_g, w_router, bias)
    plan = _plan(cnt, tile, _sorted_rows(rows.tile, n_exp))
    cap = _capacity(n, rows.grid, n_exp, tile)
    xbuf = _dispatch(h2, slot_t, plan, rows, n_exp, cap, tile)
    ybuf = _experts(xbuf, plan, layer, w_gate, w_up, w_down, tile)
    return _combine(ybuf, plan, slot_t, wt_t, h2, x, mod, rows, sh_bf16, final_g,
                    n_exp, final_norm)


def _rope_tables(pos, half):
    inv = ROPE_BASE ** (-jnp.arange(half, dtype=F32) / half)
    ang = pos.astype(F32)[:, None] * inv[None, :]
    return jnp.cos(ang), jnp.sin(ang)


def _trunk(x3, mod_all, pos0, s_ret, s_gla, wts, expert_tile):
    batch, seq, d = x3.shape
    n = batch * seq
    x = x3.reshape(n, d)
    rows = _Rows(n, seq, min(ROW_TILE, n))
    decode = seq == 1
    ret_dk = d // RET_HEADS
    ret_dv = 2 * ret_dk
    gla_dk = d // (2 * GLA_HEADS)
    gla_dv = d // GLA_HEADS
    depth = wts["ada_w"].shape[0]
    new_ret, new_gla = [], []
    for layer in range(depth):
        mod = mod_all[layer] if decode else mod_all[layer].reshape(batch, 1, 6 * d)
        j = layer // 2
        if layer % 2 == 0:
            pos = (jnp.full((rows.tile,), pos0, I32) if decode
                   else pos0 + jnp.arange(seq, dtype=I32))
            cos, sin = _rope_tables(pos, ret_dk // 2)
            q, k, v, gate = _ret_proj(x, mod, rows, wts["norm_mix_g"][layer], wts["ret_w_in"][j],
                                      cos, sin, RET_HEADS, ret_dk, ret_dv)
            log_gamma = jnp.log1p(-jnp.power(2.0, -5.0 - jnp.arange(RET_HEADS, dtype=F32)))
            if decode:
                decay = jnp.broadcast_to(jnp.repeat(jnp.exp(log_gamma), ret_dk)[None, :],
                                         (n, RET_HEADS * ret_dk))
                o, s_new = _step(s_ret, j, q, k, decay, v, RET_HEADS, ret_dk, ret_dv)
            else:
                o, s_new = _ret_core(q, k, v, log_gamma, batch, seq, RET_HEADS, ret_dk, ret_dv)
            new_ret.append(s_new)
            x = _post(o, gate, x, mod, rows, wts["ret_norm_g"][j], wts["ret_w_out"][j],
                      RET_HEADS, ret_dv, center=True)
        else:
            q, k, v, gate, la = _gla_proj(x, mod, rows, wts["norm_mix_g"][layer], wts["gla_w_in"][j],
                                          wts["gla_w_a1"][j], wts["gla_w_a2"][j], wts["gla_b_a"][j],
                                          GLA_HEADS * gla_dk, GLA_HEADS * gla_dv)
            if decode:
                o, s_new = _step(s_gla, j, q, k, jnp.exp(la), v, GLA_HEADS, gla_dk, gla_dv)
            else:
                o, s_new = _gla_core(q, k, v, la, batch, seq, GLA_HEADS, gla_dk, gla_dv)
            new_gla.append(s_new)
            x = _post(o, gate, x, mod, rows, wts["gla_norm_g"][j], wts["gla_w_out"][j],
                      GLA_HEADS, gla_dv, center=False)
        x = _moe(x, mod, rows, layer, wts["norm_ffn_g"][layer], wts["moe_w_router"][layer],
                 wts["moe_router_bias"][layer], wts["moe_w_gate"], wts["moe_w_up"],
                 wts["moe_w_down"], wts["shared"][layer], wts["final_norm_g"],
                 final_norm=layer == depth - 1, tile=expert_tile)
    return x.reshape(batch, seq, d), jnp.stack(new_ret), jnp.stack(new_gla)


def kernel(x_prompt, x_sample, state_ret, state_gla, c_prompt, c_sample, ret_w_in, ret_norm_g, ret_w_out, gla_w_in, gla_w_a1, gla_w_a2, gla_b_a, gla_norm_g, gla_w_out, ada_w, ada_b, norm_mix_g, norm_ffn_g, moe_w_router, moe_router_bias, moe_w_gate, moe_w_up, moe_w_down, sh_w_gate, sh_w_up, sh_w_down, final_norm_g):
    b = x_prompt.shape[0]
    depth = ada_w.shape[0]
    rank = gla_w_a1.shape[-1]
    pad = LANES - rank
    wts = dict(
        ret_w_in=ret_w_in.astype(BF16), ret_norm_g=ret_norm_g, ret_w_out=ret_w_out.astype(BF16),
        gla_w_in=gla_w_in.astype(BF16),
        gla_w_a1=jnp.pad(gla_w_a1, ((0, 0), (0, 0), (0, pad))).astype(BF16),
        gla_w_a2=jnp.pad(gla_w_a2, ((0, 0), (0, pad), (0, 0))).astype(BF16),
        gla_b_a=gla_b_a, gla_norm_g=gla_norm_g, gla_w_out=gla_w_out.astype(BF16),
        ada_w=ada_w, norm_mix_g=norm_mix_g, norm_ffn_g=norm_ffn_g,
        moe_w_router=moe_w_router, moe_router_bias=moe_router_bias,
        moe_w_gate=moe_w_gate, moe_w_up=moe_w_up, moe_w_down=moe_w_down,
        shared=[(sh_w_gate[l].astype(BF16), sh_w_up[l].astype(BF16), sh_w_down[l].astype(BF16))
                for l in range(depth)],
        final_norm_g=final_norm_g)
    mod = _ada(jnp.concatenate([c_prompt, c_sample], axis=0), ada_w, ada_b)
    y_p, ret_p, gla_p = _trunk(x_prompt, mod[:, :b], 0, None, None, wts, expert_tile=512)
    y_s, ret_s, gla_s = _trunk(x_sample, mod[:, b:], PAST_LEN, state_ret, state_gla, wts,
                               expert_tile=128)
    return (y_p, y_s, ret_p, gla_p, ret_s, gla_s)
```

```python
import functools

import jax
import jax.numpy as jnp
from jax import lax
from jax.experimental import pallas as pl
from jax.experimental.pallas import tpu as pltpu

F32, BF16, I32 = jnp.float32, jnp.bfloat16, jnp.int32

EPS = 1e-6
ROPE_BASE = 10000.0
PAST_LEN = 16384
RET_HEADS = 4
GLA_HEADS = 4
GLA_TAU = 16.0
N_GROUPS = 8
TOPK_GROUPS = 4
TOP_K = 8
ROUTED_SCALE = 2.5

LANES = 128
BF16_SUBLANES = 16
VMEM_LIMIT_BYTES = 56 * 1024 * 1024

ROW_TILE = 256
RET_CHUNK = 256
GLA_CHUNK = 128
GLA_KEY_BLOCK = 32
STEP_TOKENS = 8
RUN_ALIGN = BF16_SUBLANES
EXPERT_TILE = 512
EXPERT_SUB_TILE = 512
SORT_BLOCK = 512
EXPERT_RING = 3
ONEHOT_BLOCK = 256


def _cparams(*sem):
    return pltpu.CompilerParams(dimension_semantics=sem, vmem_limit_bytes=VMEM_LIMIT_BYTES)


def _sigmoid(x):
    return 1.0 / (1.0 + jnp.exp(-x))


def _silu(x):
    return x * _sigmoid(x)


def _modulate(x, g, shift, scale):
    y = x * lax.rsqrt(jnp.mean(x * x, axis=-1, keepdims=True) + EPS) * g
    return y * (1.0 + scale) + shift


def _dot(a, b):
    return jnp.dot(a, b, preferred_element_type=F32)


def _dot_nt(a, b):
    return lax.dot_general(a, b, (((1,), (1,)), ((), ())), preferred_element_type=F32)


def _dot_tn(a, b):
    return lax.dot_general(a, b, (((0,), (0,)), ((), ())), preferred_element_type=F32)


def _resident(shape):
    zeros = (0,) * len(shape)
    return pl.BlockSpec(shape, lambda *_: zeros, pipeline_mode=pl.Buffered(1))


class _Rows:
    def __init__(self, n_rows, seq_len, tile):
        self.n, self.tile = n_rows, tile
        self.per_row = seq_len == 1
        self.tiles_per_seq = max(seq_len // tile, 1)
        assert n_rows % tile == 0 and (self.per_row or seq_len % tile == 0)
        self.grid = n_rows // tile

    def rows(self, width, col=0):
        return pl.BlockSpec((self.tile, width), lambda i, *_: (i, col))

    def mod(self, d, col):
        if self.per_row:
            return pl.BlockSpec((self.tile, d), lambda i, *_: (i, col))
        tps = self.tiles_per_seq
        return pl.BlockSpec((None, 1, d), lambda i, *_: (i // tps, 0, col))


def _ada_kernel(c_ref, w_ref, b_ref, o_ref):
    s = _silu(c_ref[...]).astype(BF16)
    o_ref[0] = _dot(s, w_ref[0].astype(BF16)) + b_ref[0]


def _ada(c_all, ada_w, ada_b):
    depth, d, d6 = ada_w.shape
    n = c_all.shape[0]
    tn = d6 // 4
    return pl.pallas_call(
        _ada_kernel,
        grid=(depth, d6 // tn),
        in_specs=[pl.BlockSpec((n, d), lambda l, j: (0, 0)),
                  pl.BlockSpec((1, d, tn), lambda l, j: (l, 0, j)),
                  pl.BlockSpec((1, 1, tn), lambda l, j: (l, 0, j))],
        out_specs=pl.BlockSpec((1, n, tn), lambda l, j: (l, 0, j)),
        out_shape=jax.ShapeDtypeStruct((depth, n, d6), F32),
        compiler_params=_cparams("parallel", "parallel"),
        name="ada_mod",
    )(c_all, ada_w, ada_b.reshape(depth, 1, d6))


def _ret_proj_kernel(x_ref, sh_ref, sc_ref, g_ref, w_ref, cos_ref, sin_ref,
                     q_ref, k_ref, v_ref, gate_ref, *, n_heads, dk, dv):
    h = _modulate(x_ref[...], g_ref[...], sh_ref[...], sc_ref[...]).astype(BF16)
    cos, sin = cos_ref[...], sin_ref[...]
    half, nqk, nv = dk // 2, n_heads * dk, n_heads * dv
    for hd in range(n_heads):
        for dst, base, scale in ((q_ref, 0, None), (k_ref, nqk, dk ** -0.5)):
            p = _dot(h, w_ref[:, base + hd * dk:base + (hd + 1) * dk])
            x1, x2 = p[:, :half], p[:, half:]
            r1, r2 = x1 * cos - x2 * sin, x1 * sin + x2 * cos
            if scale is not None:
                r1, r2 = r1 * scale, r2 * scale
            dst[:, hd * dk:hd * dk + half] = r1.astype(BF16)
            dst[:, hd * dk + half:(hd + 1) * dk] = r2.astype(BF16)
    for hd in range(n_heads):
        v_ref[:, hd * dv:(hd + 1) * dv] = _dot(
            h, w_ref[:, 2 * nqk + hd * dv:2 * nqk + (hd + 1) * dv]).astype(BF16)
        gate_ref[:, hd * dv:(hd + 1) * dv] = _dot(
            h, w_ref[:, 2 * nqk + nv + hd * dv:2 * nqk + nv + (hd + 1) * dv])


def _ret_proj(x, mod, rows, norm_g, w_in_bf16, cos, sin, n_heads, dk, dv):
    n, d = x.shape
    nqk, nv = n_heads * dk, n_heads * dv
    half = dk // 2
    tps = rows.tiles_per_seq
    trig = (pl.BlockSpec((rows.tile, half), lambda i: (0, 0)) if rows.per_row
            else pl.BlockSpec((rows.tile, half), lambda i: (i % tps, 0)))
    return pl.pallas_call(
        functools.partial(_ret_proj_kernel, n_heads=n_heads, dk=dk, dv=dv),
        grid=(rows.grid,),
        in_specs=[rows.rows(d), rows.mod(d, 0), rows.mod(d, 1), _resident((1, d)),
                  _resident(w_in_bf16.shape), trig, trig],
        out_specs=[rows.rows(nqk), rows.rows(nqk), rows.rows(nv), rows.rows(nv)],
        out_shape=[jax.ShapeDtypeStruct((n, nqk), BF16), jax.ShapeDtypeStruct((n, nqk), BF16),
                   jax.ShapeDtypeStruct((n, nv), BF16), jax.ShapeDtypeStruct((n, nv), F32)],
        compiler_params=_cparams("parallel"),
        name="ret_proj",
    )(x, mod, mod, norm_g.reshape(1, d), w_in_bf16, cos, sin)


def _ret_core_kernel(lg_ref, q_ref, k_ref, v_ref, o_ref, s_out_ref, s_scr,
                     *, tc, n_chunks, n_heads, dk, dv):
    c = pl.program_id(1)

    @pl.when(c == 0)
    def _():
        s_scr[...] = jnp.zeros_like(s_scr)

    ii = lax.broadcasted_iota(I32, (tc, tc), 0)
    jj = lax.broadcasted_iota(I32, (tc, tc), 1)
    causal = ii >= jj
    lag = (ii - jj).astype(F32)
    row = lax.broadcasted_iota(I32, (tc, 1), 0).astype(F32)
    for hd in range(n_heads):
        lg = lg_ref[hd]
        q, k = q_ref[:, hd * dk:(hd + 1) * dk], k_ref[:, hd * dk:(hd + 1) * dk]
        v = v_ref[:, hd * dv:(hd + 1) * dv]
        dec = jnp.where(causal, jnp.exp(lag * lg), 0.0)
        att = (_dot_nt(q, k) * dec).astype(BF16)
        s_old = s_scr[hd]
        o_ref[:, hd * dv:(hd + 1) * dv] = (
            _dot(att, v) + jnp.exp((row + 1.0) * lg) * _dot(q, s_old.astype(BF16)))
        kd = (k.astype(F32) * jnp.exp((tc - 1.0 - row) * lg)).astype(BF16)
        s_scr[hd] = jnp.exp(jnp.full((1, 1), float(tc), F32) * lg) * s_old + _dot_tn(kd, v)

    @pl.when(c == n_chunks - 1)
    def _():
        s_out_ref[0] = s_scr[...]


def _ret_core(q, k, v, log_gamma, batch, seq, n_heads, dk, dv):
    tc = min(RET_CHUNK, seq)
    nc = seq // tc
    assert seq % tc == 0
    n = batch * seq
    qk = pl.BlockSpec((tc, n_heads * dk), lambda b, c: (b * nc + c, 0))
    vv = pl.BlockSpec((tc, n_heads * dv), lambda b, c: (b * nc + c, 0))
    return pl.pallas_call(
        functools.partial(_ret_core_kernel, tc=tc, n_chunks=nc, n_heads=n_heads, dk=dk, dv=dv),
        grid=(batch, nc),
        in_specs=[pl.BlockSpec(memory_space=pltpu.SMEM), qk, qk, vv],
        out_specs=[vv, pl.BlockSpec((1, n_heads, dk, dv), lambda b, c: (b, 0, 0, 0))],
        out_shape=[jax.ShapeDtypeStruct((n, n_heads * dv), F32),
                   jax.ShapeDtypeStruct((batch, n_heads, dk, dv), F32)],
        scratch_shapes=[pltpu.VMEM((n_heads, dk, dv), F32)],
        compiler_params=_cparams("parallel", "arbitrary"),
        name="ret_core",
    )(log_gamma, q, k, v)


def _step_kernel(s_ref, q_ref, k_ref, a_ref, v_ref, o_ref, s_out_ref, *, tb):
    for j in range(tb):
        s_new = (a_ref[0, 0, :, j:j + 1] * s_ref[j, 0]
                 + k_ref[0, 0, :, j:j + 1] * v_ref[j:j + 1, :])
        s_out_ref[j, 0] = s_new
        o_ref[j:j + 1, :] = jnp.sum(q_ref[0, 0, :, j:j + 1] * s_new, axis=0, keepdims=True)


def _columns(x, n_heads, dk, tb):
    n = x.shape[0]
    return x.astype(F32).reshape(n // tb, tb, n_heads, dk).transpose(2, 0, 3, 1)


def _step(states, layer, q, k, a, v, n_heads, dk, dv):
    n = states.shape[1]
    tb = STEP_TOKENS
    assert n % tb == 0
    col = pl.BlockSpec((1, 1, dk, tb), lambda i, h: (h, i, 0, 0))
    st_in = pl.BlockSpec((None, tb, 1, dk, dv), lambda i, h: (layer, i, h, 0, 0))
    st_out = pl.BlockSpec((tb, 1, dk, dv), lambda i, h: (i, h, 0, 0))
    row = pl.BlockSpec((tb, dv), lambda i, h: (i, h))
    return pl.pallas_call(
        functools.partial(_step_kernel, tb=tb),
        grid=(n // tb, n_heads),
        in_specs=[st_in, col, col, col, row],
        out_specs=[row, st_out],
        out_shape=[jax.ShapeDtypeStruct((n, n_heads * dv), F32),
                   jax.ShapeDtypeStruct(states.shape[1:], states.dtype)],
        compiler_params=_cparams("parallel", "parallel"),
        name="state_step",
    )(states, _columns(q, n_heads, dk, tb), _columns(k, n_heads, dk, tb),
      _columns(a, n_heads, dk, tb), v.astype(F32))


def _gla_proj_kernel(x_ref, sh_ref, sc_ref, g_ref, w_ref, a1_ref, a2_ref, ba_ref,
                     q_ref, k_ref, v_ref, r_ref, la_ref, *, nqk, nv):
    h = _modulate(x_ref[...], g_ref[...], sh_ref[...], sc_ref[...]).astype(BF16)
    q_ref[...] = _dot(h, w_ref[:, :nqk]) * ((nqk // GLA_HEADS) ** -0.5)
    k_ref[...] = _dot(h, w_ref[:, nqk:2 * nqk])
    v_ref[...] = _dot(h, w_ref[:, 2 * nqk:2 * nqk + nv]).astype(BF16)
    r_ref[...] = _dot(h, w_ref[:, 2 * nqk + nv:])
    z = _dot(_dot(h, a1_ref[...]).astype(BF16), a2_ref[...]) + ba_ref[...]
    la_ref[...] = (jnp.minimum(z, 0.0) - jnp.log1p(jnp.exp(-jnp.abs(z)))) / GLA_TAU


def _gla_proj(x, mod, rows, norm_g, w_in_bf16, a1_bf16, a2_bf16, b_a, nqk, nv):
    n, d = x.shape
    return pl.pallas_call(
        functools.partial(_gla_proj_kernel, nqk=nqk, nv=nv),
        grid=(rows.grid,),
        in_specs=[rows.rows(d), rows.mod(d, 0), rows.mod(d, 1), _resident((1, d)),
                  _resident(w_in_bf16.shape), _resident(a1_bf16.shape),
                  _resident(a2_bf16.shape), _resident((1, nqk))],
        out_specs=[rows.rows(nqk), rows.rows(nqk), rows.rows(nv), rows.rows(nv), rows.rows(nqk)],
        out_shape=[jax.ShapeDtypeStruct((n, nqk), F32), jax.ShapeDtypeStruct((n, nqk), F32),
                   jax.ShapeDtypeStruct((n, nv), BF16), jax.ShapeDtypeStruct((n, nv), F32),
                   jax.ShapeDtypeStruct((n, nqk), F32)],
        compiler_params=_cparams("parallel"),
        name="gla_proj",
    )(x, mod, mod, norm_g.reshape(1, d), w_in_bf16, a1_bf16, a2_bf16, b_a.reshape(1, nqk))


def _gla_core_kernel(q_ref, k_ref, v_ref, la_ref, o_ref, s_out_ref, s_scr,
                     *, tc, n_chunks, n_heads, dk, dv):
    c = pl.program_id(1)

    @pl.when(c == 0)
    def _():
        s_scr[...] = jnp.zeros_like(s_scr)

    ii = lax.broadcasted_iota(I32, (tc, tc), 0)
    jj = lax.broadcasted_iota(I32, (tc, tc), 1)
    la = la_ref[...]
    la_hi = la.astype(BF16)
    la_mid = (la - la_hi.astype(F32)).astype(BF16)
    la_lo = (la - la_hi.astype(F32) - la_mid.astype(F32)).astype(BF16)
    tri = jnp.where(ii >= jj, 1.0, 0.0).astype(BF16)
    b_all = _dot(tri, la_hi) + _dot(tri, la_mid) + _dot(tri, la_lo)
    ones = jnp.ones((tc, LANES), BF16)
    tot_col_all = (_dot_tn(la_hi, ones) + _dot_tn(la_mid, ones) + _dot_tn(la_lo, ones))[:, 0:1]
    qi = lax.broadcasted_iota(I32, (tc, GLA_KEY_BLOCK), 0)
    kj = lax.broadcasted_iota(I32, (tc, GLA_KEY_BLOCK), 1)
    for hd in range(n_heads):
        cols = slice(hd * dk, (hd + 1) * dk)
        b = b_all[:, cols]
        q, k, v = q_ref[:, cols], k_ref[:, cols], v_ref[:, hd * dv:(hd + 1) * dv]
        s_old = s_scr[hd]
        o = _dot((q * jnp.exp(b)).astype(BF16), s_old.astype(BF16))
        for s in range(tc // GLA_KEY_BLOCK):
            lo = s * GLA_KEY_BLOCK
            blk = slice(lo, lo + GLA_KEY_BLOCK)
            mid = lo + GLA_KEY_BLOCK // 2 - 1
            ref = b[mid:mid + 1, :]
            rel = b - ref
            if s:
                rel = jnp.where(qi[:, 0:1] < lo, 0.0, rel)
            qs = (q * jnp.exp(rel)).astype(BF16)
            ks = (k[blk, :] * jnp.exp(ref - b[blk, :])).astype(BF16)
            att = jnp.where(qi >= kj + lo, _dot_nt(qs, ks), 0.0).astype(BF16)
            o = o + _dot(att, v[blk, :])
        o_ref[:, hd * dv:(hd + 1) * dv] = o
        kd = (k * jnp.exp(b[tc - 1:tc, :] - b)).astype(BF16)
        s_scr[hd] = jnp.exp(tot_col_all[cols, :]) * s_old + _dot_tn(kd, v)

    @pl.when(c == n_chunks - 1)
    def _():
        s_out_ref[0] = s_scr[...]


def _gla_core(q, k, v, la, batch, seq, n_heads, dk, dv):
    tc = min(GLA_CHUNK, seq)
    nc = seq // tc
    assert seq % tc == 0
    n = batch * seq
    qk = pl.BlockSpec((tc, n_heads * dk), lambda b, c: (b * nc + c, 0))
    vv = pl.BlockSpec((tc, n_heads * dv), lambda b, c: (b * nc + c, 0))
    return pl.pallas_call(
        functools.partial(_gla_core_kernel, tc=tc, n_chunks=nc, n_heads=n_heads, dk=dk, dv=dv),
        grid=(batch, nc),
        in_specs=[qk, qk, vv, qk],
        out_specs=[vv, pl.BlockSpec((1, n_heads, dk, dv), lambda b, c: (b, 0, 0, 0))],
        out_shape=[jax.ShapeDtypeStruct((n, n_heads * dv), F32),
                   jax.ShapeDtypeStruct((batch, n_heads, dk, dv), F32)],
        scratch_shapes=[pltpu.VMEM((n_heads, dk, dv), F32)],
        compiler_params=_cparams("parallel", "arbitrary"),
        name="gla_core",
    )(q, k, v, la)


def _post_kernel(o_ref, gate_ref, x_ref, g1_ref, ng_ref, w_ref, out_ref, *, n_heads, dv, center):
    acc = None
    for hd in range(n_heads):
        sl = slice(hd * dv, (hd + 1) * dv)
        o = o_ref[:, sl]
        if center:
            o = o - jnp.mean(o, axis=-1, keepdims=True)
        y = o * lax.rsqrt(jnp.mean(o * o, axis=-1, keepdims=True) + EPS) * ng_ref[:, sl]
        y = (y * _silu(gate_ref[:, sl])).astype(BF16)
        part = _dot(y, w_ref[sl, :])
        acc = part if acc is None else acc + part
    out_ref[...] = x_ref[...] + g1_ref[...] * acc


def _post(o, gate, x, mod, rows, norm_g, w_out_bf16, n_heads, dv, center):
    n, d = x.shape
    nv = n_heads * dv
    return pl.pallas_call(
        functools.partial(_post_kernel, n_heads=n_heads, dv=dv, center=center),
        grid=(rows.grid,),
        in_specs=[rows.rows(nv), rows.rows(nv), rows.rows(d), rows.mod(d, 2),
                  _resident((1, nv)), _resident(w_out_bf16.shape)],
        out_specs=rows.rows(d),
        out_shape=jax.ShapeDtypeStruct((n, d), F32),
        compiler_params=_cparams("parallel"),
        name="mixer_post",
    )(o, gate, x, mod, norm_g.reshape(1, nv), w_out_bf16)


def _first_max(vals, idx_iota, n):
    m = jnp.max(vals, axis=0, keepdims=True)
    i = jnp.min(jnp.where(vals == m, idx_iota, n), axis=0, keepdims=True)
    return m, i


def _router_kernel(x_ref, sh_ref, sc_ref, g_ref, wr_ref, b_ref,
                   h2_ref, slot_ref, wt_ref, cnt_ref):
    h2 = _modulate(x_ref[...], g_ref[...], sh_ref[...], sc_ref[...])
    h2_ref[...] = h2.astype(BF16)
    h_hi = h2.astype(BF16)
    h_lo = (h2 - h_hi.astype(F32)).astype(BF16)
    wr = wr_ref[...]
    w_hi = wr.astype(BF16)
    w_lo = (wr - w_hi.astype(F32)).astype(BF16)
    logits = _dot_nt(w_hi, h_hi) + _dot_nt(w_hi, h_lo) + _dot_nt(w_lo, h_hi)
    scores = _sigmoid(logits)
    biased = scores + b_ref[...]
    n_exp, win = biased.shape
    gsz = n_exp // N_GROUPS
    neg = -jnp.inf

    sub_g = lax.broadcasted_iota(I32, (gsz, win), 0)
    group_rows = []
    for g in range(N_GROUPS):
        blk = biased[g * gsz:(g + 1) * gsz, :]
        m1, i1 = _first_max(blk, sub_g, gsz)
        m2 = jnp.max(jnp.where(sub_g == i1, neg, blk), axis=0, keepdims=True)
        group_rows.append(m1 + m2)
    cur = jnp.concatenate(group_rows, axis=0)
    sub_n = lax.broadcasted_iota(I32, (N_GROUPS, win), 0)
    gmask = jnp.zeros((N_GROUPS, win), jnp.bool_)
    for _ in range(TOPK_GROUPS):
        _, i = _first_max(cur, sub_n, N_GROUPS)
        sel = sub_n == i
        gmask = gmask | sel
        cur = jnp.where(sel, neg, cur)
    emask = jnp.concatenate(
        [jnp.broadcast_to(gmask[g:g + 1, :], (gsz, win)) for g in range(N_GROUPS)], axis=0)
    masked = jnp.where(emask, biased, neg)

    sub_e = lax.broadcasted_iota(I32, (n_exp, win), 0)
    sels, picked = [], []
    for _ in range(TOP_K):
        _, i = _first_max(masked, sub_e, n_exp)
        sel = sub_e == i
        sels.append(sel)
        picked.append(jnp.sum(jnp.where(sel, scores, 0.0), axis=0, keepdims=True))
        masked = jnp.where(sel, neg, masked)
    total = picked[0]
    for p in picked[1:]:
        total = total + p
    wt_ref[...] = jnp.concatenate([p / total * ROUTED_SCALE for p in picked], axis=0)

    chosen = sels[0]
    for s in sels[1:]:
        chosen = chosen | s
    sel_b = jnp.where(chosen, 1.0, 0.0).astype(BF16)
    r = lax.broadcasted_iota(I32, (win, win), 0)
    c = lax.broadcasted_iota(I32, (win, win), 1)
    before = jnp.where(r < c, 1.0, 0.0).astype(BF16)
    rank = _dot(sel_b, before)
    count = _dot(sel_b, jnp.ones((win, win), BF16))
    run_len = jnp.floor((count + (RUN_ALIGN - 1.0)) * (1.0 / RUN_ALIGN)) * RUN_ALIGN
    er = lax.broadcasted_iota(I32, (n_exp, n_exp), 0)
    ec = lax.broadcasted_iota(I32, (n_exp, n_exp), 1)
    lower = jnp.where(ec < er, 1.0, 0.0).astype(BF16)
    slot_all = _dot(lower, run_len.astype(BF16)) + rank
    slot_ref[...] = jnp.concatenate(
        [jnp.sum(jnp.where(s, slot_all, 0.0), axis=0, keepdims=True) for s in sels],
        axis=0).astype(I32)
    cnt_ref[0] = _dot_nt(jnp.ones((8, win), BF16), sel_b)


def _router(x, mod, rows, norm_g, w_router, bias):
    n, d = x.shape
    n_exp = w_router.shape[1]
    win = rows.tile
    lanes = pl.BlockSpec((TOP_K, win), lambda i: (0, i))
    return pl.pallas_call(
        _router_kernel,
        grid=(rows.grid,),
        in_specs=[rows.rows(d), rows.mod(d, 3), rows.mod(d, 4), _resident((1, d)),
                  _resident((n_exp, d)), _resident((n_exp, 1))],
        out_specs=[rows.rows(d), lanes, lanes, pl.BlockSpec((1, 8, n_exp), lambda i: (i, 0, 0))],
        out_shape=[jax.ShapeDtypeStruct((n, d), BF16), jax.ShapeDtypeStruct((TOP_K, n), I32),
                   jax.ShapeDtypeStruct((TOP_K, n), F32),
                   jax.ShapeDtypeStruct((rows.grid, 8, n_exp), F32)],
        compiler_params=_cparams("parallel"),
        name="moe_router",
    )(x, mod, mod, norm_g.reshape(1, d), w_router.T, bias.reshape(n_exp, 1))


def _sorted_rows(win, n_exp):
    return -(-(win * TOP_K + n_exp * (RUN_ALIGN - 1)) // SORT_BLOCK) * SORT_BLOCK


def _copy_list(count, src0, dst0, length, step):
    n_exp = count.shape[1]
    end = jnp.cumsum(count, axis=1)
    idx = jnp.arange(length, dtype=I32)
    owner = jnp.sum((end[:, None, :] <= idx[None, :, None]).astype(I32), axis=2)
    own = (owner[:, :, None] == jnp.arange(n_exp, dtype=I32)[None, None, :]).astype(I32)
    pick = lambda a: jnp.sum(own * a[:, None, :], axis=2)
    off = (idx[None, :] - pick(end - count)) * step
    return (pick(src0) + off).reshape(-1), (pick(dst0) + off).reshape(-1), end[:, -1]


def _plan(cnt, tile, sorted_rows):
    c = cnt[:, 0, :].astype(I32)
    run = (c + RUN_ALIGN - 1) // RUN_ALIGN * RUN_ALIGN
    per_exp = jnp.sum(run, axis=0)
    padded = (per_exp + tile - 1) // tile * tile
    pend = jnp.cumsum(padded)
    pstart = pend - padded
    dest = pstart[None, :] + jnp.cumsum(run, axis=0) - run
    local = jnp.cumsum(run, axis=1) - run
    n_double = run // (2 * RUN_ALIGN)
    covered = n_double * (2 * RUN_ALIGN)
    doubles = _copy_list(n_double, local, dest, sorted_rows // (2 * RUN_ALIGN), 2 * RUN_ALIGN)
    singles = _copy_list((run - covered) // RUN_ALIGN, local + covered, dest + covered,
                         c.shape[1], RUN_ALIGN)
    return dict(
        copies=doubles + singles, list_rows=sorted_rows,
        pad_start=pstart + per_exp, pad_chunks=(padded - per_exp) // RUN_ALIGN,
        pend=pend, used_tiles=(pend[-1] // tile).reshape(1))


def _capacity(n, n_win, n_exp, tile):
    worst = n * TOP_K + n_win * n_exp * (RUN_ALIGN - 1) + n_exp * (tile - RUN_ALIGN)
    return -(-worst // tile) * tile


def _window_copies(copy_refs, window, sorted_rows, n_exp, make_copy, wait):
    d_src, d_dst, d_n, s_src, s_dst, s_n = copy_refs
    for src_ref, dst_ref, n_ref, length, n_rows in (
            (d_src, d_dst, d_n, sorted_rows // (2 * RUN_ALIGN), 2 * RUN_ALIGN),
            (s_src, s_dst, s_n, n_exp, RUN_ALIGN)):
        def per_copy(i, carry, src_ref=src_ref, dst_ref=dst_ref, length=length, n_rows=n_rows):
            cp = make_copy(pl.multiple_of(src_ref[window * length + i], RUN_ALIGN),
                           pl.multiple_of(dst_ref[window * length + i], RUN_ALIGN), n_rows)
            if wait:
                cp.wait()
            else:
                cp.start()
            return carry

        lax.fori_loop(0, n_ref[window], per_copy, 0)


def _dispatch_kernel(d_src, d_dst, d_n, s_src, s_dst, s_n, pad_start_ref, pad_chunks_ref, used_ref,
                     h2_ref, slot_ref, *refs, n_exp, n_win, tile, window_base, list_rows, fill):
    xbuf_ref, xs_scr, zero_scr, sems, tail_sem = refs[-5:]
    w = pl.program_id(0)
    cur = w % 2
    n_tail = xbuf_ref.shape[0] // tile - used_ref[0]

    def tail_copy(j):
        dst = pl.multiple_of((used_ref[0] + j) * tile, tile)
        return pltpu.make_async_copy(zero_scr, xbuf_ref.at[pl.ds(dst, tile)], tail_sem)

    if fill:
        @pl.when(w == 0)
        def _():
            zero_scr[...] = jnp.zeros_like(zero_scr)
            lax.fori_loop(0, n_tail, lambda j, c: (tail_copy(j).start(), c)[1], 0)

    slot = slot_ref[...]
    n_rows, win = xs_scr.shape[1], slot.shape[1]
    h2 = h2_ref[...]
    local = lax.broadcasted_iota(I32, (ONEHOT_BLOCK, win), 0).astype(F32).astype(BF16)
    slot_block = slot // ONEHOT_BLOCK
    slot_local = (slot % ONEHOT_BLOCK).astype(F32)
    one, zero = jnp.ones((), BF16), jnp.zeros((), BF16)
    for blk in range(n_rows // ONEHOT_BLOCK):
        key = jnp.where(slot_block == blk, slot_local, -1.0).astype(BF16)
        onehot = jnp.where(local == key[0:1, :], one, zero)
        for k in range(1, TOP_K):
            onehot = jnp.where(local == key[k:k + 1, :], one, onehot)
        lo = blk * ONEHOT_BLOCK
        xs_scr[cur, lo:lo + ONEHOT_BLOCK, :] = _dot(onehot, h2).astype(BF16)

    def copies(window, buf, wait):
        def copy(src, dst, n):
            return pltpu.make_async_copy(xs_scr.at[buf, pl.ds(src, n)],
                                         xbuf_ref.at[pl.ds(dst, n)], sems.at[buf])
        _window_copies((d_src, d_dst, d_n, s_src, s_dst, s_n), window_base + window, list_rows,
                       n_exp, copy, wait)

    copies(w, cur, wait=False)

    @pl.when(w > 0)
    def _():
        copies(w - 1, 1 - cur, wait=True)

    @pl.when(w == n_win - 1)
    def _():
        copies(w, cur, wait=True)

    def fill_pads_and_finish():
        sem = sems.at[cur]

        def pad_copy(e, j):
            dst = pl.multiple_of(pad_start_ref[e] + j * RUN_ALIGN, RUN_ALIGN)
            return pltpu.make_async_copy(zero_scr.at[pl.ds(0, RUN_ALIGN)],
                                         xbuf_ref.at[pl.ds(dst, RUN_ALIGN)], sem)

        def pads(wait):
            def per_expert(e, carry):
                def per_chunk(j, carry2):
                    if wait:
                        pad_copy(e, j).wait()
                    else:
                        pad_copy(e, j).start()
                    return carry2
                return lax.fori_loop(0, pad_chunks_ref[e], per_chunk, carry)
            lax.fori_loop(0, n_exp, per_expert, 0)

        pads(False)
        pads(True)
        lax.fori_loop(0, n_tail, lambda j, c: (tail_copy(j).wait(), c)[1], 0)

    if fill:
        pl.when(w == n_win - 1)(fill_pads_and_finish)


def _dispatch(h2, slot_t, plan, rows, n_exp, cap, tile, window_base, xbuf=None):
    n, d = h2.shape
    win = rows.tile
    n_prefetch = 9
    in_specs = [rows.rows(d), pl.BlockSpec((TOP_K, win), lambda i, *_: (0, i))]
    operands = [*plan["copies"], plan["pad_start"], plan["pad_chunks"], plan["used_tiles"],
                h2, slot_t]
    aliases = {}
    if xbuf is not None:
        in_specs.append(pl.BlockSpec(memory_space=pl.ANY))
        operands.append(xbuf)
        aliases = {len(operands) - 1: 0}
    return pl.pallas_call(
        functools.partial(_dispatch_kernel, n_exp=n_exp, n_win=rows.grid, tile=tile,
                          window_base=window_base, list_rows=plan["list_rows"],
                          fill=xbuf is None),
        grid_spec=pltpu.PrefetchScalarGridSpec(
            num_scalar_prefetch=n_prefetch,
            grid=(rows.grid,),
            in_specs=in_specs,
            out_specs=pl.BlockSpec(memory_space=pl.ANY),
            scratch_shapes=[pltpu.VMEM((2, _sorted_rows(win, n_exp), d), BF16),
                            pltpu.VMEM((tile, d), BF16),
                            pltpu.SemaphoreType.DMA((2,)), pltpu.SemaphoreType.DMA(())]),
        out_shape=jax.ShapeDtypeStruct((cap, d), BF16),
        input_output_aliases=aliases,
        compiler_params=_cparams("arbitrary"),
        name="moe_dispatch",
    )(*operands)


def _experts_kernel(texp_ref, used_ref, xbuf_ref, wg_ref, wu_ref, wd_ref, y_ref,
                    x_ring, sems, wgu_scr, wd_scr, *, sub_tiles):
    i = pl.program_id(0)
    used = used_ref[0]
    live = i < used
    de = wd_scr.shape[0]
    depth, tile = x_ring.shape[0], x_ring.shape[1]

    def fetch(j):
        slot = j % depth
        return pltpu.make_async_copy(xbuf_ref.at[pl.ds(pl.multiple_of(j * tile, tile), tile)],
                                     x_ring.at[slot], sems.at[slot])

    @pl.when(i == 0)
    def _():
        for j in range(depth - 1):
            @pl.when(j < used)
            def _():
                fetch(j).start()

    @pl.when(i + (depth - 1) < used)
    def _():
        fetch(i + (depth - 1)).start()

    @pl.when(jnp.logical_or(i == 0, texp_ref[i] != texp_ref[jnp.maximum(i - 1, 0)]))
    def _():
        wgu_scr[:, :de] = wg_ref[...].astype(BF16)
        wgu_scr[:, de:] = wu_ref[...].astype(BF16)
        wd_scr[...] = wd_ref[...].astype(BF16)

    @pl.when(live)
    def _():
        fetch(i).wait()
        x_ref = x_ring.at[i % depth]
        sub = tile // sub_tiles
        for s in range(sub_tiles):
            rows = slice(s * sub, (s + 1) * sub)
            ab = _dot(x_ref[rows, :], wgu_scr[...])
            mid = (_silu(ab[:, :de]) * ab[:, de:]).astype(BF16)
            y_ref[rows, :] = _dot(mid, wd_scr[...]).astype(BF16)

    @pl.when(jnp.logical_not(live))
    def _():
        y_ref[...] = jnp.zeros_like(y_ref)


def _experts(xbuf, plan, layer, w_gate, w_up, w_down, tile):
    cap, d = xbuf.shape
    _, n_exp, _, de = w_gate.shape
    n_tiles = cap // tile
    used = plan["used_tiles"]
    tmap = jnp.minimum(jnp.arange(n_tiles, dtype=I32), used[0] - 1)
    texp = jnp.sum((plan["pend"][None, :] <= (tmap * tile)[:, None]).astype(I32), axis=1)
    texp = jnp.minimum(texp, n_exp - 1)
    return pl.pallas_call(
        functools.partial(_experts_kernel, sub_tiles=max(tile // EXPERT_SUB_TILE, 1)),
        grid_spec=pltpu.PrefetchScalarGridSpec(
            num_scalar_prefetch=2,
            grid=(n_tiles,),
            in_specs=[pl.BlockSpec(memory_space=pl.ANY),
                      pl.BlockSpec((None, None, d, de), lambda i, te, u: (layer, te[i], 0, 0)),
                      pl.BlockSpec((None, None, d, de), lambda i, te, u: (layer, te[i], 0, 0)),
                      pl.BlockSpec((None, None, de, d), lambda i, te, u: (layer, te[i], 0, 0))],
            out_specs=pl.BlockSpec((tile, d), lambda i, te, u: (i, 0)),
            scratch_shapes=[pltpu.VMEM((EXPERT_RING, tile, d), BF16),
                            pltpu.SemaphoreType.DMA((EXPERT_RING,)),
                            pltpu.VMEM((d, 2 * de), BF16), pltpu.VMEM((de, d), BF16)]),
        out_shape=jax.ShapeDtypeStruct((cap, d), BF16),
        compiler_params=_cparams("arbitrary"),
        name="moe_experts",
    )(texp, used, xbuf, w_gate, w_up, w_down)


def _combine_kernel(d_src, d_dst, d_n, s_src, s_dst, s_n, ybuf_ref, slot_ref, wt_ref, h2_ref,
                    x_ref, g2_ref, sg_ref, su_ref, sd_ref, fin_ref, out_ref, ys_scr, sems,
                    *, n_exp, n_win, final_norm, window_base, list_rows):
    w = pl.program_id(0)
    cur = w % 2
    n_rows = ys_scr.shape[1]

    def copies(window, buf, wait):
        def copy(loc, dst, n):
            return pltpu.make_async_copy(ybuf_ref.at[pl.ds(dst, n)],
                                         ys_scr.at[buf, pl.ds(loc, n)], sems.at[buf])
        _window_copies((d_src, d_dst, d_n, s_src, s_dst, s_n), window_base + window, list_rows,
                       n_exp, copy, wait)

    @pl.when(w == 0)
    def _():
        ys_scr[...] = jnp.zeros_like(ys_scr)
        copies(w, cur, wait=False)

    @pl.when(w + 1 < n_win)
    def _():
        copies(w + 1, 1 - cur, wait=False)

    h2 = h2_ref[...]
    shared = _dot((_silu(_dot(h2, sg_ref[...])) * _dot(h2, su_ref[...])).astype(BF16), sd_ref[...])
    copies(w, cur, wait=True)

    slot, wt = slot_ref[...], wt_ref[...].astype(BF16)
    win = slot.shape[1]
    local = lax.broadcasted_iota(I32, (ONEHOT_BLOCK, win), 0).astype(F32).astype(BF16)
    slot_block = slot // ONEHOT_BLOCK
    slot_local = (slot % ONEHOT_BLOCK).astype(F32)
    routed = shared
    for blk in range(n_rows // ONEHOT_BLOCK):
        key = jnp.where(slot_block == blk, slot_local, -1.0).astype(BF16)
        weights = jnp.where(local == key[0:1, :], wt[0:1, :], jnp.zeros((), BF16))
        for k in range(1, TOP_K):
            weights = jnp.where(local == key[k:k + 1, :], wt[k:k + 1, :], weights)
        lo = blk * ONEHOT_BLOCK
        routed = routed + _dot_tn(weights, ys_scr[cur, lo:lo + ONEHOT_BLOCK, :])
    out = x_ref[...] + g2_ref[...] * routed
    if final_norm:
        out = out * lax.rsqrt(jnp.mean(out * out, axis=-1, keepdims=True) + EPS) * fin_ref[...]
    out_ref[...] = out


def _combine(ybuf, plan, slot, wt, h2, x, mod, rows, sh_bf16, final_g, n_exp, final_norm,
             window_base):
    n, d = x.shape
    win = rows.tile
    sg, su, sd = sh_bf16
    pairs = pl.BlockSpec((TOP_K, win), lambda i, *_: (0, i))
    return pl.pallas_call(
        functools.partial(_combine_kernel, n_exp=n_exp, n_win=rows.grid, final_norm=final_norm,
                          window_base=window_base, list_rows=plan["list_rows"]),
        grid_spec=pltpu.PrefetchScalarGridSpec(
            num_scalar_prefetch=6,
            grid=(rows.grid,),
            in_specs=[pl.BlockSpec(memory_space=pl.ANY), pairs, pairs, rows.rows(d), rows.rows(d),
                      rows.mod(d, 5), _resident(sg.shape), _resident(su.shape),
                      _resident(sd.shape), _resident((1, d))],
            out_specs=rows.rows(d),
            scratch_shapes=[pltpu.VMEM((2, _sorted_rows(win, n_exp), d), BF16),
                            pltpu.SemaphoreType.DMA((2,))]),
        out_shape=jax.ShapeDtypeStruct((n, d), F32),
        compiler_params=_cparams("arbitrary"),
        name="moe_combine",
    )(*plan["copies"], ybuf, slot, wt, h2, x, mod, sg, su, sd, final_g.reshape(1, d))


def _moe(groups, layer, norm_g, w_router, bias, w_gate, w_up, w_down, sh_bf16, final_g,
         final_norm, tile):
    n_exp = w_router.shape[1]
    routed = [_router(x, mod, rows, norm_g, w_router, bias) for x, mod, rows in groups]
    bases, total = [], 0
    for _, _, rows in groups:
        bases.append(total)
        total += rows.grid
    plan = _plan(jnp.concatenate([r[3] for r in routed], axis=0), tile,
                 max(_sorted_rows(rows.tile, n_exp) for _, _, rows in groups))
    cap = _capacity(sum(x.shape[0] for x, _, _ in groups), total, n_exp, tile)
    xbuf = None
    for (x, mod, rows), (h2, slot_t, _, _), base in zip(groups, routed, bases):
        xbuf = _dispatch(h2, slot_t, plan, rows, n_exp, cap, tile, base, xbuf)
    ybuf = _experts(xbuf, plan, layer, w_gate, w_up, w_down, tile)
    return [_combine(ybuf, plan, slot_t, wt_t, h2, x, mod, rows, sh_bf16, final_g, n_exp,
                     final_norm, base)
            for (x, mod, rows), (h2, slot_t, wt_t, _), base in zip(groups, routed, bases)]


def _rope_tables(pos, half):
    inv = ROPE_BASE ** (-jnp.arange(half, dtype=F32) / half)
    ang = pos.astype(F32)[:, None] * inv[None, :]
    return jnp.cos(ang), jnp.sin(ang)


class _Group:
    def __init__(self, x3, mod_all, pos0, s_ret, s_gla):
        self.batch, self.seq, d = x3.shape
        self.n = self.batch * self.seq
        self.x = x3.reshape(self.n, d)
        self.rows = _Rows(self.n, self.seq, min(ROW_TILE, self.n))
        self.decode = self.seq == 1
        self.mod_all, self.pos0, self.s_ret, self.s_gla = mod_all, pos0, s_ret, s_gla
        self.new_ret, self.new_gla = [], []

    def mod(self, layer):
        m = self.mod_all[layer]
        return m if self.decode else m.reshape(self.batch, 1, m.shape[-1])


def _mixer(g, layer, wts):
    d = g.x.shape[1]
    x, mod, rows = g.x, g.mod(layer), g.rows
    ret_dk = d // RET_HEADS
    ret_dv = 2 * ret_dk
    gla_dk = d // (2 * GLA_HEADS)
    gla_dv = d // GLA_HEADS
    j = layer // 2
    if layer % 2 == 0:
        pos = (jnp.full((rows.tile,), g.pos0, I32) if g.decode
               else g.pos0 + jnp.arange(g.seq, dtype=I32))
        cos, sin = _rope_tables(pos, ret_dk // 2)
        q, k, v, gate = _ret_proj(x, mod, rows, wts["norm_mix_g"][layer], wts["ret_w_in"][j],
                                  cos, sin, RET_HEADS, ret_dk, ret_dv)
        log_gamma = jnp.log1p(-jnp.power(2.0, -5.0 - jnp.arange(RET_HEADS, dtype=F32)))
        if g.decode:
            decay = jnp.broadcast_to(jnp.repeat(jnp.exp(log_gamma), ret_dk)[None, :],
                                     (g.n, RET_HEADS * ret_dk))
            o, s_new = _step(g.s_ret, j, q, k, decay, v, RET_HEADS, ret_dk, ret_dv)
        else:
            o, s_new = _ret_core(q, k, v, log_gamma, g.batch, g.seq, RET_HEADS, ret_dk, ret_dv)
        g.new_ret.append(s_new)
        return _post(o, gate, x, mod, rows, wts["ret_norm_g"][j], wts["ret_w_out"][j],
                     RET_HEADS, ret_dv, center=True)
    q, k, v, gate, la = _gla_proj(x, mod, rows, wts["norm_mix_g"][layer], wts["gla_w_in"][j],
                                  wts["gla_w_a1"][j], wts["gla_w_a2"][j], wts["gla_b_a"][j],
                                  GLA_HEADS * gla_dk, GLA_HEADS * gla_dv)
    if g.decode:
        o, s_new = _step(g.s_gla, j, q, k, jnp.exp(la), v, GLA_HEADS, gla_dk, gla_dv)
    else:
        o, s_new = _gla_core(q, k, v, la, g.batch, g.seq, GLA_HEADS, gla_dk, gla_dv)
    g.new_gla.append(s_new)
    return _post(o, gate, x, mod, rows, wts["gla_norm_g"][j], wts["gla_w_out"][j],
                 GLA_HEADS, gla_dv, center=False)


def _trunk(groups, wts):
    depth = wts["ada_w"].shape[0]
    for layer in range(depth):
        mixed = [(_mixer(g, layer, wts), g.mod(layer), g.rows) for g in groups]
        outs = _moe(mixed, layer, wts["norm_ffn_g"][layer], wts["moe_w_router"][layer],
                    wts["moe_router_bias"][layer], wts["moe_w_gate"], wts["moe_w_up"],
                    wts["moe_w_down"], wts["shared"][layer], wts["final_norm_g"],
                    final_norm=layer == depth - 1, tile=EXPERT_TILE)
        for g, x in zip(groups, outs):
            g.x = x
    return [(g.x.reshape(g.batch, g.seq, -1), jnp.stack(g.new_ret), jnp.stack(g.new_gla))
            for g in groups]


def kernel(x_prompt, x_sample, state_ret, state_gla, c_prompt, c_sample, ret_w_in, ret_norm_g, ret_w_out, gla_w_in, gla_w_a1, gla_w_a2, gla_b_a, gla_norm_g, gla_w_out, ada_w, ada_b, norm_mix_g, norm_ffn_g, moe_w_router, moe_router_bias, moe_w_gate, moe_w_up, moe_w_down, sh_w_gate, sh_w_up, sh_w_down, final_norm_g):
    b = x_prompt.shape[0]
    depth = ada_w.shape[0]
    rank = gla_w_a1.shape[-1]
    pad = LANES - rank
    wts = dict(
        ret_w_in=ret_w_in.astype(BF16), ret_norm_g=ret_norm_g, ret_w_out=ret_w_out.astype(BF16),
        gla_w_in=gla_w_in.astype(BF16),
        gla_w_a1=jnp.pad(gla_w_a1, ((0, 0), (0, 0), (0, pad))).astype(BF16),
        gla_w_a2=jnp.pad(gla_w_a2, ((0, 0), (0, pad), (0, 0))).astype(BF16),
        gla_b_a=gla_b_a, gla_norm_g=gla_norm_g, gla_w_out=gla_w_out.astype(BF16),
        ada_w=ada_w, norm_mix_g=norm_mix_g, norm_ffn_g=norm_ffn_g,
        moe_w_router=moe_w_router, moe_router_bias=moe_router_bias,
        moe_w_gate=moe_w_gate, moe_w_up=moe_w_up, moe_w_down=moe_w_down,
        shared=[(sh_w_gate[l].astype(BF16), sh_w_up[l].astype(BF16), sh_w_down[l].astype(BF16))
                for l in range(depth)],
        final_norm_g=final_norm_g)
    mod = _ada(jnp.concatenate([c_prompt, c_sample], axis=0), ada_w, ada_b)
    (y_p, ret_p, gla_p), (y_s, ret_s, gla_s) = _trunk(
        [_Group(x_prompt, mod[:, :b], 0, None, None),
         _Group(x_sample, mod[:, b:], PAST_LEN, state_ret, state_gla)], wts)
    return (y_p, y_s, ret_p, gla_p, ret_s, gla_s)
```

```python
import functools

import jax
import jax.numpy as jnp
from jax import lax
from jax.experimental import pallas as pl
from jax.experimental.pallas import tpu as pltpu

F32, BF16, I32 = jnp.float32, jnp.bfloat16, jnp.int32

EPS = 1e-6
ROPE_BASE = 10000.0
PAST_LEN = 16384
RET_HEADS = 4
GLA_HEADS = 4
GLA_TAU = 16.0
N_GROUPS = 8
TOPK_GROUPS = 4
TOP_K = 8
ROUTED_SCALE = 2.5

LANES = 128
BF16_SUBLANES = 16
VMEM_LIMIT_BYTES = 56 * 1024 * 1024

ROW_TILE = 256
RET_CHUNK = 256
GLA_CHUNK = 128
GLA_KEY_BLOCK = 32
STEP_TOKENS = 8
RUN_ALIGN = BF16_SUBLANES
EXPERT_TILE = 1024
EXPERT_SUB_TILE = 1024
SORT_BLOCK = 512
EXPERT_RING = 3
ONEHOT_BLOCK = 256
PAD_PIECE = 256


def _cparams(*sem):
    return pltpu.CompilerParams(dimension_semantics=sem, vmem_limit_bytes=VMEM_LIMIT_BYTES)


def _sigmoid(x):
    return 1.0 / (1.0 + jnp.exp(-x))


def _silu(x):
    return x * _sigmoid(x)


def _modulate(x, g, shift, scale):
    y = x * lax.rsqrt(jnp.mean(x * x, axis=-1, keepdims=True) + EPS) * g
    return y * (1.0 + scale) + shift


def _dot(a, b):
    return jnp.dot(a, b, preferred_element_type=F32)


def _dot_nt(a, b):
    return lax.dot_general(a, b, (((1,), (1,)), ((), ())), preferred_element_type=F32)


def _dot_tn(a, b):
    return lax.dot_general(a, b, (((0,), (0,)), ((), ())), preferred_element_type=F32)


def _resident(shape):
    zeros = (0,) * len(shape)
    return pl.BlockSpec(shape, lambda *_: zeros, pipeline_mode=pl.Buffered(1))


class _Rows:
    def __init__(self, n_rows, seq_len, tile):
        self.n, self.tile = n_rows, tile
        self.per_row = seq_len == 1
        self.tiles_per_seq = max(seq_len // tile, 1)
        assert n_rows % tile == 0 and (self.per_row or seq_len % tile == 0)
        self.grid = n_rows // tile

    def rows(self, width, col=0):
        return pl.BlockSpec((self.tile, width), lambda i, *_: (i, col))

    def mod(self, d, col):
        if self.per_row:
            return pl.BlockSpec((self.tile, d), lambda i, *_: (i, col))
        tps = self.tiles_per_seq
        return pl.BlockSpec((None, 1, d), lambda i, *_: (i // tps, 0, col))


def _ada_kernel(c_ref, w_ref, b_ref, o_ref):
    s = _silu(c_ref[...]).astype(BF16)
    o_ref[0] = _dot(s, w_ref[0].astype(BF16)) + b_ref[0]


def _ada(c_all, ada_w, ada_b):
    depth, d, d6 = ada_w.shape
    n = c_all.shape[0]
    tn = d6 // 4
    return pl.pallas_call(
        _ada_kernel,
        grid=(depth, d6 // tn),
        in_specs=[pl.BlockSpec((n, d), lambda l, j: (0, 0)),
                  pl.BlockSpec((1, d, tn), lambda l, j: (l, 0, j)),
                  pl.BlockSpec((1, 1, tn), lambda l, j: (l, 0, j))],
        out_specs=pl.BlockSpec((1, n, tn), lambda l, j: (l, 0, j)),
        out_shape=jax.ShapeDtypeStruct((depth, n, d6), F32),
        compiler_params=_cparams("parallel", "parallel"),
        name="ada_mod",
    )(c_all, ada_w, ada_b.reshape(depth, 1, d6))


def _ret_proj_kernel(x_ref, sh_ref, sc_ref, g_ref, w_ref, cos_ref, sin_ref,
                     q_ref, k_ref, v_ref, gate_ref, *, n_heads, dk, dv):
    h = _modulate(x_ref[...], g_ref[...], sh_ref[...], sc_ref[...]).astype(BF16)
    cos, sin = cos_ref[...], sin_ref[...]
    half, nqk, nv = dk // 2, n_heads * dk, n_heads * dv
    for hd in range(n_heads):
        for dst, base, scale in ((q_ref, 0, None), (k_ref, nqk, dk ** -0.5)):
            p = _dot(h, w_ref[:, base + hd * dk:base + (hd + 1) * dk])
            x1, x2 = p[:, :half], p[:, half:]
            r1, r2 = x1 * cos - x2 * sin, x1 * sin + x2 * cos
            if scale is not None:
                r1, r2 = r1 * scale, r2 * scale
            dst[:, hd * dk:hd * dk + half] = r1.astype(BF16)
            dst[:, hd * dk + half:(hd + 1) * dk] = r2.astype(BF16)
    for hd in range(n_heads):
        v_ref[:, hd * dv:(hd + 1) * dv] = _dot(
            h, w_ref[:, 2 * nqk + hd * dv:2 * nqk + (hd + 1) * dv]).astype(BF16)
        gate_ref[:, hd * dv:(hd + 1) * dv] = _dot(
            h, w_ref[:, 2 * nqk + nv + hd * dv:2 * nqk + nv + (hd + 1) * dv])


def _ret_proj(x, mod, rows, norm_g, w_in_bf16, cos, sin, n_heads, dk, dv):
    n, d = x.shape
    nqk, nv = n_heads * dk, n_heads * dv
    half = dk // 2
    tps = rows.tiles_per_seq
    trig = (pl.BlockSpec((rows.tile, half), lambda i: (0, 0)) if rows.per_row
            else pl.BlockSpec((rows.tile, half), lambda i: (i % tps, 0)))
    return pl.pallas_call(
        functools.partial(_ret_proj_kernel, n_heads=n_heads, dk=dk, dv=dv),
        grid=(rows.grid,),
        in_specs=[rows.rows(d), rows.mod(d, 0), rows.mod(d, 1), _resident((1, d)),
                  _resident(w_in_bf16.shape), trig, trig],
        out_specs=[rows.rows(nqk), rows.rows(nqk), rows.rows(nv), rows.rows(nv)],
        out_shape=[jax.ShapeDtypeStruct((n, nqk), BF16), jax.ShapeDtypeStruct((n, nqk), BF16),
                   jax.ShapeDtypeStruct((n, nv), BF16), jax.ShapeDtypeStruct((n, nv), F32)],
        compiler_params=_cparams("parallel"),
        name="ret_proj",
    )(x, mod, mod, norm_g.reshape(1, d), w_in_bf16, cos, sin)


def _head_out(o, gate, norm_g, w_out, center):
    if center:
        o = o - jnp.mean(o, axis=-1, keepdims=True)
    y = o * lax.rsqrt(jnp.mean(o * o, axis=-1, keepdims=True) + EPS) * norm_g
    return _dot((y * _silu(gate)).astype(BF16), w_out)


def _ret_core_kernel(lg_ref, q_ref, k_ref, v_ref, gate_ref, x_ref, g1_ref, ng_ref, w_ref,
                     out_ref, s_out_ref, s_scr, *, tc, n_chunks, n_heads, dk, dv):
    c = pl.program_id(1)

    @pl.when(c == 0)
    def _():
        s_scr[...] = jnp.zeros_like(s_scr)

    ii = lax.broadcasted_iota(I32, (tc, tc), 0)
    jj = lax.broadcasted_iota(I32, (tc, tc), 1)
    causal = ii >= jj
    lag = (ii - jj).astype(F32)
    row = lax.broadcasted_iota(I32, (tc, 1), 0).astype(F32)
    mixed = None
    for hd in range(n_heads):
        lg = lg_ref[hd]
        q, k = q_ref[:, hd * dk:(hd + 1) * dk], k_ref[:, hd * dk:(hd + 1) * dk]
        cols = slice(hd * dv, (hd + 1) * dv)
        v = v_ref[:, cols]
        dec = jnp.where(causal, jnp.exp(lag * lg), 0.0)
        att = (_dot_nt(q, k) * dec).astype(BF16)
        s_old = s_scr[hd]
        o = _dot(att, v) + jnp.exp((row + 1.0) * lg) * _dot(q, s_old.astype(BF16))
        part = _head_out(o, gate_ref[:, cols], ng_ref[:, cols], w_ref[cols, :], center=True)
        mixed = part if mixed is None else mixed + part
        kd = (k.astype(F32) * jnp.exp((tc - 1.0 - row) * lg)).astype(BF16)
        s_scr[hd] = jnp.exp(jnp.full((1, 1), float(tc), F32) * lg) * s_old + _dot_tn(kd, v)
    out_ref[...] = x_ref[...] + g1_ref[...] * mixed

    @pl.when(c == n_chunks - 1)
    def _():
        s_out_ref[0] = s_scr[...]


def _core_specs(tc, nc, d, widths):
    row = lambda width: pl.BlockSpec((tc, width), lambda b, c: (b * nc + c, 0))
    g1 = pl.BlockSpec((None, 1, d), lambda b, c: (b, 0, 2))
    return [row(w) for w in widths], row(d), g1


def _ret_core(q, k, v, gate, x, mod, norm_g, w_out_bf16, log_gamma, batch, seq, n_heads, dk, dv):
    tc = min(RET_CHUNK, seq)
    nc = seq // tc
    assert seq % tc == 0
    n, d = x.shape
    nqk, nv = n_heads * dk, n_heads * dv
    ins, xrow, g1 = _core_specs(tc, nc, d, (nqk, nqk, nv, nv))
    return pl.pallas_call(
        functools.partial(_ret_core_kernel, tc=tc, n_chunks=nc, n_heads=n_heads, dk=dk, dv=dv),
        grid=(batch, nc),
        in_specs=[pl.BlockSpec(memory_space=pltpu.SMEM), *ins, xrow, g1,
                  _resident((1, nv)), _resident(w_out_bf16.shape)],
        out_specs=[xrow, pl.BlockSpec((1, n_heads, dk, dv), lambda b, c: (b, 0, 0, 0))],
        out_shape=[jax.ShapeDtypeStruct((n, d), F32),
                   jax.ShapeDtypeStruct((batch, n_heads, dk, dv), F32)],
        scratch_shapes=[pltpu.VMEM((n_heads, dk, dv), F32)],
        compiler_params=_cparams("parallel", "arbitrary"),
        name="ret_core",
    )(log_gamma, q, k, v, gate, x, mod, norm_g.reshape(1, nv), w_out_bf16)


def _step_kernel(s_ref, q_ref, k_ref, a_ref, v_ref, o_ref, s_out_ref, *, tb):
    for j in range(tb):
        s_new = (a_ref[0, 0, :, j:j + 1] * s_ref[j, 0]
                 + k_ref[0, 0, :, j:j + 1] * v_ref[j:j + 1, :])
        s_out_ref[j, 0] = s_new
        o_ref[j:j + 1, :] = jnp.sum(q_ref[0, 0, :, j:j + 1] * s_new, axis=0, keepdims=True)


def _columns(x, n_heads, dk, tb):
    n = x.shape[0]
    return x.astype(F32).reshape(n // tb, tb, n_heads, dk).transpose(2, 0, 3, 1)


def _step(states, layer, q, k, a, v, n_heads, dk, dv):
    n = states.shape[1]
    tb = STEP_TOKENS
    assert n % tb == 0
    col = pl.BlockSpec((1, 1, dk, tb), lambda i, h: (h, i, 0, 0))
    st_in = pl.BlockSpec((None, tb, 1, dk, dv), lambda i, h: (layer, i, h, 0, 0))
    st_out = pl.BlockSpec((tb, 1, dk, dv), lambda i, h: (i, h, 0, 0))
    row = pl.BlockSpec((tb, dv), lambda i, h: (i, h))
    return pl.pallas_call(
        functools.partial(_step_kernel, tb=tb),
        grid=(n // tb, n_heads),
        in_specs=[st_in, col, col, col, row],
        out_specs=[row, st_out],
        out_shape=[jax.ShapeDtypeStruct((n, n_heads * dv), F32),
                   jax.ShapeDtypeStruct(states.shape[1:], states.dtype)],
        compiler_params=_cparams("parallel", "parallel"),
        name="state_step",
    )(states, _columns(q, n_heads, dk, tb), _columns(k, n_heads, dk, tb),
      _columns(a, n_heads, dk, tb), v.astype(F32))


def _gla_proj_kernel(x_ref, sh_ref, sc_ref, g_ref, w_ref, a1_ref, a2_ref, ba_ref,
                     q_ref, k_ref, v_ref, r_ref, la_ref, *, nqk, nv):
    h = _modulate(x_ref[...], g_ref[...], sh_ref[...], sc_ref[...]).astype(BF16)
    q_ref[...] = _dot(h, w_ref[:, :nqk]) * ((nqk // GLA_HEADS) ** -0.5)
    k_ref[...] = _dot(h, w_ref[:, nqk:2 * nqk])
    v_ref[...] = _dot(h, w_ref[:, 2 * nqk:2 * nqk + nv]).astype(BF16)
    r_ref[...] = _dot(h, w_ref[:, 2 * nqk + nv:])
    z = _dot(_dot(h, a1_ref[...]).astype(BF16), a2_ref[...]) + ba_ref[...]
    la_ref[...] = (jnp.minimum(z, 0.0) - jnp.log1p(jnp.exp(-jnp.abs(z)))) / GLA_TAU


def _gla_proj(x, mod, rows, norm_g, w_in_bf16, a1_bf16, a2_bf16, b_a, nqk, nv):
    n, d = x.shape
    return pl.pallas_call(
        functools.partial(_gla_proj_kernel, nqk=nqk, nv=nv),
        grid=(rows.grid,),
        in_specs=[rows.rows(d), rows.mod(d, 0), rows.mod(d, 1), _resident((1, d)),
                  _resident(w_in_bf16.shape), _resident(a1_bf16.shape),
                  _resident(a2_bf16.shape), _resident((1, nqk))],
        out_specs=[rows.rows(nqk), rows.rows(nqk), rows.rows(nv), rows.rows(nv), rows.rows(nqk)],
        out_shape=[jax.ShapeDtypeStruct((n, nqk), F32), jax.ShapeDtypeStruct((n, nqk), F32),
                   jax.ShapeDtypeStruct((n, nv), BF16), jax.ShapeDtypeStruct((n, nv), F32),
                   jax.ShapeDtypeStruct((n, nqk), F32)],
        compiler_params=_cparams("parallel"),
        name="gla_proj",
    )(x, mod, mod, norm_g.reshape(1, d), w_in_bf16, a1_bf16, a2_bf16, b_a.reshape(1, nqk))


def _gla_core_kernel(q_ref, k_ref, la_ref, v_ref, o_ref, s_out_ref, s_scr,
                     *, tc, n_chunks, n_heads, dk, dv):
    c = pl.program_id(1)

    @pl.when(c == 0)
    def _():
        s_scr[...] = jnp.zeros_like(s_scr)

    ii = lax.broadcasted_iota(I32, (tc, tc), 0)
    jj = lax.broadcasted_iota(I32, (tc, tc), 1)
    la = la_ref[...]
    la_hi = la.astype(BF16)
    la_mid = (la - la_hi.astype(F32)).astype(BF16)
    la_lo = (la - la_hi.astype(F32) - la_mid.astype(F32)).astype(BF16)
    tri = jnp.where(ii >= jj, 1.0, 0.0).astype(BF16)
    b_all = _dot(tri, la_hi) + _dot(tri, la_mid) + _dot(tri, la_lo)
    ones = jnp.ones((tc, LANES), BF16)
    tot_col_all = (_dot_tn(la_hi, ones) + _dot_tn(la_mid, ones) + _dot_tn(la_lo, ones))[:, 0:1]
    qi = lax.broadcasted_iota(I32, (tc, GLA_KEY_BLOCK), 0)
    kj = lax.broadcasted_iota(I32, (tc, GLA_KEY_BLOCK), 1)
    for hd in range(n_heads):
        cols = slice(hd * dk, (hd + 1) * dk)
        b = b_all[:, cols]
        q, k, v = q_ref[:, cols], k_ref[:, cols], v_ref[:, hd * dv:(hd + 1) * dv]
        s_old = s_scr[hd]
        o = _dot((q * jnp.exp(b)).astype(BF16), s_old.astype(BF16))
        for s in range(tc // GLA_KEY_BLOCK):
            lo = s * GLA_KEY_BLOCK
            blk = slice(lo, lo + GLA_KEY_BLOCK)
            mid = lo + GLA_KEY_BLOCK // 2 - 1
            ref = b[mid:mid + 1, :]
            rel = b - ref
            if s:
                rel = jnp.where(qi[:, 0:1] < lo, 0.0, rel)
            qs = (q * jnp.exp(rel)).astype(BF16)
            ks = (k[blk, :] * jnp.exp(ref - b[blk, :])).astype(BF16)
            att = jnp.where(qi >= kj + lo, _dot_nt(qs, ks), 0.0).astype(BF16)
            o = o + _dot(att, v[blk, :])
        o_ref[:, hd * dv:(hd + 1) * dv] = o
        kd = (k * jnp.exp(b[tc - 1:tc, :] - b)).astype(BF16)
        s_scr[hd] = jnp.exp(tot_col_all[cols, :]) * s_old + _dot_tn(kd, v)

    @pl.when(c == n_chunks - 1)
    def _():
        s_out_ref[0] = s_scr[...]


def _gla_core(q, k, v, la, batch, seq, n_heads, dk, dv):
    tc = min(GLA_CHUNK, seq)
    nc = seq // tc
    assert seq % tc == 0
    n = batch * seq
    nqk, nv = n_heads * dk, n_heads * dv
    (qk, vv), _, _ = _core_specs(tc, nc, nv, (nqk, nv))
    return pl.pallas_call(
        functools.partial(_gla_core_kernel, tc=tc, n_chunks=nc, n_heads=n_heads, dk=dk, dv=dv),
        grid=(batch, nc),
        in_specs=[qk, qk, qk, vv],
        out_specs=[vv, pl.BlockSpec((1, n_heads, dk, dv), lambda b, c: (b, 0, 0, 0))],
        out_shape=[jax.ShapeDtypeStruct((n, nv), F32),
                   jax.ShapeDtypeStruct((batch, n_heads, dk, dv), F32)],
        scratch_shapes=[pltpu.VMEM((n_heads, dk, dv), F32)],
        compiler_params=_cparams("parallel", "arbitrary"),
        name="gla_core",
    )(q, k, la, v)


def _post_kernel(o_ref, gate_ref, x_ref, g1_ref, ng_ref, w_ref, out_ref, *, n_heads, dv, center):
    acc = None
    for hd in range(n_heads):
        sl = slice(hd * dv, (hd + 1) * dv)
        part = _head_out(o_ref[:, sl], gate_ref[:, sl], ng_ref[:, sl], w_ref[sl, :], center)
        acc = part if acc is None else acc + part
    out_ref[...] = x_ref[...] + g1_ref[...] * acc


def _post(o, gate, x, mod, rows, norm_g, w_out_bf16, n_heads, dv, center):
    n, d = x.shape
    nv = n_heads * dv
    return pl.pallas_call(
        functools.partial(_post_kernel, n_heads=n_heads, dv=dv, center=center),
        grid=(rows.grid,),
        in_specs=[rows.rows(nv), rows.rows(nv), rows.rows(d), rows.mod(d, 2),
                  _resident((1, nv)), _resident(w_out_bf16.shape)],
        out_specs=rows.rows(d),
        out_shape=jax.ShapeDtypeStruct((n, d), F32),
        compiler_params=_cparams("parallel"),
        name="mixer_post",
    )(o, gate, x, mod, norm_g.reshape(1, nv), w_out_bf16)


def _first_max(vals, idx_iota, n):
    m = jnp.max(vals, axis=0, keepdims=True)
    i = jnp.min(jnp.where(vals == m, idx_iota, n), axis=0, keepdims=True)
    return m, i


def _router_kernel(x_ref, sh_ref, sc_ref, g_ref, wr_ref, b_ref,
                   h2_ref, slot_ref, wt_ref, cnt_ref):
    h2 = _modulate(x_ref[...], g_ref[...], sh_ref[...], sc_ref[...])
    h2_ref[...] = h2.astype(BF16)
    h_hi = h2.astype(BF16)
    h_lo = (h2 - h_hi.astype(F32)).astype(BF16)
    wr = wr_ref[...]
    w_hi = wr.astype(BF16)
    w_lo = (wr - w_hi.astype(F32)).astype(BF16)
    logits = _dot_nt(w_hi, h_hi) + _dot_nt(w_hi, h_lo) + _dot_nt(w_lo, h_hi)
    scores = _sigmoid(logits)
    biased = scores + b_ref[...]
    n_exp, win = biased.shape
    gsz = n_exp // N_GROUPS
    neg = -jnp.inf

    sub_g = lax.broadcasted_iota(I32, (gsz, win), 0)
    group_rows = []
    for g in range(N_GROUPS):
        blk = biased[g * gsz:(g + 1) * gsz, :]
        m1, i1 = _first_max(blk, sub_g, gsz)
        m2 = jnp.max(jnp.where(sub_g == i1, neg, blk), axis=0, keepdims=True)
        group_rows.append(m1 + m2)
    cur = jnp.concatenate(group_rows, axis=0)
    sub_n = lax.broadcasted_iota(I32, (N_GROUPS, win), 0)
    gmask = jnp.zeros((N_GROUPS, win), jnp.bool_)
    for _ in range(TOPK_GROUPS):
        _, i = _first_max(cur, sub_n, N_GROUPS)
        sel = sub_n == i
        gmask = gmask | sel
        cur = jnp.where(sel, neg, cur)
    emask = jnp.concatenate(
        [jnp.broadcast_to(gmask[g:g + 1, :], (gsz, win)) for g in range(N_GROUPS)], axis=0)
    masked = jnp.where(emask, biased, neg)

    sub_e = lax.broadcasted_iota(I32, (n_exp, win), 0)
    sels, picked = [], []
    for _ in range(TOP_K):
        _, i = _first_max(masked, sub_e, n_exp)
        sel = sub_e == i
        sels.append(sel)
        picked.append(jnp.sum(jnp.where(sel, scores, 0.0), axis=0, keepdims=True))
        masked = jnp.where(sel, neg, masked)
    total = picked[0]
    for p in picked[1:]:
        total = total + p
    wt_ref[...] = jnp.concatenate([p / total * ROUTED_SCALE for p in picked], axis=0)

    chosen = sels[0]
    for s in sels[1:]:
        chosen = chosen | s
    sel_b = jnp.where(chosen, 1.0, 0.0).astype(BF16)
    r = lax.broadcasted_iota(I32, (win, win), 0)
    c = lax.broadcasted_iota(I32, (win, win), 1)
    before = jnp.where(r < c, 1.0, 0.0).astype(BF16)
    rank = _dot(sel_b, before)
    count = _dot(sel_b, jnp.ones((win, win), BF16))
    run_len = jnp.floor((count + (RUN_ALIGN - 1.0)) * (1.0 / RUN_ALIGN)) * RUN_ALIGN
    er = lax.broadcasted_iota(I32, (n_exp, n_exp), 0)
    ec = lax.broadcasted_iota(I32, (n_exp, n_exp), 1)
    lower = jnp.where(ec < er, 1.0, 0.0).astype(BF16)
    slot_all = _dot(lower, run_len.astype(BF16)) + rank
    slot_ref[...] = jnp.concatenate(
        [jnp.sum(jnp.where(s, slot_all, 0.0), axis=0, keepdims=True) for s in sels],
        axis=0).astype(I32)
    cnt_ref[0] = _dot_nt(jnp.ones((8, win), BF16), sel_b)


def _router(x, mod, rows, norm_g, w_router, bias):
    n, d = x.shape
    n_exp = w_router.shape[1]
    win = rows.tile
    lanes = pl.BlockSpec((TOP_K, win), lambda i: (0, i))
    return pl.pallas_call(
        _router_kernel,
        grid=(rows.grid,),
        in_specs=[rows.rows(d), rows.mod(d, 3), rows.mod(d, 4), _resident((1, d)),
                  _resident((n_exp, d)), _resident((n_exp, 1))],
        out_specs=[rows.rows(d), lanes, lanes, pl.BlockSpec((1, 8, n_exp), lambda i: (i, 0, 0))],
        out_shape=[jax.ShapeDtypeStruct((n, d), BF16), jax.ShapeDtypeStruct((TOP_K, n), I32),
                   jax.ShapeDtypeStruct((TOP_K, n), F32),
                   jax.ShapeDtypeStruct((rows.grid, 8, n_exp), F32)],
        compiler_params=_cparams("parallel"),
        name="moe_router",
    )(x, mod, mod, norm_g.reshape(1, d), w_router.T, bias.reshape(n_exp, 1))


def _sorted_rows(win, n_exp):
    return -(-(win * TOP_K + n_exp * (RUN_ALIGN - 1)) // SORT_BLOCK) * SORT_BLOCK


def _copy_list(count, src0, dst0, length, step):
    n_exp = count.shape[1]
    end = jnp.cumsum(count, axis=1)
    idx = jnp.arange(length, dtype=I32)
    owner = jnp.sum((end[:, None, :] <= idx[None, :, None]).astype(I32), axis=2)
    own = (owner[:, :, None] == jnp.arange(n_exp, dtype=I32)[None, None, :]).astype(I32)
    pick = lambda a: jnp.sum(own * a[:, None, :], axis=2)
    off = (idx[None, :] - pick(end - count)) * step
    return (pick(src0) + off).reshape(-1), (pick(dst0) + off).reshape(-1), end[:, -1]


def _plan(cnt, tile, sorted_rows):
    c = cnt[:, 0, :].astype(I32)
    run = (c + RUN_ALIGN - 1) // RUN_ALIGN * RUN_ALIGN
    per_exp = jnp.sum(run, axis=0)
    padded = (per_exp + tile - 1) // tile * tile
    pend = jnp.cumsum(padded)
    pstart = pend - padded
    dest = pstart[None, :] + jnp.cumsum(run, axis=0) - run
    local = jnp.cumsum(run, axis=1) - run
    n_double = run // (2 * RUN_ALIGN)
    covered = n_double * (2 * RUN_ALIGN)
    doubles = _copy_list(n_double, local, dest, sorted_rows // (2 * RUN_ALIGN), 2 * RUN_ALIGN)
    singles = _copy_list((run - covered) // RUN_ALIGN, local + covered, dest + covered,
                         c.shape[1], RUN_ALIGN)
    return dict(
        copies=doubles + singles, list_rows=sorted_rows,
        pad_start=pstart + per_exp, pad_chunks=(padded - per_exp) // RUN_ALIGN,
        pend=pend, used_tiles=(pend[-1] // tile).reshape(1))


def _capacity(n, n_win, n_exp, tile):
    worst = n * TOP_K + n_win * n_exp * (RUN_ALIGN - 1) + n_exp * (tile - RUN_ALIGN)
    return -(-worst // tile) * tile


def _window_copies(copy_refs, window, sorted_rows, n_exp, make_copy, wait):
    d_src, d_dst, d_n, s_src, s_dst, s_n = copy_refs
    for src_ref, dst_ref, n_ref, length, n_rows in (
            (d_src, d_dst, d_n, sorted_rows // (2 * RUN_ALIGN), 2 * RUN_ALIGN),
            (s_src, s_dst, s_n, n_exp, RUN_ALIGN)):
        def per_copy(i, carry, src_ref=src_ref, dst_ref=dst_ref, length=length, n_rows=n_rows):
            cp = make_copy(pl.multiple_of(src_ref[window * length + i], RUN_ALIGN),
                           pl.multiple_of(dst_ref[window * length + i], RUN_ALIGN), n_rows)
            if wait:
                cp.wait()
            else:
                cp.start()
            return carry

        lax.fori_loop(0, n_ref[window], per_copy, 0)


def _dispatch_kernel(d_src, d_dst, d_n, s_src, s_dst, s_n, pad_start_ref, pad_chunks_ref, used_ref,
                     *refs, n_exp, n_win, tile, groups, list_rows):
    xbuf_ref, xs_scr, zero_scr, sems, tail_sem = refs[-5:]
    w = pl.program_id(0)
    cur = w % 2
    n_tail = xbuf_ref.shape[0] // tile - used_ref[0]

    def tail_copy(j):
        dst = pl.multiple_of((used_ref[0] + j) * tile, tile)
        return pltpu.make_async_copy(zero_scr, xbuf_ref.at[pl.ds(dst, tile)], tail_sem)

    @pl.when(w == 0)
    def _():
        zero_scr[...] = jnp.zeros_like(zero_scr)
        lax.fori_loop(0, n_tail, lambda j, c: (tail_copy(j).start(), c)[1], 0)

    def sort_window(h2_ref, slot_ref, n_rows):
        slot = slot_ref[...]
        win = slot.shape[1]
        h2 = h2_ref[...]
        local = lax.broadcasted_iota(I32, (ONEHOT_BLOCK, win), 0).astype(F32).astype(BF16)
        slot_block = slot // ONEHOT_BLOCK
        slot_local = (slot % ONEHOT_BLOCK).astype(F32)
        one, zero = jnp.ones((), BF16), jnp.zeros((), BF16)
        for blk in range(n_rows // ONEHOT_BLOCK):
            key = jnp.where(slot_block == blk, slot_local, -1.0).astype(BF16)
            onehot = jnp.where(local == key[0:1, :], one, zero)
            for k in range(1, TOP_K):
                onehot = jnp.where(local == key[k:k + 1, :], one, onehot)
            lo = blk * ONEHOT_BLOCK
            xs_scr[cur, lo:lo + ONEHOT_BLOCK, :] = _dot(onehot, h2).astype(BF16)

    for gi, (first, count, n_rows) in enumerate(groups):
        @pl.when(jnp.logical_and(w >= first, w < first + count))
        def _(gi=gi, n_rows=n_rows):
            sort_window(refs[2 * gi], refs[2 * gi + 1], n_rows)

    def copies(window, buf, wait):
        def copy(src, dst, n):
            return pltpu.make_async_copy(xs_scr.at[buf, pl.ds(src, n)],
                                         xbuf_ref.at[pl.ds(dst, n)], sems.at[buf])
        _window_copies((d_src, d_dst, d_n, s_src, s_dst, s_n), window, list_rows, n_exp, copy, wait)

    copies(w, cur, wait=False)

    @pl.when(w > 0)
    def _():
        copies(w - 1, 1 - cur, wait=True)

    @pl.when(w == n_win - 1)
    def _():
        copies(w, cur, wait=True)

    def fill_pads_and_finish():
        sem = sems.at[cur]

        big = PAD_PIECE // RUN_ALIGN

        def pad_copy(row, n_rows):
            return pltpu.make_async_copy(zero_scr.at[pl.ds(0, n_rows)],
                                         xbuf_ref.at[pl.ds(pl.multiple_of(row, RUN_ALIGN), n_rows)],
                                         sem)

        def pads(wait):
            def go(cp):
                if wait:
                    cp.wait()
                else:
                    cp.start()

            def per_expert(e, carry):
                n_big = pad_chunks_ref[e] // big
                rest = pad_start_ref[e] + n_big * PAD_PIECE
                lax.fori_loop(0, n_big, lambda j, c: (go(pad_copy(
                    pad_start_ref[e] + j * PAD_PIECE, PAD_PIECE)), c)[1], 0)
                lax.fori_loop(0, pad_chunks_ref[e] - n_big * big, lambda j, c: (go(pad_copy(
                    rest + j * RUN_ALIGN, RUN_ALIGN)), c)[1], 0)
                return carry
            lax.fori_loop(0, n_exp, per_expert, 0)

        pads(False)
        pads(True)
        lax.fori_loop(0, n_tail, lambda j, c: (tail_copy(j).wait(), c)[1], 0)

    pl.when(w == n_win - 1)(fill_pads_and_finish)


def _dispatch(sources, plan, n_exp, cap, tile):
    d = sources[0][0].shape[1]
    in_specs, operands, groups, first = [], [], [], 0
    for h2, slot_t, rows in sources:
        idx = lambda i, first=first, last=rows.grid - 1: jnp.clip(i - first, 0, last)
        in_specs += [pl.BlockSpec((rows.tile, d), lambda i, *_, idx=idx: (idx(i), 0)),
                     pl.BlockSpec((TOP_K, rows.tile), lambda i, *_, idx=idx: (0, idx(i)))]
        operands += [h2, slot_t]
        groups.append((first, rows.grid, _sorted_rows(rows.tile, n_exp)))
        first += rows.grid
    return pl.pallas_call(
        functools.partial(_dispatch_kernel, n_exp=n_exp, n_win=first, tile=tile,
                          groups=tuple(groups), list_rows=plan["list_rows"]),
        grid_spec=pltpu.PrefetchScalarGridSpec(
            num_scalar_prefetch=9,
            grid=(first,),
            in_specs=in_specs,
            out_specs=pl.BlockSpec(memory_space=pl.ANY),
            scratch_shapes=[pltpu.VMEM((2, plan["list_rows"], d), BF16),
                            pltpu.VMEM((tile, d), BF16),
                            pltpu.SemaphoreType.DMA((2,)), pltpu.SemaphoreType.DMA(())]),
        out_shape=jax.ShapeDtypeStruct((cap, d), BF16),
        compiler_params=_cparams("arbitrary"),
        name="moe_dispatch",
    )(*plan["copies"], plan["pad_start"], plan["pad_chunks"], plan["used_tiles"], *operands)


def _experts_kernel(texp_ref, used_ref, xbuf_ref, wg_ref, wu_ref, wd_ref, y_ref,
                    x_ring, sems, wgu_scr, wd_scr, *, sub_tiles):
    i = pl.program_id(0)
    used = used_ref[0]
    live = i < used
    de = wd_scr.shape[0]
    depth, tile = x_ring.shape[0], x_ring.shape[1]

    def fetch(j):
        slot = j % depth
        return pltpu.make_async_copy(xbuf_ref.at[pl.ds(pl.multiple_of(j * tile, tile), tile)],
                                     x_ring.at[slot], sems.at[slot])

    @pl.when(i == 0)
    def _():
        for j in range(depth - 1):
            @pl.when(j < used)
            def _():
                fetch(j).start()

    @pl.when(i + (depth - 1) < used)
    def _():
        fetch(i + (depth - 1)).start()

    @pl.when(jnp.logical_or(i == 0, texp_ref[i] != texp_ref[jnp.maximum(i - 1, 0)]))
    def _():
        wgu_scr[:, :de] = wg_ref[...].astype(BF16)
        wgu_scr[:, de:] = wu_ref[...].astype(BF16)
        wd_scr[...] = wd_ref[...].astype(BF16)

    @pl.when(live)
    def _():
        fetch(i).wait()
        x_ref = x_ring.at[i % depth]
        sub = tile // sub_tiles
        for s in range(sub_tiles):
            rows = slice(s * sub, (s + 1) * sub)
            ab = _dot(x_ref[rows, :], wgu_scr[...])
            mid = (_silu(ab[:, :de]) * ab[:, de:]).astype(BF16)
            y_ref[rows, :] = _dot(mid, wd_scr[...]).astype(BF16)

    @pl.when(jnp.logical_not(live))
    def _():
        y_ref[...] = jnp.zeros_like(y_ref)


def _experts(xbuf, plan, layer, w_gate, w_up, w_down, tile):
    cap, d = xbuf.shape
    _, n_exp, _, de = w_gate.shape
    n_tiles = cap // tile
    used = plan["used_tiles"]
    tmap = jnp.minimum(jnp.arange(n_tiles, dtype=I32), used[0] - 1)
    texp = jnp.sum((plan["pend"][None, :] <= (tmap * tile)[:, None]).astype(I32), axis=1)
    texp = jnp.minimum(texp, n_exp - 1)
    return pl.pallas_call(
        functools.partial(_experts_kernel, sub_tiles=max(tile // EXPERT_SUB_TILE, 1)),
        grid_spec=pltpu.PrefetchScalarGridSpec(
            num_scalar_prefetch=2,
            grid=(n_tiles,),
            in_specs=[pl.BlockSpec(memory_space=pl.ANY),
                      pl.BlockSpec((None, None, d, de), lambda i, te, u: (layer, te[i], 0, 0)),
                      pl.BlockSpec((None, None, d, de), lambda i, te, u: (layer, te[i], 0, 0)),
                      pl.BlockSpec((None, None, de, d), lambda i, te, u: (layer, te[i], 0, 0))],
            out_specs=pl.BlockSpec((tile, d), lambda i, te, u: (i, 0)),
            scratch_shapes=[pltpu.VMEM((EXPERT_RING, tile, d), BF16),
                            pltpu.SemaphoreType.DMA((EXPERT_RING,)),
                            pltpu.VMEM((d, 2 * de), BF16), pltpu.VMEM((de, d), BF16)]),
        out_shape=jax.ShapeDtypeStruct((cap, d), BF16),
        compiler_params=_cparams("arbitrary"),
        name="moe_experts",
    )(texp, used, xbuf, w_gate, w_up, w_down)


def _combine_kernel(d_src, d_dst, d_n, s_src, s_dst, s_n, ybuf_ref, slot_ref, wt_ref, h2_ref,
                    x_ref, g2_ref, sg_ref, su_ref, sd_ref, fin_ref, out_ref, ys_scr, sems,
                    *, n_exp, n_win, final_norm, window_base, list_rows):
    w = pl.program_id(0)
    cur = w % 2
    n_rows = ys_scr.shape[1]

    def copies(window, buf, wait):
        def copy(loc, dst, n):
            return pltpu.make_async_copy(ybuf_ref.at[pl.ds(dst, n)],
                                         ys_scr.at[buf, pl.ds(loc, n)], sems.at[buf])
        _window_copies((d_src, d_dst, d_n, s_src, s_dst, s_n), window_base + window, list_rows,
                       n_exp, copy, wait)

    @pl.when(w == 0)
    def _():
        ys_scr[...] = jnp.zeros_like(ys_scr)
        copies(w, cur, wait=False)

    @pl.when(w + 1 < n_win)
    def _():
        copies(w + 1, 1 - cur, wait=False)

    h2 = h2_ref[...]
    shared = _dot((_silu(_dot(h2, sg_ref[...])) * _dot(h2, su_ref[...])).astype(BF16), sd_ref[...])
    copies(w, cur, wait=True)

    slot, wt = slot_ref[...], wt_ref[...].astype(BF16)
    win = slot.shape[1]
    local = lax.broadcasted_iota(I32, (ONEHOT_BLOCK, win), 0).astype(F32).astype(BF16)
    slot_block = slot // ONEHOT_BLOCK
    slot_local = (slot % ONEHOT_BLOCK).astype(F32)
    routed = shared
    for blk in range(n_rows // ONEHOT_BLOCK):
        key = jnp.where(slot_block == blk, slot_local, -1.0).astype(BF16)
        weights = jnp.where(local == key[0:1, :], wt[0:1, :], jnp.zeros((), BF16))
        for k in range(1, TOP_K):
            weights = jnp.where(local == key[k:k + 1, :], wt[k:k + 1, :], weights)
        lo = blk * ONEHOT_BLOCK
        routed = routed + _dot_tn(weights, ys_scr[cur, lo:lo + ONEHOT_BLOCK, :])
    out = x_ref[...] + g2_ref[...] * routed
    if final_norm:
        out = out * lax.rsqrt(jnp.mean(out * out, axis=-1, keepdims=True) + EPS) * fin_ref[...]
    out_ref[...] = out


def _combine(ybuf, plan, slot, wt, h2, x, mod, rows, sh_bf16, final_g, n_exp, final_norm,
             window_base):
    n, d = x.shape
    win = rows.tile
    sg, su, sd = sh_bf16
    pairs = pl.BlockSpec((TOP_K, win), lambda i, *_: (0, i))
    return pl.pallas_call(
        functools.partial(_combine_kernel, n_exp=n_exp, n_win=rows.grid, final_norm=final_norm,
                          window_base=window_base, list_rows=plan["list_rows"]),
        grid_spec=pltpu.PrefetchScalarGridSpec(
            num_scalar_prefetch=6,
            grid=(rows.grid,),
            in_specs=[pl.BlockSpec(memory_space=pl.ANY), pairs, pairs, rows.rows(d), rows.rows(d),
                      rows.mod(d, 5), _resident(sg.shape), _resident(su.shape),
                      _resident(sd.shape), _resident((1, d))],
            out_specs=rows.rows(d),
            scratch_shapes=[pltpu.VMEM((2, _sorted_rows(win, n_exp), d), BF16),
                            pltpu.SemaphoreType.DMA((2,))]),
        out_shape=jax.ShapeDtypeStruct((n, d), F32),
        compiler_params=_cparams("arbitrary"),
        name="moe_combine",
    )(*plan["copies"], ybuf, slot, wt, h2, x, mod, sg, su, sd, final_g.reshape(1, d))


def _moe(groups, layer, norm_g, w_router, bias, w_gate, w_up, w_down, sh_bf16, final_g,
         final_norm, tile):
    n_exp = w_router.shape[1]
    routed = [_router(x, mod, rows, norm_g, w_router, bias) for x, mod, rows in groups]
    bases, total = [], 0
    for _, _, rows in groups:
        bases.append(total)
        total += rows.grid
    plan = _plan(jnp.concatenate([r[3] for r in routed], axis=0), tile,
                 max(_sorted_rows(rows.tile, n_exp) for _, _, rows in groups))
    cap = _capacity(sum(x.shape[0] for x, _, _ in groups), total, n_exp, tile)
    xbuf = _dispatch([(r[0], r[1], g[2]) for r, g in zip(routed, groups)], plan, n_exp, cap, tile)
    ybuf = _experts(xbuf, plan, layer, w_gate, w_up, w_down, tile)
    return [_combine(ybuf, plan, slot_t, wt_t, h2, x, mod, rows, sh_bf16, final_g, n_exp,
                     final_norm, base)
            for (x, mod, rows), (h2, slot_t, wt_t, _), base in zip(groups, routed, bases)]


def _rope_tables(pos, half):
    inv = ROPE_BASE ** (-jnp.arange(half, dtype=F32) / half)
    ang = pos.astype(F32)[:, None] * inv[None, :]
    return jnp.cos(ang), jnp.sin(ang)


class _Group:
    def __init__(self, x3, mod_all, pos0, s_ret, s_gla):
        self.batch, self.seq, d = x3.shape
        self.n = self.batch * self.seq
        self.x = x3.reshape(self.n, d)
        self.rows = _Rows(self.n, self.seq, min(ROW_TILE, self.n))
        self.decode = self.seq == 1
        self.mod_all, self.pos0, self.s_ret, self.s_gla = mod_all, pos0, s_ret, s_gla
        self.new_ret, self.new_gla = [], []

    def mod(self, layer):
        m = self.mod_all[layer]
        return m if self.decode else m.reshape(self.batch, 1, m.shape[-1])


def _mixer(g, layer, wts):
    d = g.x.shape[1]
    x, mod, rows = g.x, g.mod(layer), g.rows
    ret_dk = d // RET_HEADS
    ret_dv = 2 * ret_dk
    gla_dk = d // (2 * GLA_HEADS)
    gla_dv = d // GLA_HEADS
    j = layer // 2
    if layer % 2 == 0:
        pos = (jnp.full((rows.tile,), g.pos0, I32) if g.decode
               else g.pos0 + jnp.arange(g.seq, dtype=I32))
        cos, sin = _rope_tables(pos, ret_dk // 2)
        q, k, v, gate = _ret_proj(x, mod, rows, wts["norm_mix_g"][layer], wts["ret_w_in"][j],
                                  cos, sin, RET_HEADS, ret_dk, ret_dv)
        log_gamma = jnp.log1p(-jnp.power(2.0, -5.0 - jnp.arange(RET_HEADS, dtype=F32)))
        if g.decode:
            decay = jnp.broadcast_to(jnp.repeat(jnp.exp(log_gamma), ret_dk)[None, :],
                                     (g.n, RET_HEADS * ret_dk))
            o, s_new = _step(g.s_ret, j, q, k, decay, v, RET_HEADS, ret_dk, ret_dv)
            out = _post(o, gate, x, mod, rows, wts["ret_norm_g"][j], wts["ret_w_out"][j],
                        RET_HEADS, ret_dv, center=True)
        else:
            out, s_new = _ret_core(q, k, v, gate, x, mod, wts["ret_norm_g"][j],
                                   wts["ret_w_out"][j], log_gamma, g.batch, g.seq,
                                   RET_HEADS, ret_dk, ret_dv)
        g.new_ret.append(s_new)
        return out
    q, k, v, gate, la = _gla_proj(x, mod, rows, wts["norm_mix_g"][layer], wts["gla_w_in"][j],
                                  wts["gla_w_a1"][j], wts["gla_w_a2"][j], wts["gla_b_a"][j],
                                  GLA_HEADS * gla_dk, GLA_HEADS * gla_dv)
    if g.decode:
        o, s_new = _step(g.s_gla, j, q, k, jnp.exp(la), v, GLA_HEADS, gla_dk, gla_dv)
    else:
        o, s_new = _gla_core(q, k, v, la, g.batch, g.seq, GLA_HEADS, gla_dk, gla_dv)
    g.new_gla.append(s_new)
    return _post(o, gate, x, mod, rows, wts["gla_norm_g"][j], wts["gla_w_out"][j],
                 GLA_HEADS, gla_dv, center=False)


def _trunk(groups, wts):
    depth = wts["ada_w"].shape[0]
    for layer in range(depth):
        mixed = [(_mixer(g, layer, wts), g.mod(layer), g.rows) for g in groups]
        outs = _moe(mixed, layer, wts["norm_ffn_g"][layer], wts["moe_w_router"][layer],
                    wts["moe_router_bias"][layer], wts["moe_w_gate"], wts["moe_w_up"],
                    wts["moe_w_down"], wts["shared"][layer], wts["final_norm_g"],
                    final_norm=layer == depth - 1, tile=EXPERT_TILE)
        for g, x in zip(groups, outs):
            g.x = x
    return [(g.x.reshape(g.batch, g.seq, -1), jnp.stack(g.new_ret), jnp.stack(g.new_gla))
            for g in groups]


def kernel(x_prompt, x_sample, state_ret, state_gla, c_prompt, c_sample, ret_w_in, ret_norm_g, ret_w_out, gla_w_in, gla_w_a1, gla_w_a2, gla_b_a, gla_norm_g, gla_w_out, ada_w, ada_b, norm_mix_g, norm_ffn_g, moe_w_router, moe_router_bias, moe_w_gate, moe_w_up, moe_w_down, sh_w_gate, sh_w_up, sh_w_down, final_norm_g):
    b = x_prompt.shape[0]
    depth = ada_w.shape[0]
    rank = gla_w_a1.shape[-1]
    pad = LANES - rank
    wts = dict(
        ret_w_in=ret_w_in.astype(BF16), ret_norm_g=ret_norm_g, ret_w_out=ret_w_out.astype(BF16),
        gla_w_in=gla_w_in.astype(BF16),
        gla_w_a1=jnp.pad(gla_w_a1, ((0, 0), (0, 0), (0, pad))).astype(BF16),
        gla_w_a2=jnp.pad(gla_w_a2, ((0, 0), (0, pad), (0, 0))).astype(BF16),
        gla_b_a=gla_b_a, gla_norm_g=gla_norm_g, gla_w_out=gla_w_out.astype(BF16),
        ada_w=ada_w, norm_mix_g=norm_mix_g, norm_ffn_g=norm_ffn_g,
        moe_w_router=moe_w_router, moe_router_bias=moe_router_bias,
        moe_w_gate=moe_w_gate, moe_w_up=moe_w_up, moe_w_down=moe_w_down,
        shared=[(sh_w_gate[l].astype(BF16), sh_w_up[l].astype(BF16), sh_w_down[l].astype(BF16))
                for l in range(depth)],
        final_norm_g=final_norm_g)
    mod = _ada(jnp.concatenate([c_prompt, c_sample], axis=0), ada_w, ada_b)
    (y_p, ret_p, gla_p), (y_s, ret_s, gla_s) = _trunk(
        [_Group(x_prompt, mod[:, :b], 0, None, None),
         _Group(x_sample, mod[:, b:], PAST_LEN, state_ret, state_gla)], wts)
    return (y_p, y_s, ret_p, gla_p, ret_s, gla_s)
```

```python
import functools

import jax
import jax.numpy as jnp
from jax import lax
from jax.experimental import pallas as pl
from jax.experimental.pallas import tpu as pltpu

F32, BF16, I32 = jnp.float32, jnp.bfloat16, jnp.int32

EPS = 1e-6
ROPE_BASE = 10000.0
PAST_LEN = 16384
RET_HEADS = 4
GLA_HEADS = 4
GLA_TAU = 16.0
N_GROUPS = 8
TOPK_GROUPS = 4
TOP_K = 8
ROUTED_SCALE = 2.5

LANES = 128
BF16_SUBLANES = 16
VMEM_LIMIT_BYTES = 56 * 1024 * 1024

ROW_TILE = 256
RET_CHUNK = 256
GLA_CHUNK = 128
GLA_KEY_BLOCK = 32
STEP_TOKENS = 8
RUN_ALIGN = BF16_SUBLANES
EXPERT_TILE = 1024
EXPERT_SUB_TILE = 1024
SORT_BLOCK = 512
EXPERT_RING = 3
ONEHOT_BLOCK = 256
PAD_PIECE = 256


def _cparams(*sem):
    return pltpu.CompilerParams(dimension_semantics=sem, vmem_limit_bytes=VMEM_LIMIT_BYTES)


def _sigmoid(x):
    return 1.0 / (1.0 + jnp.exp(-x))


def _silu(x):
    return x * _sigmoid(x)


def _modulate(x, g, shift, scale):
    y = x * lax.rsqrt(jnp.mean(x * x, axis=-1, keepdims=True) + EPS) * g
    return y * (1.0 + scale) + shift


def _dot(a, b):
    return jnp.dot(a, b, preferred_element_type=F32)


def _dot_nt(a, b):
    return lax.dot_general(a, b, (((1,), (1,)), ((), ())), preferred_element_type=F32)


def _dot_tn(a, b):
    return lax.dot_general(a, b, (((0,), (0,)), ((), ())), preferred_element_type=F32)


def _resident(shape):
    zeros = (0,) * len(shape)
    return pl.BlockSpec(shape, lambda *_: zeros, pipeline_mode=pl.Buffered(1))


class _Rows:
    def __init__(self, n_rows, seq_len, tile):
        self.n, self.tile = n_rows, tile
        self.per_row = seq_len == 1
        self.tiles_per_seq = max(seq_len // tile, 1)
        assert n_rows % tile == 0 and (self.per_row or seq_len % tile == 0)
        self.grid = n_rows // tile

    def rows(self, width, col=0):
        return pl.BlockSpec((self.tile, width), lambda i, *_: (i, col))

    def mod(self, d, col):
        if self.per_row:
            return pl.BlockSpec((self.tile, d), lambda i, *_: (i, col))
        tps = self.tiles_per_seq
        return pl.BlockSpec((None, 1, d), lambda i, *_: (i // tps, 0, col))


def _ada_kernel(c_ref, w_ref, b_ref, o_ref):
    s = _silu(c_ref[...]).astype(BF16)
    o_ref[0] = _dot(s, w_ref[0].astype(BF16)) + b_ref[0]


def _ada(c_all, ada_w, ada_b):
    depth, d, d6 = ada_w.shape
    n = c_all.shape[0]
    tn = d6 // 4
    return pl.pallas_call(
        _ada_kernel,
        grid=(depth, d6 // tn),
        in_specs=[pl.BlockSpec((n, d), lambda l, j: (0, 0)),
                  pl.BlockSpec((1, d, tn), lambda l, j: (l, 0, j)),
                  pl.BlockSpec((1, 1, tn), lambda l, j: (l, 0, j))],
        out_specs=pl.BlockSpec((1, n, tn), lambda l, j: (l, 0, j)),
        out_shape=jax.ShapeDtypeStruct((depth, n, d6), F32),
        compiler_params=_cparams("parallel", "parallel"),
        name="ada_mod",
    )(c_all, ada_w, ada_b.reshape(depth, 1, d6))


def _ret_proj_kernel(x_ref, sh_ref, sc_ref, g_ref, w_ref, cos_ref, sin_ref,
                     q_ref, k_ref, v_ref, gate_ref, *, n_heads, dk, dv):
    h = _modulate(x_ref[...], g_ref[...], sh_ref[...], sc_ref[...]).astype(BF16)
    cos, sin = cos_ref[...], sin_ref[...]
    half, nqk, nv = dk // 2, n_heads * dk, n_heads * dv
    for hd in range(n_heads):
        for dst, base, scale in ((q_ref, 0, None), (k_ref, nqk, dk ** -0.5)):
            p = _dot(h, w_ref[:, base + hd * dk:base + (hd + 1) * dk])
            x1, x2 = p[:, :half], p[:, half:]
            r1, r2 = x1 * cos - x2 * sin, x1 * sin + x2 * cos
            if scale is not None:
                r1, r2 = r1 * scale, r2 * scale
            dst[:, hd * dk:hd * dk + half] = r1.astype(BF16)
            dst[:, hd * dk + half:(hd + 1) * dk] = r2.astype(BF16)
    for hd in range(n_heads):
        v_ref[:, hd * dv:(hd + 1) * dv] = _dot(
            h, w_ref[:, 2 * nqk + hd * dv:2 * nqk + (hd + 1) * dv]).astype(BF16)
        gate_ref[:, hd * dv:(hd + 1) * dv] = _dot(
            h, w_ref[:, 2 * nqk + nv + hd * dv:2 * nqk + nv + (hd + 1) * dv])


def _ret_proj(x, mod, rows, norm_g, w_in_bf16, cos, sin, n_heads, dk, dv):
    n, d = x.shape
    nqk, nv = n_heads * dk, n_heads * dv
    half = dk // 2
    tps = rows.tiles_per_seq
    trig = (pl.BlockSpec((rows.tile, half), lambda i: (0, 0)) if rows.per_row
            else pl.BlockSpec((rows.tile, half), lambda i: (i % tps, 0)))
    return pl.pallas_call(
        functools.partial(_ret_proj_kernel, n_heads=n_heads, dk=dk, dv=dv),
        grid=(rows.grid,),
        in_specs=[rows.rows(d), rows.mod(d, 0), rows.mod(d, 1), _resident((1, d)),
                  _resident(w_in_bf16.shape), trig, trig],
        out_specs=[rows.rows(nqk), rows.rows(nqk), rows.rows(nv), rows.rows(nv)],
        out_shape=[jax.ShapeDtypeStruct((n, nqk), BF16), jax.ShapeDtypeStruct((n, nqk), BF16),
                   jax.ShapeDtypeStruct((n, nv), BF16), jax.ShapeDtypeStruct((n, nv), F32)],
        compiler_params=_cparams("parallel"),
        name="ret_proj",
    )(x, mod, mod, norm_g.reshape(1, d), w_in_bf16, cos, sin)


def _head_out(o, gate, norm_g, w_out, center):
    if center:
        o = o - jnp.mean(o, axis=-1, keepdims=True)
    y = o * lax.rsqrt(jnp.mean(o * o, axis=-1, keepdims=True) + EPS) * norm_g
    return _dot((y * _silu(gate)).astype(BF16), w_out)


def _ret_core_kernel(lg_ref, q_ref, k_ref, v_ref, gate_ref, x_ref, g1_ref, ng_ref, w_ref,
                     out_ref, s_out_ref, s_scr, *, tc, n_chunks, n_heads, dk, dv):
    c = pl.program_id(1)

    @pl.when(c == 0)
    def _():
        s_scr[...] = jnp.zeros_like(s_scr)

    ii = lax.broadcasted_iota(I32, (tc, tc), 0)
    jj = lax.broadcasted_iota(I32, (tc, tc), 1)
    causal = ii >= jj
    lag = (ii - jj).astype(F32)
    row = lax.broadcasted_iota(I32, (tc, 1), 0).astype(F32)
    mixed = None
    for hd in range(n_heads):
        lg = lg_ref[hd]
        q, k = q_ref[:, hd * dk:(hd + 1) * dk], k_ref[:, hd * dk:(hd + 1) * dk]
        cols = slice(hd * dv, (hd + 1) * dv)
        v = v_ref[:, cols]
        dec = jnp.where(causal, jnp.exp(lag * lg), 0.0)
        att = (_dot_nt(q, k) * dec).astype(BF16)
        s_old = s_scr[hd]
        o = _dot(att, v) + jnp.exp((row + 1.0) * lg) * _dot(q, s_old.astype(BF16))
        part = _head_out(o, gate_ref[:, cols], ng_ref[:, cols], w_ref[cols, :], center=True)
        mixed = part if mixed is None else mixed + part
        kd = (k.astype(F32) * jnp.exp((tc - 1.0 - row) * lg)).astype(BF16)
        s_scr[hd] = jnp.exp(jnp.full((1, 1), float(tc), F32) * lg) * s_old + _dot_tn(kd, v)
    out_ref[...] = x_ref[...] + g1_ref[...] * mixed

    @pl.when(c == n_chunks - 1)
    def _():
        s_out_ref[0] = s_scr[...]


def _core_specs(tc, nc, d, widths):
    row = lambda width: pl.BlockSpec((tc, width), lambda b, c: (b * nc + c, 0))
    g1 = pl.BlockSpec((None, 1, d), lambda b, c: (b, 0, 2))
    return [row(w) for w in widths], row(d), g1


def _ret_core(q, k, v, gate, x, mod, norm_g, w_out_bf16, log_gamma, batch, seq, n_heads, dk, dv):
    tc = min(RET_CHUNK, seq)
    nc = seq // tc
    assert seq % tc == 0
    n, d = x.shape
    nqk, nv = n_heads * dk, n_heads * dv
    ins, xrow, g1 = _core_specs(tc, nc, d, (nqk, nqk, nv, nv))
    return pl.pallas_call(
        functools.partial(_ret_core_kernel, tc=tc, n_chunks=nc, n_heads=n_heads, dk=dk, dv=dv),
        grid=(batch, nc),
        in_specs=[pl.BlockSpec(memory_space=pltpu.SMEM), *ins, xrow, g1,
                  _resident((1, nv)), _resident(w_out_bf16.shape)],
        out_specs=[xrow, pl.BlockSpec((1, n_heads, dk, dv), lambda b, c: (b, 0, 0, 0))],
        out_shape=[jax.ShapeDtypeStruct((n, d), F32),
                   jax.ShapeDtypeStruct((batch, n_heads, dk, dv), F32)],
        scratch_shapes=[pltpu.VMEM((n_heads, dk, dv), F32)],
        compiler_params=_cparams("parallel", "arbitrary"),
        name="ret_core",
    )(log_gamma, q, k, v, gate, x, mod, norm_g.reshape(1, nv), w_out_bf16)


def _step_kernel(s_ref, q_ref, k_ref, a_ref, v_ref, o_ref, s_out_ref, *, tb):
    for j in range(tb):
        s_new = (a_ref[0, 0, :, j:j + 1] * s_ref[j, 0]
                 + k_ref[0, 0, :, j:j + 1] * v_ref[j:j + 1, :])
        s_out_ref[j, 0] = s_new
        o_ref[j:j + 1, :] = jnp.sum(q_ref[0, 0, :, j:j + 1] * s_new, axis=0, keepdims=True)


def _columns(x, n_heads, dk, tb):
    n = x.shape[0]
    return x.astype(F32).reshape(n // tb, tb, n_heads, dk).transpose(2, 0, 3, 1)


def _step(states, layer, q, k, a, v, n_heads, dk, dv):
    n = states.shape[1]
    tb = STEP_TOKENS
    assert n % tb == 0
    col = pl.BlockSpec((1, 1, dk, tb), lambda i, h: (h, i, 0, 0))
    st_in = pl.BlockSpec((None, tb, 1, dk, dv), lambda i, h: (layer, i, h, 0, 0))
    st_out = pl.BlockSpec((tb, 1, dk, dv), lambda i, h: (i, h, 0, 0))
    row = pl.BlockSpec((tb, dv), lambda i, h: (i, h))
    return pl.pallas_call(
        functools.partial(_step_kernel, tb=tb),
        grid=(n // tb, n_heads),
        in_specs=[st_in, col, col, col, row],
        out_specs=[row, st_out],
        out_shape=[jax.ShapeDtypeStruct((n, n_heads * dv), F32),
                   jax.ShapeDtypeStruct(states.shape[1:], states.dtype)],
        compiler_params=_cparams("parallel", "parallel"),
        name="state_step",
    )(states, _columns(q, n_heads, dk, tb), _columns(k, n_heads, dk, tb),
      _columns(a, n_heads, dk, tb), v.astype(F32))


def _gla_proj_kernel(x_ref, sh_ref, sc_ref, g_ref, w_ref, a1_ref, a2_ref, ba_ref,
                     q_ref, k_ref, v_ref, r_ref, la_ref, *, nqk, nv):
    h = _modulate(x_ref[...], g_ref[...], sh_ref[...], sc_ref[...]).astype(BF16)
    q_ref[...] = _dot(h, w_ref[:, :nqk]) * ((nqk // GLA_HEADS) ** -0.5)
    k_ref[...] = _dot(h, w_ref[:, nqk:2 * nqk])
    v_ref[...] = _dot(h, w_ref[:, 2 * nqk:2 * nqk + nv]).astype(BF16)
    r_ref[...] = _dot(h, w_ref[:, 2 * nqk + nv:])
    z = _dot(_dot(h, a1_ref[...]).astype(BF16), a2_ref[...]) + ba_ref[...]
    la_ref[...] = (jnp.minimum(z, 0.0) - jnp.log1p(jnp.exp(-jnp.abs(z)))) / GLA_TAU


def _gla_proj(x, mod, rows, norm_g, w_in_bf16, a1_bf16, a2_bf16, b_a, nqk, nv):
    n, d = x.shape
    return pl.pallas_call(
        functools.partial(_gla_proj_kernel, nqk=nqk, nv=nv),
        grid=(rows.grid,),
        in_specs=[rows.rows(d), rows.mod(d, 0), rows.mod(d, 1), _resident((1, d)),
                  _resident(w_in_bf16.shape), _resident(a1_bf16.shape),
                  _resident(a2_bf16.shape), _resident((1, nqk))],
        out_specs=[rows.rows(nqk), rows.rows(nqk), rows.rows(nv), rows.rows(nv), rows.rows(nqk)],
        out_shape=[jax.ShapeDtypeStruct((n, nqk), F32), jax.ShapeDtypeStruct((n, nqk), F32),
                   jax.ShapeDtypeStruct((n, nv), BF16), jax.ShapeDtypeStruct((n, nv), F32),
                   jax.ShapeDtypeStruct((n, nqk), F32)],
        compiler_params=_cparams("parallel"),
        name="gla_proj",
    )(x, mod, mod, norm_g.reshape(1, d), w_in_bf16, a1_bf16, a2_bf16, b_a.reshape(1, nqk))


def _gla_core_kernel(q_ref, k_ref, la_ref, v_ref, o_ref, s_out_ref, s_scr,
                     *, tc, n_chunks, n_heads, dk, dv):
    c = pl.program_id(1)

    @pl.when(c == 0)
    def _():
        s_scr[...] = jnp.zeros_like(s_scr)

    ii = lax.broadcasted_iota(I32, (tc, tc), 0)
    jj = lax.broadcasted_iota(I32, (tc, tc), 1)
    la = la_ref[...]
    la_hi = la.astype(BF16)
    la_mid = (la - la_hi.astype(F32)).astype(BF16)
    la_lo = (la - la_hi.astype(F32) - la_mid.astype(F32)).astype(BF16)
    tri = jnp.where(ii >= jj, 1.0, 0.0).astype(BF16)
    b_all = _dot(tri, la_hi) + _dot(tri, la_mid) + _dot(tri, la_lo)
    ones = jnp.ones((tc, LANES), BF16)
    tot_col_all = (_dot_tn(la_hi, ones) + _dot_tn(la_mid, ones) + _dot_tn(la_lo, ones))[:, 0:1]
    qi = lax.broadcasted_iota(I32, (tc, GLA_KEY_BLOCK), 0)
    kj = lax.broadcasted_iota(I32, (tc, GLA_KEY_BLOCK), 1)
    for hd in range(n_heads):
        cols = slice(hd * dk, (hd + 1) * dk)
        b = b_all[:, cols]
        q, k, v = q_ref[:, cols], k_ref[:, cols], v_ref[:, hd * dv:(hd + 1) * dv]
        s_old = s_scr[hd]
        o = _dot((q * jnp.exp(b)).astype(BF16), s_old.astype(BF16))
        for s in range(tc // GLA_KEY_BLOCK):
            lo = s * GLA_KEY_BLOCK
            blk = slice(lo, lo + GLA_KEY_BLOCK)
            mid = lo + GLA_KEY_BLOCK // 2 - 1
            ref = b[mid:mid + 1, :]
            rel = b - ref
            if s:
                rel = jnp.where(qi[:, 0:1] < lo, 0.0, rel)
            qs = (q * jnp.exp(rel)).astype(BF16)
            ks = (k[blk, :] * jnp.exp(ref - b[blk, :])).astype(BF16)
            att = jnp.where(qi >= kj + lo, _dot_nt(qs, ks), 0.0).astype(BF16)
            o = o + _dot(att, v[blk, :])
        o_ref[:, hd * dv:(hd + 1) * dv] = o
        kd = (k * jnp.exp(b[tc - 1:tc, :] - b)).astype(BF16)
        s_scr[hd] = jnp.exp(tot_col_all[cols, :]) * s_old + _dot_tn(kd, v)

    @pl.when(c == n_chunks - 1)
    def _():
        s_out_ref[0] = s_scr[...]


def _gla_core(q, k, v, la, batch, seq, n_heads, dk, dv):
    tc = min(GLA_CHUNK, seq)
    nc = seq // tc
    assert seq % tc == 0
    n = batch * seq
    nqk, nv = n_heads * dk, n_heads * dv
    (qk, vv), _, _ = _core_specs(tc, nc, nv, (nqk, nv))
    return pl.pallas_call(
        functools.partial(_gla_core_kernel, tc=tc, n_chunks=nc, n_heads=n_heads, dk=dk, dv=dv),
        grid=(batch, nc),
        in_specs=[qk, qk, qk, vv],
        out_specs=[vv, pl.BlockSpec((1, n_heads, dk, dv), lambda b, c: (b, 0, 0, 0))],
        out_shape=[jax.ShapeDtypeStruct((n, nv), F32),
                   jax.ShapeDtypeStruct((batch, n_heads, dk, dv), F32)],
        scratch_shapes=[pltpu.VMEM((n_heads, dk, dv), F32)],
        compiler_params=_cparams("parallel", "arbitrary"),
        name="gla_core",
    )(q, k, la, v)


def _post_kernel(o_ref, gate_ref, x_ref, g1_ref, ng_ref, w_ref, out_ref, *, n_heads, dv, center):
    acc = None
    for hd in range(n_heads):
        sl = slice(hd * dv, (hd + 1) * dv)
        part = _head_out(o_ref[:, sl], gate_ref[:, sl], ng_ref[:, sl], w_ref[sl, :], center)
        acc = part if acc is None else acc + part
    out_ref[...] = x_ref[...] + g1_ref[...] * acc


def _post(o, gate, x, mod, rows, norm_g, w_out_bf16, n_heads, dv, center):
    n, d = x.shape
    nv = n_heads * dv
    return pl.pallas_call(
        functools.partial(_post_kernel, n_heads=n_heads, dv=dv, center=center),
        grid=(rows.grid,),
        in_specs=[rows.rows(nv), rows.rows(nv), rows.rows(d), rows.mod(d, 2),
                  _resident((1, nv)), _resident(w_out_bf16.shape)],
        out_specs=rows.rows(d),
        out_shape=jax.ShapeDtypeStruct((n, d), F32),
        compiler_params=_cparams("parallel"),
        name="mixer_post",
    )(o, gate, x, mod, norm_g.reshape(1, nv), w_out_bf16)


def _first_max(vals, idx_iota, n):
    m = jnp.max(vals, axis=0, keepdims=True)
    i = jnp.min(jnp.where(vals == m, idx_iota, n), axis=0, keepdims=True)
    return m, i


def _router_kernel(x_ref, sh_ref, sc_ref, g_ref, wr_ref, b_ref,
                   h2_ref, slot_ref, wt_ref, cnt_ref):
    h2 = _modulate(x_ref[...], g_ref[...], sh_ref[...], sc_ref[...])
    h2_ref[...] = h2.astype(BF16)
    h_hi = h2.astype(BF16)
    h_lo = (h2 - h_hi.astype(F32)).astype(BF16)
    wr = wr_ref[...]
    w_hi = wr.astype(BF16)
    w_lo = (wr - w_hi.astype(F32)).astype(BF16)
    logits = _dot_nt(w_hi, h_hi) + _dot_nt(w_hi, h_lo) + _dot_nt(w_lo, h_hi)
    scores = _sigmoid(logits)
    biased = scores + b_ref[...]
    n_exp, win = biased.shape
    gsz = n_exp // N_GROUPS
    neg = -jnp.inf

    sub_g = lax.broadcasted_iota(I32, (gsz, win), 0)
    group_rows = []
    for g in range(N_GROUPS):
        blk = biased[g * gsz:(g + 1) * gsz, :]
        m1, i1 = _first_max(blk, sub_g, gsz)
        m2 = jnp.max(jnp.where(sub_g == i1, neg, blk), axis=0, keepdims=True)
        group_rows.append(m1 + m2)
    cur = jnp.concatenate(group_rows, axis=0)
    sub_n = lax.broadcasted_iota(I32, (N_GROUPS, win), 0)
    gmask = jnp.zeros((N_GROUPS, win), jnp.bool_)
    for _ in range(TOPK_GROUPS):
        _, i = _first_max(cur, sub_n, N_GROUPS)
        sel = sub_n == i
        gmask = gmask | sel
        cur = jnp.where(sel, neg, cur)
    emask = jnp.concatenate(
        [jnp.broadcast_to(gmask[g:g + 1, :], (gsz, win)) for g in range(N_GROUPS)], axis=0)
    masked = jnp.where(emask, biased, neg)

    sub_e = lax.broadcasted_iota(I32, (n_exp, win), 0)
    sels, picked = [], []
    for _ in range(TOP_K):
        _, i = _first_max(masked, sub_e, n_exp)
        sel = sub_e == i
        sels.append(sel)
        picked.append(jnp.sum(jnp.where(sel, scores, 0.0), axis=0, keepdims=True))
        masked = jnp.where(sel, neg, masked)
    total = picked[0]
    for p in picked[1:]:
        total = total + p
    wt_ref[...] = jnp.concatenate([p / total * ROUTED_SCALE for p in picked], axis=0)

    chosen = sels[0]
    for s in sels[1:]:
        chosen = chosen | s
    sel_b = jnp.where(chosen, 1.0, 0.0).astype(BF16)
    r = lax.broadcasted_iota(I32, (win, win), 0)
    c = lax.broadcasted_iota(I32, (win, win), 1)
    before = jnp.where(r < c, 1.0, 0.0).astype(BF16)
    rank = _dot(sel_b, before)
    count = _dot(sel_b, jnp.ones((win, win), BF16))
    run_len = jnp.floor((count + (RUN_ALIGN - 1.0)) * (1.0 / RUN_ALIGN)) * RUN_ALIGN
    er = lax.broadcasted_iota(I32, (n_exp, n_exp), 0)
    ec = lax.broadcasted_iota(I32, (n_exp, n_exp), 1)
    lower = jnp.where(ec < er, 1.0, 0.0).astype(BF16)
    slot_all = _dot(lower, run_len.astype(BF16)) + rank
    slot_ref[...] = jnp.concatenate(
        [jnp.sum(jnp.where(s, slot_all, 0.0), axis=0, keepdims=True) for s in sels],
        axis=0).astype(I32)
    cnt_ref[0] = _dot_nt(jnp.ones((8, win), BF16), sel_b)


def _router(x, mod, rows, norm_g, w_router, bias):
    n, d = x.shape
    n_exp = w_router.shape[1]
    win = rows.tile
    lanes = pl.BlockSpec((TOP_K, win), lambda i: (0, i))
    return pl.pallas_call(
        _router_kernel,
        grid=(rows.grid,),
        in_specs=[rows.rows(d), rows.mod(d, 3), rows.mod(d, 4), _resident((1, d)),
                  _resident((n_exp, d)), _resident((n_exp, 1))],
        out_specs=[rows.rows(d), lanes, lanes, pl.BlockSpec((1, 8, n_exp), lambda i: (i, 0, 0))],
        out_shape=[jax.ShapeDtypeStruct((n, d), BF16), jax.ShapeDtypeStruct((TOP_K, n), I32),
                   jax.ShapeDtypeStruct((TOP_K, n), F32),
                   jax.ShapeDtypeStruct((rows.grid, 8, n_exp), F32)],
        compiler_params=_cparams("parallel"),
        name="moe_router",
    )(x, mod, mod, norm_g.reshape(1, d), w_router.T, bias.reshape(n_exp, 1))


def _sorted_rows(win, n_exp):
    return -(-(win * TOP_K + n_exp * (RUN_ALIGN - 1)) // SORT_BLOCK) * SORT_BLOCK


def _copy_list(count, src0, dst0, length, step):
    n_exp = count.shape[1]
    end = jnp.cumsum(count, axis=1)
    idx = jnp.arange(length, dtype=I32)
    owner = jnp.sum((end[:, None, :] <= idx[None, :, None]).astype(I32), axis=2)
    own = (owner[:, :, None] == jnp.arange(n_exp, dtype=I32)[None, None, :]).astype(I32)
    pick = lambda a: jnp.sum(own * a[:, None, :], axis=2)
    off = (idx[None, :] - pick(end - count)) * step
    return (pick(src0) + off).reshape(-1), (pick(dst0) + off).reshape(-1), end[:, -1]


def _plan(cnt, tile, sorted_rows):
    c = cnt[:, 0, :].astype(I32)
    run = (c + RUN_ALIGN - 1) // RUN_ALIGN * RUN_ALIGN
    per_exp = jnp.sum(run, axis=0)
    padded = (per_exp + tile - 1) // tile * tile
    pend = jnp.cumsum(padded)
    pstart = pend - padded
    dest = pstart[None, :] + jnp.cumsum(run, axis=0) - run
    local = jnp.cumsum(run, axis=1) - run
    n_double = run // (2 * RUN_ALIGN)
    covered = n_double * (2 * RUN_ALIGN)
    doubles = _copy_list(n_double, local, dest, sorted_rows // (2 * RUN_ALIGN), 2 * RUN_ALIGN)
    singles = _copy_list((run - covered) // RUN_ALIGN, local + covered, dest + covered,
                         c.shape[1], RUN_ALIGN)
    return dict(
        copies=doubles + singles, list_rows=sorted_rows,
        pad_start=pstart + per_exp, pad_chunks=(padded - per_exp) // RUN_ALIGN,
        pend=pend, used_tiles=(pend[-1] // tile).reshape(1))


def _capacity(n, n_win, n_exp, tile):
    worst = n * TOP_K + n_win * n_exp * (RUN_ALIGN - 1) + n_exp * (tile - RUN_ALIGN)
    return -(-worst // tile) * tile


def _window_copies(copy_refs, window, sorted_rows, n_exp, make_copy, wait):
    d_src, d_dst, d_n, s_src, s_dst, s_n = copy_refs
    for src_ref, dst_ref, n_ref, length, n_rows in (
            (d_src, d_dst, d_n, sorted_rows // (2 * RUN_ALIGN), 2 * RUN_ALIGN),
            (s_src, s_dst, s_n, n_exp, RUN_ALIGN)):
        def per_copy(i, carry, src_ref=src_ref, dst_ref=dst_ref, length=length, n_rows=n_rows):
            cp = make_copy(pl.multiple_of(src_ref[window * length + i], RUN_ALIGN),
                           pl.multiple_of(dst_ref[window * length + i], RUN_ALIGN), n_rows)
            if wait:
                cp.wait()
            else:
                cp.start()
            return carry

        lax.fori_loop(0, n_ref[window], per_copy, 0)


def _dispatch_kernel(d_src, d_dst, d_n, s_src, s_dst, s_n, pad_start_ref, pad_chunks_ref, used_ref,
                     *refs, n_exp, n_win, tile, groups, list_rows):
    xbuf_ref, xs_scr, zero_scr, sems, tail_sem = refs[-5:]
    w = pl.program_id(0)
    cur = w % 2
    n_tail = xbuf_ref.shape[0] // tile - used_ref[0]

    def tail_copy(j):
        dst = pl.multiple_of((used_ref[0] + j) * tile, tile)
        return pltpu.make_async_copy(zero_scr, xbuf_ref.at[pl.ds(dst, tile)], tail_sem)

    @pl.when(w == 0)
    def _():
        zero_scr[...] = jnp.zeros_like(zero_scr)
        lax.fori_loop(0, n_tail, lambda j, c: (tail_copy(j).start(), c)[1], 0)

    def sort_window(h2_ref, slot_ref, n_rows):
        slot = slot_ref[...]
        win = slot.shape[1]
        h2 = h2_ref[...]
        local = lax.broadcasted_iota(I32, (ONEHOT_BLOCK, win), 0).astype(F32).astype(BF16)
        slot_block = slot // ONEHOT_BLOCK
        slot_local = (slot % ONEHOT_BLOCK).astype(F32)
        one, zero = jnp.ones((), BF16), jnp.zeros((), BF16)

        def sort_block(blk):
            key = jnp.where(slot_block == blk, slot_local, -1.0).astype(BF16)
            onehot = jnp.where(local == key[0:1, :], one, zero)
            for k in range(1, TOP_K):
                onehot = jnp.where(local == key[k:k + 1, :], one, onehot)
            lo = blk * ONEHOT_BLOCK
            xs_scr[cur, lo:lo + ONEHOT_BLOCK, :] = _dot(onehot, h2).astype(BF16)

        used_rows = d_n[w] * (2 * RUN_ALIGN) + s_n[w] * RUN_ALIGN
        typical = win * TOP_K + n_exp * (RUN_ALIGN - 1) // 2
        for blk in range(n_rows // ONEHOT_BLOCK):
            if blk * ONEHOT_BLOCK < typical:
                sort_block(blk)
            else:
                pl.when(used_rows > blk * ONEHOT_BLOCK)(functools.partial(sort_block, blk))

    for gi, (first, count, n_rows) in enumerate(groups):
        @pl.when(jnp.logical_and(w >= first, w < first + count))
        def _(gi=gi, n_rows=n_rows):
            sort_window(refs[2 * gi], refs[2 * gi + 1], n_rows)

    def copies(window, buf, wait):
        def copy(src, dst, n):
            return pltpu.make_async_copy(xs_scr.at[buf, pl.ds(src, n)],
                                         xbuf_ref.at[pl.ds(dst, n)], sems.at[buf])
        _window_copies((d_src, d_dst, d_n, s_src, s_dst, s_n), window, list_rows, n_exp, copy, wait)

    copies(w, cur, wait=False)

    @pl.when(w > 0)
    def _():
        copies(w - 1, 1 - cur, wait=True)

    @pl.when(w == n_win - 1)
    def _():
        copies(w, cur, wait=True)

    def fill_pads_and_finish():
        sem = sems.at[cur]

        big = PAD_PIECE // RUN_ALIGN

        def pad_copy(row, n_rows):
            return pltpu.make_async_copy(zero_scr.at[pl.ds(0, n_rows)],
                                         xbuf_ref.at[pl.ds(pl.multiple_of(row, RUN_ALIGN), n_rows)],
                                         sem)

        def pads(wait):
            def go(cp):
                if wait:
                    cp.wait()
                else:
                    cp.start()

            def per_expert(e, carry):
                n_big = pad_chunks_ref[e] // big
                rest = pad_start_ref[e] + n_big * PAD_PIECE
                lax.fori_loop(0, n_big, lambda j, c: (go(pad_copy(
                    pad_start_ref[e] + j * PAD_PIECE, PAD_PIECE)), c)[1], 0)
                lax.fori_loop(0, pad_chunks_ref[e] - n_big * big, lambda j, c: (go(pad_copy(
                    rest + j * RUN_ALIGN, RUN_ALIGN)), c)[1], 0)
                return carry
            lax.fori_loop(0, n_exp, per_expert, 0)

        pads(False)
        pads(True)
        lax.fori_loop(0, n_tail, lambda j, c: (tail_copy(j).wait(), c)[1], 0)

    pl.when(w == n_win - 1)(fill_pads_and_finish)


def _dispatch(sources, plan, n_exp, cap, tile):
    d = sources[0][0].shape[1]
    in_specs, operands, groups, first = [], [], [], 0
    for h2, slot_t, rows in sources:
        idx = lambda i, first=first, last=rows.grid - 1: jnp.clip(i - first, 0, last)
        in_specs += [pl.BlockSpec((rows.tile, d), lambda i, *_, idx=idx: (idx(i), 0)),
                     pl.BlockSpec((TOP_K, rows.tile), lambda i, *_, idx=idx: (0, idx(i)))]
        operands += [h2, slot_t]
        groups.append((first, rows.grid, _sorted_rows(rows.tile, n_exp)))
        first += rows.grid
    return pl.pallas_call(
        functools.partial(_dispatch_kernel, n_exp=n_exp, n_win=first, tile=tile,
                          groups=tuple(groups), list_rows=plan["list_rows"]),
        grid_spec=pltpu.PrefetchScalarGridSpec(
            num_scalar_prefetch=9,
            grid=(first,),
            in_specs=in_specs,
            out_specs=pl.BlockSpec(memory_space=pl.ANY),
            scratch_shapes=[pltpu.VMEM((2, plan["list_rows"], d), BF16),
                            pltpu.VMEM((tile, d), BF16),
                            pltpu.SemaphoreType.DMA((2,)), pltpu.SemaphoreType.DMA(())]),
        out_shape=jax.ShapeDtypeStruct((cap, d), BF16),
        compiler_params=_cparams("arbitrary"),
        name="moe_dispatch",
    )(*plan["copies"], plan["pad_start"], plan["pad_chunks"], plan["used_tiles"], *operands)


def _experts_kernel(texp_ref, used_ref, xbuf_ref, wg_ref, wu_ref, wd_ref, ybuf_ref,
                    x_ring, sems, y_ring, y_sems, wgu_scr, wd_scr, *, sub_tiles):
    i = pl.program_id(0)
    used = used_ref[0]
    live = i < used
    de = wd_scr.shape[0]
    depth, tile = x_ring.shape[0], x_ring.shape[1]
    y_depth = y_ring.shape[0]

    def fetch(j):
        slot = j % depth
        return pltpu.make_async_copy(xbuf_ref.at[pl.ds(pl.multiple_of(j * tile, tile), tile)],
                                     x_ring.at[slot], sems.at[slot])

    def put(j):
        slot = j % y_depth
        return pltpu.make_async_copy(y_ring.at[slot],
                                     ybuf_ref.at[pl.ds(pl.multiple_of(j * tile, tile), tile)],
                                     y_sems.at[slot])

    @pl.when(i == 0)
    def _():
        for j in range(depth - 1):
            @pl.when(j < used)
            def _():
                fetch(j).start()

    @pl.when(i + (depth - 1) < used)
    def _():
        fetch(i + (depth - 1)).start()

    @pl.when(jnp.logical_or(i == 0, texp_ref[i] != texp_ref[jnp.maximum(i - 1, 0)]))
    def _():
        wgu_scr[:, :de] = wg_ref[...].astype(BF16)
        wgu_scr[:, de:] = wu_ref[...].astype(BF16)
        wd_scr[...] = wd_ref[...].astype(BF16)

    @pl.when(live)
    def _():
        fetch(i).wait()

        @pl.when(i >= y_depth)
        def _():
            put(i - y_depth).wait()

        x_ref, y_ref = x_ring.at[i % depth], y_ring.at[i % y_depth]
        sub = tile // sub_tiles
        for s in range(sub_tiles):
            rows = slice(s * sub, (s + 1) * sub)
            ab = _dot(x_ref[rows, :], wgu_scr[...])
            mid = (_silu(ab[:, :de]) * ab[:, de:]).astype(BF16)
            y_ref[rows, :] = _dot(mid, wd_scr[...]).astype(BF16)
        put(i).start()

    @pl.when(i == pl.num_programs(0) - 1)
    def _():
        for back in range(y_depth, 0, -1):
            @pl.when(used >= back)
            def _():
                put(used - back).wait()


def _experts(xbuf, plan, layer, w_gate, w_up, w_down, tile):
    cap, d = xbuf.shape
    _, n_exp, _, de = w_gate.shape
    n_tiles = cap // tile
    used = plan["used_tiles"]
    tmap = jnp.minimum(jnp.arange(n_tiles, dtype=I32), used[0] - 1)
    texp = jnp.sum((plan["pend"][None, :] <= (tmap * tile)[:, None]).astype(I32), axis=1)
    texp = jnp.minimum(texp, n_exp - 1)
    return pl.pallas_call(
        functools.partial(_experts_kernel, sub_tiles=max(tile // EXPERT_SUB_TILE, 1)),
        grid_spec=pltpu.PrefetchScalarGridSpec(
            num_scalar_prefetch=2,
            grid=(n_tiles,),
            in_specs=[pl.BlockSpec(memory_space=pl.ANY),
                      pl.BlockSpec((None, None, d, de), lambda i, te, u: (layer, te[i], 0, 0)),
                      pl.BlockSpec((None, None, d, de), lambda i, te, u: (layer, te[i], 0, 0)),
                      pl.BlockSpec((None, None, de, d), lambda i, te, u: (layer, te[i], 0, 0))],
            out_specs=pl.BlockSpec(memory_space=pl.ANY),
            scratch_shapes=[pltpu.VMEM((EXPERT_RING, tile, d), BF16),
                            pltpu.SemaphoreType.DMA((EXPERT_RING,)),
                            pltpu.VMEM((2, tile, d), BF16), pltpu.SemaphoreType.DMA((2,)),
                            pltpu.VMEM((d, 2 * de), BF16), pltpu.VMEM((de, d), BF16)]),
        out_shape=jax.ShapeDtypeStruct((cap, d), BF16),
        input_output_aliases={2: 0},
        compiler_params=_cparams("arbitrary"),
        name="moe_experts",
    )(texp, used, xbuf, w_gate, w_up, w_down)


def _combine_kernel(d_src, d_dst, d_n, s_src, s_dst, s_n, ybuf_ref, slot_ref, wt_ref, h2_ref,
                    x_ref, g2_ref, sg_ref, su_ref, sd_ref, fin_ref, out_ref, ys_scr, sems, acc_scr,
                    *, n_exp, n_win, final_norm, window_base, list_rows):
    w = pl.program_id(0)
    cur = w % 2
    n_rows = ys_scr.shape[1]

    def copies(window, buf, wait):
        def copy(loc, dst, n):
            return pltpu.make_async_copy(ybuf_ref.at[pl.ds(dst, n)],
                                         ys_scr.at[buf, pl.ds(loc, n)], sems.at[buf])
        _window_copies((d_src, d_dst, d_n, s_src, s_dst, s_n), window_base + window, list_rows,
                       n_exp, copy, wait)

    @pl.when(w == 0)
    def _():
        ys_scr[...] = jnp.zeros_like(ys_scr)
        copies(w, cur, wait=False)

    @pl.when(w + 1 < n_win)
    def _():
        copies(w + 1, 1 - cur, wait=False)

    h2 = h2_ref[...]
    shared = _dot((_silu(_dot(h2, sg_ref[...])) * _dot(h2, su_ref[...])).astype(BF16), sd_ref[...])
    copies(w, cur, wait=True)

    slot, wt = slot_ref[...], wt_ref[...].astype(BF16)
    win = slot.shape[1]
    local = lax.broadcasted_iota(I32, (ONEHOT_BLOCK, win), 0).astype(F32).astype(BF16)
    slot_block = slot // ONEHOT_BLOCK
    slot_local = (slot % ONEHOT_BLOCK).astype(F32)
    def block_sum(blk):
        key = jnp.where(slot_block == blk, slot_local, -1.0).astype(BF16)
        weights = jnp.where(local == key[0:1, :], wt[0:1, :], jnp.zeros((), BF16))
        for k in range(1, TOP_K):
            weights = jnp.where(local == key[k:k + 1, :], wt[k:k + 1, :], weights)
        lo = blk * ONEHOT_BLOCK
        return _dot_tn(weights, ys_scr[cur, lo:lo + ONEHOT_BLOCK, :])

    used_rows = d_n[window_base + w] * (2 * RUN_ALIGN) + s_n[window_base + w] * RUN_ALIGN
    typical = win * TOP_K + n_exp * (RUN_ALIGN - 1) // 2
    n_blocks = n_rows // ONEHOT_BLOCK
    always = [blk for blk in range(n_blocks) if blk * ONEHOT_BLOCK < typical]
    routed = shared
    for blk in always:
        routed = routed + block_sum(blk)
    if len(always) < n_blocks:
        acc_scr[...] = routed
        for blk in range(len(always), n_blocks):
            @pl.when(used_rows > blk * ONEHOT_BLOCK)
            def _(blk=blk):
                acc_scr[...] += block_sum(blk)
        routed = acc_scr[...]
    out = x_ref[...] + g2_ref[...] * routed
    if final_norm:
        out = out * lax.rsqrt(jnp.mean(out * out, axis=-1, keepdims=True) + EPS) * fin_ref[...]
    out_ref[...] = out


def _combine(ybuf, plan, slot, wt, h2, x, mod, rows, sh_bf16, final_g, n_exp, final_norm,
             window_base):
    n, d = x.shape
    win = rows.tile
    sg, su, sd = sh_bf16
    pairs = pl.BlockSpec((TOP_K, win), lambda i, *_: (0, i))
    return pl.pallas_call(
        functools.partial(_combine_kernel, n_exp=n_exp, n_win=rows.grid, final_norm=final_norm,
                          window_base=window_base, list_rows=plan["list_rows"]),
        grid_spec=pltpu.PrefetchScalarGridSpec(
            num_scalar_prefetch=6,
            grid=(rows.grid,),
            in_specs=[pl.BlockSpec(memory_space=pl.ANY), pairs, pairs, rows.rows(d), rows.rows(d),
                      rows.mod(d, 5), _resident(sg.shape), _resident(su.shape),
                      _resident(sd.shape), _resident((1, d))],
            out_specs=rows.rows(d),
            scratch_shapes=[pltpu.VMEM((2, _sorted_rows(win, n_exp), d), BF16),
                            pltpu.SemaphoreType.DMA((2,)), pltpu.VMEM((win, d), F32)]),
        out_shape=jax.ShapeDtypeStruct((n, d), F32),
        compiler_params=_cparams("arbitrary"),
        name="moe_combine",
    )(*plan["copies"], ybuf, slot, wt, h2, x, mod, sg, su, sd, final_g.reshape(1, d))


def _moe(groups, layer, norm_g, w_router, bias, w_gate, w_up, w_down, sh_bf16, final_g,
         final_norm, tile):
    n_exp = w_router.shape[1]
    routed = [_router(x, mod, rows, norm_g, w_router, bias) for x, mod, rows in groups]
    bases, total = [], 0
    for _, _, rows in groups:
        bases.append(total)
        total += rows.grid
    plan = _plan(jnp.concatenate([r[3] for r in routed], axis=0), tile,
                 max(_sorted_rows(rows.tile, n_exp) for _, _, rows in groups))
    cap = _capacity(sum(x.shape[0] for x, _, _ in groups), total, n_exp, tile)
    xbuf = _dispatch([(r[0], r[1], g[2]) for r, g in zip(routed, groups)], plan, n_exp, cap, tile)
    ybuf = _experts(xbuf, plan, layer, w_gate, w_up, w_down, tile)
    return [_combine(ybuf, plan, slot_t, wt_t, h2, x, mod, rows, sh_bf16, final_g, n_exp,
                     final_norm, base)
            for (x, mod, rows), (h2, slot_t, wt_t, _), base in zip(groups, routed, bases)]


def _rope_tables(pos, half):
    inv = ROPE_BASE ** (-jnp.arange(half, dtype=F32) / half)
    ang = pos.astype(F32)[:, None] * inv[None, :]
    return jnp.cos(ang), jnp.sin(ang)


class _Group:
    def __init__(self, x3, mod_all, pos0, s_ret, s_gla):
        self.batch, self.seq, d = x3.shape
        self.n = self.batch * self.seq
        self.x = x3.reshape(self.n, d)
        self.rows = _Rows(self.n, self.seq, min(ROW_TILE, self.n))
        self.decode = self.seq == 1
        self.mod_all, self.pos0, self.s_ret, self.s_gla = mod_all, pos0, s_ret, s_gla
        self.new_ret, self.new_gla = [], []

    def mod(self, layer):
        m = self.mod_all[layer]
        return m if self.decode else m.reshape(self.batch, 1, m.shape[-1])


def _mixer(g, layer, wts):
    d = g.x.shape[1]
    x, mod, rows = g.x, g.mod(layer), g.rows
    ret_dk = d // RET_HEADS
    ret_dv = 2 * ret_dk
    gla_dk = d // (2 * GLA_HEADS)
    gla_dv = d // GLA_HEADS
    j = layer // 2
    if layer % 2 == 0:
        pos = (jnp.full((rows.tile,), g.pos0, I32) if g.decode
               else g.pos0 + jnp.arange(g.seq, dtype=I32))
        cos, sin = _rope_tables(pos, ret_dk // 2)
        q, k, v, gate = _ret_proj(x, mod, rows, wts["norm_mix_g"][layer], wts["ret_w_in"][j],
                                  cos, sin, RET_HEADS, ret_dk, ret_dv)
        log_gamma = jnp.log1p(-jnp.power(2.0, -5.0 - jnp.arange(RET_HEADS, dtype=F32)))
        if g.decode:
            decay = jnp.broadcast_to(jnp.repeat(jnp.exp(log_gamma), ret_dk)[None, :],
                                     (g.n, RET_HEADS * ret_dk))
            o, s_new = _step(g.s_ret, j, q, k, decay, v, RET_HEADS, ret_dk, ret_dv)
            out = _post(o, gate, x, mod, rows, wts["ret_norm_g"][j], wts["ret_w_out"][j],
                        RET_HEADS, ret_dv, center=True)
        else:
            out, s_new = _ret_core(q, k, v, gate, x, mod, wts["ret_norm_g"][j],
                                   wts["ret_w_out"][j], log_gamma, g.batch, g.seq,
                                   RET_HEADS, ret_dk, ret_dv)
        g.new_ret.append(s_new)
        return out
    q, k, v, gate, la = _gla_proj(x, mod, rows, wts["norm_mix_g"][layer], wts["gla_w_in"][j],
                                  wts["gla_w_a1"][j], wts["gla_w_a2"][j], wts["gla_b_a"][j],
                                  GLA_HEADS * gla_dk, GLA_HEADS * gla_dv)
    if g.decode:
        o, s_new = _step(g.s_gla, j, q, k, jnp.exp(la), v, GLA_HEADS, gla_dk, gla_dv)
    else:
        o, s_new = _gla_core(q, k, v, la, g.batch, g.seq, GLA_HEADS, gla_dk, gla_dv)
    g.new_gla.append(s_new)
    return _post(o, gate, x, mod, rows, wts["gla_norm_g"][j], wts["gla_w_out"][j],
                 GLA_HEADS, gla_dv, center=False)


def _trunk(groups, wts):
    depth = wts["ada_w"].shape[0]
    for layer in range(depth):
        mixed = [(_mixer(g, layer, wts), g.mod(layer), g.rows) for g in groups]
        outs = _moe(mixed, layer, wts["norm_ffn_g"][layer], wts["moe_w_router"][layer],
                    wts["moe_router_bias"][layer], wts["moe_w_gate"], wts["moe_w_up"],
                    wts["moe_w_down"], wts["shared"][layer], wts["final_norm_g"],
                    final_norm=layer == depth - 1, tile=EXPERT_TILE)
        for g, x in zip(groups, outs):
            g.x = x
    return [(g.x.reshape(g.batch, g.seq, -1), jnp.stack(g.new_ret), jnp.stack(g.new_gla))
            for g in groups]


def kernel(x_prompt, x_sample, state_ret, state_gla, c_prompt, c_sample, ret_w_in, ret_norm_g, ret_w_out, gla_w_in, gla_w_a1, gla_w_a2, gla_b_a, gla_norm_g, gla_w_out, ada_w, ada_b, norm_mix_g, norm_ffn_g, moe_w_router, moe_router_bias, moe_w_gate, moe_w_up, moe_w_down, sh_w_gate, sh_w_up, sh_w_down, final_norm_g):
    b = x_prompt.shape[0]
    depth = ada_w.shape[0]
    rank = gla_w_a1.shape[-1]
    pad = LANES - rank
    wts = dict(
        ret_w_in=ret_w_in.astype(BF16), ret_norm_g=ret_norm_g, ret_w_out=ret_w_out.astype(BF16),
        gla_w_in=gla_w_in.astype(BF16),
        gla_w_a1=jnp.pad(gla_w_a1, ((0, 0), (0, 0), (0, pad))).astype(BF16),
        gla_w_a2=jnp.pad(gla_w_a2, ((0, 0), (0, pad), (0, 0))).astype(BF16),
        gla_b_a=gla_b_a, gla_norm_g=gla_norm_g, gla_w_out=gla_w_out.astype(BF16),
        ada_w=ada_w, norm_mix_g=norm_mix_g, norm_ffn_g=norm_ffn_g,
        moe_w_router=moe_w_router, moe_router_bias=moe_router_bias,
        moe_w_gate=moe_w_gate, moe_w_up=moe_w_up, moe_w_down=moe_w_down,
        shared=[(sh_w_gate[l].astype(BF16), sh_w_up[l].astype(BF16), sh_w_down[l].astype(BF16))
                for l in range(depth)],
        final_norm_g=final_norm_g)
    mod = _ada(jnp.concatenate([c_prompt, c_sample], axis=0), ada_w, ada_b)
    (y_p, ret_p, gla_p), (y_s, ret_s, gla_s) = _trunk(
        [_Group(x_prompt, mod[:, :b], 0, None, None),
         _Group(x_sample, mod[:, b:], PAST_LEN, state_ret, state_gla)], wts)
    return (y_p, y_s, ret_p, gla_p, ret_s, gla_s)
```

```python
import functools

import jax
import jax.numpy as jnp
from jax import lax
from jax.experimental import pallas as pl
from jax.experimental.pallas import tpu as pltpu

F32, BF16, I32 = jnp.float32, jnp.bfloat16, jnp.int32

EPS = 1e-6
ROPE_BASE = 10000.0
PAST_LEN = 16384
RET_HEADS = 4
GLA_HEADS = 4
GLA_TAU = 16.0
N_GROUPS = 8
TOPK_GROUPS = 4
TOP_K = 8
ROUTED_SCALE = 2.5

LANES = 128
BF16_SUBLANES = 16
VMEM_LIMIT_BYTES = 56 * 1024 * 1024

ROW_TILE = 256
RET_CHUNK = 256
GLA_CHUNK = 128
GLA_KEY_BLOCK = 32
STEP_TOKENS = 8
RUN_ALIGN = BF16_SUBLANES
EXPERT_TILE = 1024
EXPERT_SUB_TILE = 1024
SORT_BLOCK = 512
EXPERT_RING = 3
ONEHOT_BLOCK = 256
PAD_PIECE = 256
COPY_UNITS = (4, 3, 2, 1)


def _cparams(*sem):
    return pltpu.CompilerParams(dimension_semantics=sem, vmem_limit_bytes=VMEM_LIMIT_BYTES)


def _sigmoid(x):
    return 1.0 / (1.0 + jnp.exp(-x))


def _silu(x):
    return x * _sigmoid(x)


def _modulate(x, g, shift, scale):
    y = x * lax.rsqrt(jnp.mean(x * x, axis=-1, keepdims=True) + EPS) * g
    return y * (1.0 + scale) + shift


def _dot(a, b):
    return jnp.dot(a, b, preferred_element_type=F32)


def _dot_nt(a, b):
    return lax.dot_general(a, b, (((1,), (1,)), ((), ())), preferred_element_type=F32)


def _dot_tn(a, b):
    return lax.dot_general(a, b, (((0,), (0,)), ((), ())), preferred_element_type=F32)


def _resident(shape):
    zeros = (0,) * len(shape)
    return pl.BlockSpec(shape, lambda *_: zeros, pipeline_mode=pl.Buffered(1))


class _Rows:
    def __init__(self, n_rows, seq_len, tile):
        self.n, self.tile = n_rows, tile
        self.per_row = seq_len == 1
        self.tiles_per_seq = max(seq_len // tile, 1)
        assert n_rows % tile == 0 and (self.per_row or seq_len % tile == 0)
        self.grid = n_rows // tile

    def rows(self, width, col=0):
        return pl.BlockSpec((self.tile, width), lambda i, *_: (i, col))

    def mod(self, d, col):
        if self.per_row:
            return pl.BlockSpec((self.tile, d), lambda i, *_: (i, col))
        tps = self.tiles_per_seq
        return pl.BlockSpec((None, 1, d), lambda i, *_: (i // tps, 0, col))


def _ada_kernel(c_ref, w_ref, b_ref, o_ref):
    s = _silu(c_ref[...]).astype(BF16)
    o_ref[0] = _dot(s, w_ref[0].astype(BF16)) + b_ref[0]


def _ada(c_all, ada_w, ada_b):
    depth, d, d6 = ada_w.shape
    n = c_all.shape[0]
    tn = d6 // 4
    return pl.pallas_call(
        _ada_kernel,
        grid=(depth, d6 // tn),
        in_specs=[pl.BlockSpec((n, d), lambda l, j: (0, 0)),
                  pl.BlockSpec((1, d, tn), lambda l, j: (l, 0, j)),
                  pl.BlockSpec((1, 1, tn), lambda l, j: (l, 0, j))],
        out_specs=pl.BlockSpec((1, n, tn), lambda l, j: (l, 0, j)),
        out_shape=jax.ShapeDtypeStruct((depth, n, d6), F32),
        compiler_params=_cparams("parallel", "parallel"),
        name="ada_mod",
    )(c_all, ada_w, ada_b.reshape(depth, 1, d6))


def _ret_proj_kernel(x_ref, sh_ref, sc_ref, g_ref, w_ref, cos_ref, sin_ref,
                     q_ref, k_ref, v_ref, gate_ref, *, n_heads, dk, dv):
    h = _modulate(x_ref[...], g_ref[...], sh_ref[...], sc_ref[...]).astype(BF16)
    cos, sin = cos_ref[...], sin_ref[...]
    half, nqk, nv = dk // 2, n_heads * dk, n_heads * dv
    for hd in range(n_heads):
        for dst, base, scale in ((q_ref, 0, None), (k_ref, nqk, dk ** -0.5)):
            p = _dot(h, w_ref[:, base + hd * dk:base + (hd + 1) * dk])
            x1, x2 = p[:, :half], p[:, half:]
            r1, r2 = x1 * cos - x2 * sin, x1 * sin + x2 * cos
            if scale is not None:
                r1, r2 = r1 * scale, r2 * scale
            dst[:, hd * dk:hd * dk + half] = r1.astype(BF16)
            dst[:, hd * dk + half:(hd + 1) * dk] = r2.astype(BF16)
    for hd in range(n_heads):
        v_ref[:, hd * dv:(hd + 1) * dv] = _dot(
            h, w_ref[:, 2 * nqk + hd * dv:2 * nqk + (hd + 1) * dv]).astype(BF16)
        gate_ref[:, hd * dv:(hd + 1) * dv] = _dot(
            h, w_ref[:, 2 * nqk + nv + hd * dv:2 * nqk + nv + (hd + 1) * dv])


def _ret_proj(x, mod, rows, norm_g, w_in_bf16, cos, sin, n_heads, dk, dv):
    n, d = x.shape
    nqk, nv = n_heads * dk, n_heads * dv
    half = dk // 2
    tps = rows.tiles_per_seq
    trig = (pl.BlockSpec((rows.tile, half), lambda i: (0, 0)) if rows.per_row
            else pl.BlockSpec((rows.tile, half), lambda i: (i % tps, 0)))
    return pl.pallas_call(
        functools.partial(_ret_proj_kernel, n_heads=n_heads, dk=dk, dv=dv),
        grid=(rows.grid,),
        in_specs=[rows.rows(d), rows.mod(d, 0), rows.mod(d, 1), _resident((1, d)),
                  _resident(w_in_bf16.shape), trig, trig],
        out_specs=[rows.rows(nqk), rows.rows(nqk), rows.rows(nv), rows.rows(nv)],
        out_shape=[jax.ShapeDtypeStruct((n, nqk), BF16), jax.ShapeDtypeStruct((n, nqk), BF16),
                   jax.ShapeDtypeStruct((n, nv), BF16), jax.ShapeDtypeStruct((n, nv), F32)],
        compiler_params=_cparams("parallel"),
        name="ret_proj",
    )(x, mod, mod, norm_g.reshape(1, d), w_in_bf16, cos, sin)


def _head_out(o, gate, norm_g, w_out, center):
    if center:
        o = o - jnp.mean(o, axis=-1, keepdims=True)
    y = o * lax.rsqrt(jnp.mean(o * o, axis=-1, keepdims=True) + EPS) * norm_g
    return _dot((y * _silu(gate)).astype(BF16), w_out)


def _ret_core_kernel(lg_ref, q_ref, k_ref, v_ref, gate_ref, x_ref, g1_ref, ng_ref, w_ref,
                     out_ref, s_out_ref, s_scr, *, tc, n_chunks, n_heads, dk, dv):
    c = pl.program_id(1)

    @pl.when(c == 0)
    def _():
        s_scr[...] = jnp.zeros_like(s_scr)

    ii = lax.broadcasted_iota(I32, (tc, tc), 0)
    jj = lax.broadcasted_iota(I32, (tc, tc), 1)
    causal = ii >= jj
    lag = (ii - jj).astype(F32)
    row = lax.broadcasted_iota(I32, (tc, 1), 0).astype(F32)
    mixed = None
    for hd in range(n_heads):
        lg = lg_ref[hd]
        q, k = q_ref[:, hd * dk:(hd + 1) * dk], k_ref[:, hd * dk:(hd + 1) * dk]
        cols = slice(hd * dv, (hd + 1) * dv)
        v = v_ref[:, cols]
        dec = jnp.where(causal, jnp.exp(lag * lg), 0.0)
        att = (_dot_nt(q, k) * dec).astype(BF16)
        s_old = s_scr[hd]
        o = _dot(att, v) + jnp.exp((row + 1.0) * lg) * _dot(q, s_old.astype(BF16))
        part = _head_out(o, gate_ref[:, cols], ng_ref[:, cols], w_ref[cols, :], center=True)
        mixed = part if mixed is None else mixed + part
        kd = (k.astype(F32) * jnp.exp((tc - 1.0 - row) * lg)).astype(BF16)
        s_scr[hd] = jnp.exp(jnp.full((1, 1), float(tc), F32) * lg) * s_old + _dot_tn(kd, v)
    out_ref[...] = x_ref[...] + g1_ref[...] * mixed

    @pl.when(c == n_chunks - 1)
    def _():
        s_out_ref[0] = s_scr[...]


def _core_specs(tc, nc, d, widths):
    row = lambda width: pl.BlockSpec((tc, width), lambda b, c: (b * nc + c, 0))
    g1 = pl.BlockSpec((None, 1, d), lambda b, c: (b, 0, 2))
    return [row(w) for w in widths], row(d), g1


def _ret_core(q, k, v, gate, x, mod, norm_g, w_out_bf16, log_gamma, batch, seq, n_heads, dk, dv):
    tc = min(RET_CHUNK, seq)
    nc = seq // tc
    assert seq % tc == 0
    n, d = x.shape
    nqk, nv = n_heads * dk, n_heads * dv
    ins, xrow, g1 = _core_specs(tc, nc, d, (nqk, nqk, nv, nv))
    return pl.pallas_call(
        functools.partial(_ret_core_kernel, tc=tc, n_chunks=nc, n_heads=n_heads, dk=dk, dv=dv),
        grid=(batch, nc),
        in_specs=[pl.BlockSpec(memory_space=pltpu.SMEM), *ins, xrow, g1,
                  _resident((1, nv)), _resident(w_out_bf16.shape)],
        out_specs=[xrow, pl.BlockSpec((1, n_heads, dk, dv), lambda b, c: (b, 0, 0, 0))],
        out_shape=[jax.ShapeDtypeStruct((n, d), F32),
                   jax.ShapeDtypeStruct((batch, n_heads, dk, dv), F32)],
        scratch_shapes=[pltpu.VMEM((n_heads, dk, dv), F32)],
        compiler_params=_cparams("parallel", "arbitrary"),
        name="ret_core",
    )(log_gamma, q, k, v, gate, x, mod, norm_g.reshape(1, nv), w_out_bf16)


def _step_kernel(s_ref, q_ref, k_ref, a_ref, v_ref, o_ref, s_out_ref, *, tb):
    for j in range(tb):
        s_new = (a_ref[0, 0, :, j:j + 1] * s_ref[j, 0]
                 + k_ref[0, 0, :, j:j + 1] * v_ref[j:j + 1, :])
        s_out_ref[j, 0] = s_new
        o_ref[j:j + 1, :] = jnp.sum(q_ref[0, 0, :, j:j + 1] * s_new, axis=0, keepdims=True)


def _columns(x, n_heads, dk, tb):
    n = x.shape[0]
    return x.astype(F32).reshape(n // tb, tb, n_heads, dk).transpose(2, 0, 3, 1)


def _step(states, layer, q, k, a, v, n_heads, dk, dv):
    n = states.shape[1]
    tb = STEP_TOKENS
    assert n % tb == 0
    col = pl.BlockSpec((1, 1, dk, tb), lambda i, h: (h, i, 0, 0))
    st_in = pl.BlockSpec((None, tb, 1, dk, dv), lambda i, h: (layer, i, h, 0, 0))
    st_out = pl.BlockSpec((tb, 1, dk, dv), lambda i, h: (i, h, 0, 0))
    row = pl.BlockSpec((tb, dv), lambda i, h: (i, h))
    return pl.pallas_call(
        functools.partial(_step_kernel, tb=tb),
        grid=(n // tb, n_heads),
        in_specs=[st_in, col, col, col, row],
        out_specs=[row, st_out],
        out_shape=[jax.ShapeDtypeStruct((n, n_heads * dv), F32),
                   jax.ShapeDtypeStruct(states.shape[1:], states.dtype)],
        compiler_params=_cparams("parallel", "parallel"),
        name="state_step",
    )(states, _columns(q, n_heads, dk, tb), _columns(k, n_heads, dk, tb),
      _columns(a, n_heads, dk, tb), v.astype(F32))


def _gla_proj_kernel(x_ref, sh_ref, sc_ref, g_ref, w_ref, a1_ref, a2_ref, ba_ref,
                     q_ref, k_ref, v_ref, r_ref, la_ref, *, nqk, nv):
    h = _modulate(x_ref[...], g_ref[...], sh_ref[...], sc_ref[...]).astype(BF16)
    q_ref[...] = _dot(h, w_ref[:, :nqk]) * ((nqk // GLA_HEADS) ** -0.5)
    k_ref[...] = _dot(h, w_ref[:, nqk:2 * nqk])
    v_ref[...] = _dot(h, w_ref[:, 2 * nqk:2 * nqk + nv]).astype(BF16)
    r_ref[...] = _dot(h, w_ref[:, 2 * nqk + nv:])
    z = _dot(_dot(h, a1_ref[...]).astype(BF16), a2_ref[...]) + ba_ref[...]
    la_ref[...] = (jnp.minimum(z, 0.0) - jnp.log1p(jnp.exp(-jnp.abs(z)))) / GLA_TAU


def _gla_proj(x, mod, rows, norm_g, w_in_bf16, a1_bf16, a2_bf16, b_a, nqk, nv):
    n, d = x.shape
    return pl.pallas_call(
        functools.partial(_gla_proj_kernel, nqk=nqk, nv=nv),
        grid=(rows.grid,),
        in_specs=[rows.rows(d), rows.mod(d, 0), rows.mod(d, 1), _resident((1, d)),
                  _resident(w_in_bf16.shape), _resident(a1_bf16.shape),
                  _resident(a2_bf16.shape), _resident((1, nqk))],
        out_specs=[rows.rows(nqk), rows.rows(nqk), rows.rows(nv), rows.rows(nv), rows.rows(nqk)],
        out_shape=[jax.ShapeDtypeStruct((n, nqk), F32), jax.ShapeDtypeStruct((n, nqk), F32),
                   jax.ShapeDtypeStruct((n, nv), BF16), jax.ShapeDtypeStruct((n, nv), F32),
                   jax.ShapeDtypeStruct((n, nqk), F32)],
        compiler_params=_cparams("parallel"),
        name="gla_proj",
    )(x, mod, mod, norm_g.reshape(1, d), w_in_bf16, a1_bf16, a2_bf16, b_a.reshape(1, nqk))


def _gla_core_kernel(q_ref, k_ref, la_ref, v_ref, o_ref, s_out_ref, s_scr,
                     *, tc, n_chunks, n_heads, dk, dv):
    c = pl.program_id(1)

    @pl.when(c == 0)
    def _():
        s_scr[...] = jnp.zeros_like(s_scr)

    ii = lax.broadcasted_iota(I32, (tc, tc), 0)
    jj = lax.broadcasted_iota(I32, (tc, tc), 1)
    la = la_ref[...]
    la_hi = la.astype(BF16)
    la_mid = (la - la_hi.astype(F32)).astype(BF16)
    la_lo = (la - la_hi.astype(F32) - la_mid.astype(F32)).astype(BF16)
    tri = jnp.where(ii >= jj, 1.0, 0.0).astype(BF16)
    b_all = _dot(tri, la_hi) + _dot(tri, la_mid) + _dot(tri, la_lo)
    ones = jnp.ones((tc, LANES), BF16)
    tot_col_all = (_dot_tn(la_hi, ones) + _dot_tn(la_mid, ones) + _dot_tn(la_lo, ones))[:, 0:1]
    qi = lax.broadcasted_iota(I32, (tc, GLA_KEY_BLOCK), 0)
    kj = lax.broadcasted_iota(I32, (tc, GLA_KEY_BLOCK), 1)
    for hd in range(n_heads):
        cols = slice(hd * dk, (hd + 1) * dk)
        b = b_all[:, cols]
        q, k, v = q_ref[:, cols], k_ref[:, cols], v_ref[:, hd * dv:(hd + 1) * dv]
        s_old = s_scr[hd]
        o = _dot((q * jnp.exp(b)).astype(BF16), s_old.astype(BF16))
        for s in range(tc // GLA_KEY_BLOCK):
            lo = s * GLA_KEY_BLOCK
            blk = slice(lo, lo + GLA_KEY_BLOCK)
            mid = lo + GLA_KEY_BLOCK // 2 - 1
            ref = b[mid:mid + 1, :]
            rel = b - ref
            if s:
                rel = jnp.where(qi[:, 0:1] < lo, 0.0, rel)
            qs = (q * jnp.exp(rel)).astype(BF16)
            ks = (k[blk, :] * jnp.exp(ref - b[blk, :])).astype(BF16)
            att = jnp.where(qi >= kj + lo, _dot_nt(qs, ks), 0.0).astype(BF16)
            o = o + _dot(att, v[blk, :])
        o_ref[:, hd * dv:(hd + 1) * dv] = o
        kd = (k * jnp.exp(b[tc - 1:tc, :] - b)).astype(BF16)
        s_scr[hd] = jnp.exp(tot_col_all[cols, :]) * s_old + _dot_tn(kd, v)

    @pl.when(c == n_chunks - 1)
    def _():
        s_out_ref[0] = s_scr[...]


def _gla_core(q, k, v, la, batch, seq, n_heads, dk, dv):
    tc = min(GLA_CHUNK, seq)
    nc = seq // tc
    assert seq % tc == 0
    n = batch * seq
    nqk, nv = n_heads * dk, n_heads * dv
    (qk, vv), _, _ = _core_specs(tc, nc, nv, (nqk, nv))
    return pl.pallas_call(
        functools.partial(_gla_core_kernel, tc=tc, n_chunks=nc, n_heads=n_heads, dk=dk, dv=dv),
        grid=(batch, nc),
        in_specs=[qk, qk, qk, vv],
        out_specs=[vv, pl.BlockSpec((1, n_heads, dk, dv), lambda b, c: (b, 0, 0, 0))],
        out_shape=[jax.ShapeDtypeStruct((n, nv), F32),
                   jax.ShapeDtypeStruct((batch, n_heads, dk, dv), F32)],
        scratch_shapes=[pltpu.VMEM((n_heads, dk, dv), F32)],
        compiler_params=_cparams("parallel", "arbitrary"),
        name="gla_core",
    )(q, k, la, v)


def _post_kernel(o_ref, gate_ref, x_ref, g1_ref, ng_ref, w_ref, out_ref, *, n_heads, dv, center):
    acc = None
    for hd in range(n_heads):
        sl = slice(hd * dv, (hd + 1) * dv)
        part = _head_out(o_ref[:, sl], gate_ref[:, sl], ng_ref[:, sl], w_ref[sl, :], center)
        acc = part if acc is None else acc + part
    out_ref[...] = x_ref[...] + g1_ref[...] * acc


def _post(o, gate, x, mod, rows, norm_g, w_out_bf16, n_heads, dv, center):
    n, d = x.shape
    nv = n_heads * dv
    return pl.pallas_call(
        functools.partial(_post_kernel, n_heads=n_heads, dv=dv, center=center),
        grid=(rows.grid,),
        in_specs=[rows.rows(nv), rows.rows(nv), rows.rows(d), rows.mod(d, 2),
                  _resident((1, nv)), _resident(w_out_bf16.shape)],
        out_specs=rows.rows(d),
        out_shape=jax.ShapeDtypeStruct((n, d), F32),
        compiler_params=_cparams("parallel"),
        name="mixer_post",
    )(o, gate, x, mod, norm_g.reshape(1, nv), w_out_bf16)


def _first_max(vals, idx_iota, n):
    m = jnp.max(vals, axis=0, keepdims=True)
    i = jnp.min(jnp.where(vals == m, idx_iota, n), axis=0, keepdims=True)
    return m, i


def _router_kernel(x_ref, sh_ref, sc_ref, g_ref, wr_ref, b_ref,
                   h2_ref, slot_ref, wt_ref, cnt_ref):
    h2 = _modulate(x_ref[...], g_ref[...], sh_ref[...], sc_ref[...])
    h2_ref[...] = h2.astype(BF16)
    h_hi = h2.astype(BF16)
    h_lo = (h2 - h_hi.astype(F32)).astype(BF16)
    wr = wr_ref[...]
    w_hi = wr.astype(BF16)
    w_lo = (wr - w_hi.astype(F32)).astype(BF16)
    logits = _dot_nt(w_hi, h_hi) + _dot_nt(w_hi, h_lo) + _dot_nt(w_lo, h_hi)
    scores = _sigmoid(logits)
    biased = scores + b_ref[...]
    n_exp, win = biased.shape
    gsz = n_exp // N_GROUPS
    neg = -jnp.inf

    sub_g = lax.broadcasted_iota(I32, (gsz, win), 0)
    group_rows = []
    for g in range(N_GROUPS):
        blk = biased[g * gsz:(g + 1) * gsz, :]
        m1, i1 = _first_max(blk, sub_g, gsz)
        m2 = jnp.max(jnp.where(sub_g == i1, neg, blk), axis=0, keepdims=True)
        group_rows.append(m1 + m2)
    cur = jnp.concatenate(group_rows, axis=0)
    sub_n = lax.broadcasted_iota(I32, (N_GROUPS, win), 0)
    gmask = jnp.zeros((N_GROUPS, win), jnp.bool_)
    for _ in range(TOPK_GROUPS):
        _, i = _first_max(cur, sub_n, N_GROUPS)
        sel = sub_n == i
        gmask = gmask | sel
        cur = jnp.where(sel, neg, cur)
    emask = jnp.concatenate(
        [jnp.broadcast_to(gmask[g:g + 1, :], (gsz, win)) for g in range(N_GROUPS)], axis=0)
    masked = jnp.where(emask, biased, neg)

    sub_e = lax.broadcasted_iota(I32, (n_exp, win), 0)
    sels, picked = [], []
    for _ in range(TOP_K):
        _, i = _first_max(masked, sub_e, n_exp)
        sel = sub_e == i
        sels.append(sel)
        picked.append(jnp.sum(jnp.where(sel, scores, 0.0), axis=0, keepdims=True))
        masked = jnp.where(sel, neg, masked)
    total = picked[0]
    for p in picked[1:]:
        total = total + p
    wt_ref[...] = jnp.concatenate([p / total * ROUTED_SCALE for p in picked], axis=0)

    chosen = sels[0]
    for s in sels[1:]:
        chosen = chosen | s
    sel_b = jnp.where(chosen, 1.0, 0.0).astype(BF16)
    r = lax.broadcasted_iota(I32, (win, win), 0)
    c = lax.broadcasted_iota(I32, (win, win), 1)
    before = jnp.where(r < c, 1.0, 0.0).astype(BF16)
    rank = _dot(sel_b, before)
    count = _dot(sel_b, jnp.ones((win, win), BF16))
    run_len = jnp.floor((count + (RUN_ALIGN - 1.0)) * (1.0 / RUN_ALIGN)) * RUN_ALIGN
    er = lax.broadcasted_iota(I32, (n_exp, n_exp), 0)
    ec = lax.broadcasted_iota(I32, (n_exp, n_exp), 1)
    lower = jnp.where(ec < er, 1.0, 0.0).astype(BF16)
    slot_all = _dot(lower, run_len.astype(BF16)) + rank
    slot_ref[...] = jnp.concatenate(
        [jnp.sum(jnp.where(s, slot_all, 0.0), axis=0, keepdims=True) for s in sels],
        axis=0).astype(I32)
    cnt_ref[0] = _dot_nt(jnp.ones((8, win), BF16), sel_b)


def _router(x, mod, rows, norm_g, w_router, bias):
    n, d = x.shape
    n_exp = w_router.shape[1]
    win = rows.tile
    lanes = pl.BlockSpec((TOP_K, win), lambda i: (0, i))
    return pl.pallas_call(
        _router_kernel,
        grid=(rows.grid,),
        in_specs=[rows.rows(d), rows.mod(d, 3), rows.mod(d, 4), _resident((1, d)),
                  _resident((n_exp, d)), _resident((n_exp, 1))],
        out_specs=[rows.rows(d), lanes, lanes, pl.BlockSpec((1, 8, n_exp), lambda i: (i, 0, 0))],
        out_shape=[jax.ShapeDtypeStruct((n, d), BF16), jax.ShapeDtypeStruct((TOP_K, n), I32),
                   jax.ShapeDtypeStruct((TOP_K, n), F32),
                   jax.ShapeDtypeStruct((rows.grid, 8, n_exp), F32)],
        compiler_params=_cparams("parallel"),
        name="moe_router",
    )(x, mod, mod, norm_g.reshape(1, d), w_router.T, bias.reshape(n_exp, 1))


def _sorted_rows(win, n_exp):
    return -(-(win * TOP_K + n_exp * (RUN_ALIGN - 1)) // SORT_BLOCK) * SORT_BLOCK


def _copy_list(count, src0, dst0, length, step):
    n_exp = count.shape[1]
    end = jnp.cumsum(count, axis=1)
    idx = jnp.arange(length, dtype=I32)
    owner = jnp.sum((end[:, None, :] <= idx[None, :, None]).astype(I32), axis=2)
    own = (owner[:, :, None] == jnp.arange(n_exp, dtype=I32)[None, None, :]).astype(I32)
    pick = lambda a: jnp.sum(own * a[:, None, :], axis=2)
    off = (idx[None, :] - pick(end - count)) * step
    return (pick(src0) + off).reshape(-1), (pick(dst0) + off).reshape(-1), end[:, -1]


def _plan(cnt, tile, sorted_rows):
    c = cnt[:, 0, :].astype(I32)
    run = (c + RUN_ALIGN - 1) // RUN_ALIGN * RUN_ALIGN
    per_exp = jnp.sum(run, axis=0)
    padded = (per_exp + tile - 1) // tile * tile
    pend = jnp.cumsum(padded)
    pstart = pend - padded
    dest = pstart[None, :] + jnp.cumsum(run, axis=0) - run
    local = jnp.cumsum(run, axis=1) - run
    big = COPY_UNITS[0] * RUN_ALIGN
    n_big = run // big
    covered = n_big * big
    copies = _copy_list(n_big, local, dest, sorted_rows // big, big)
    for units in COPY_UNITS[1:]:
        one = ((run - covered) == units * RUN_ALIGN).astype(I32)
        copies += _copy_list(one, local + covered, dest + covered, c.shape[1], units * RUN_ALIGN)
    return dict(
        copies=copies, list_rows=sorted_rows,
        pad_start=pstart + per_exp, pad_chunks=(padded - per_exp) // RUN_ALIGN,
        pend=pend, used_tiles=(pend[-1] // tile).reshape(1))


def _capacity(n, n_win, n_exp, tile):
    worst = n * TOP_K + n_win * n_exp * (RUN_ALIGN - 1) + n_exp * (tile - RUN_ALIGN)
    return -(-worst // tile) * tile


def _copy_classes(copy_refs, sorted_rows, n_exp):
    for j, units in enumerate(COPY_UNITS):
        n_rows = units * RUN_ALIGN
        length = sorted_rows // n_rows if j == 0 else n_exp
        yield (*copy_refs[3 * j:3 * j + 3], length, n_rows)


def _used_rows(copy_refs, window, sorted_rows, n_exp):
    return sum(n_ref[window] * n_rows
               for _, _, n_ref, _, n_rows in _copy_classes(copy_refs, sorted_rows, n_exp))


def _window_copies(copy_refs, window, sorted_rows, n_exp, make_copy, wait):
    for src_ref, dst_ref, n_ref, length, n_rows in _copy_classes(copy_refs, sorted_rows, n_exp):
        def per_copy(i, carry, src_ref=src_ref, dst_ref=dst_ref, length=length, n_rows=n_rows):
            cp = make_copy(pl.multiple_of(src_ref[window * length + i], RUN_ALIGN),
                           pl.multiple_of(dst_ref[window * length + i], RUN_ALIGN), n_rows)
            if wait:
                cp.wait()
            else:
                cp.start()
            return carry

        lax.fori_loop(0, n_ref[window], per_copy, 0)


def _dispatch_kernel(*refs, n_exp, n_win, tile, groups, list_rows):
    n_lists = 3 * len(COPY_UNITS)
    copy_refs = refs[:n_lists]
    pad_start_ref, pad_chunks_ref, used_ref = refs[n_lists:n_lists + 3]
    refs = refs[n_lists + 3:]
    xbuf_ref, xs_scr, zero_scr, sems, tail_sem = refs[-5:]
    w = pl.program_id(0)
    cur = w % 2
    n_tail = xbuf_ref.shape[0] // tile - used_ref[0]

    def tail_copy(j):
        dst = pl.multiple_of((used_ref[0] + j) * tile, tile)
        return pltpu.make_async_copy(zero_scr, xbuf_ref.at[pl.ds(dst, tile)], tail_sem)

    @pl.when(w == 0)
    def _():
        zero_scr[...] = jnp.zeros_like(zero_scr)
        lax.fori_loop(0, n_tail, lambda j, c: (tail_copy(j).start(), c)[1], 0)

    def sort_window(h2_ref, slot_ref, n_rows):
        slot = slot_ref[...]
        win = slot.shape[1]
        h2 = h2_ref[...]
        local = lax.broadcasted_iota(I32, (ONEHOT_BLOCK, win), 0).astype(F32).astype(BF16)
        slot_block = slot // ONEHOT_BLOCK
        slot_local = (slot % ONEHOT_BLOCK).astype(F32)
        one, zero = jnp.ones((), BF16), jnp.zeros((), BF16)

        def sort_block(blk):
            key = jnp.where(slot_block == blk, slot_local, -1.0).astype(BF16)
            onehot = jnp.where(local == key[0:1, :], one, zero)
            for k in range(1, TOP_K):
                onehot = jnp.where(local == key[k:k + 1, :], one, onehot)
            lo = blk * ONEHOT_BLOCK
            xs_scr[cur, lo:lo + ONEHOT_BLOCK, :] = _dot(onehot, h2).astype(BF16)

        used_rows = _used_rows(copy_refs, w, list_rows, n_exp)
        typical = win * TOP_K + n_exp * (RUN_ALIGN - 1) // 2
        for blk in range(n_rows // ONEHOT_BLOCK):
            if blk * ONEHOT_BLOCK < typical:
                sort_block(blk)
            else:
                pl.when(used_rows > blk * ONEHOT_BLOCK)(functools.partial(sort_block, blk))

    for gi, (first, count, n_rows) in enumerate(groups):
        @pl.when(jnp.logical_and(w >= first, w < first + count))
        def _(gi=gi, n_rows=n_rows):
            sort_window(refs[2 * gi], refs[2 * gi + 1], n_rows)

    def copies(window, buf, wait):
        def copy(src, dst, n):
            return pltpu.make_async_copy(xs_scr.at[buf, pl.ds(src, n)],
                                         xbuf_ref.at[pl.ds(dst, n)], sems.at[buf])
        _window_copies(copy_refs, window, list_rows, n_exp, copy, wait)

    copies(w, cur, wait=False)

    @pl.when(w > 0)
    def _():
        copies(w - 1, 1 - cur, wait=True)

    @pl.when(w == n_win - 1)
    def _():
        copies(w, cur, wait=True)

    def fill_pads_and_finish():
        sem = sems.at[cur]

        big = PAD_PIECE // RUN_ALIGN

        def pad_copy(row, n_rows):
            return pltpu.make_async_copy(zero_scr.at[pl.ds(0, n_rows)],
                                         xbuf_ref.at[pl.ds(pl.multiple_of(row, RUN_ALIGN), n_rows)],
                                         sem)

        def pads(wait):
            def go(cp):
                if wait:
                    cp.wait()
                else:
                    cp.start()

            def per_expert(e, carry):
                n_big = pad_chunks_ref[e] // big
                rest = pad_start_ref[e] + n_big * PAD_PIECE
                lax.fori_loop(0, n_big, lambda j, c: (go(pad_copy(
                    pad_start_ref[e] + j * PAD_PIECE, PAD_PIECE)), c)[1], 0)
                lax.fori_loop(0, pad_chunks_ref[e] - n_big * big, lambda j, c: (go(pad_copy(
                    rest + j * RUN_ALIGN, RUN_ALIGN)), c)[1], 0)
                return carry
            lax.fori_loop(0, n_exp, per_expert, 0)

        pads(False)
        pads(True)
        lax.fori_loop(0, n_tail, lambda j, c: (tail_copy(j).wait(), c)[1], 0)

    pl.when(w == n_win - 1)(fill_pads_and_finish)


def _dispatch(sources, plan, n_exp, cap, tile):
    d = sources[0][0].shape[1]
    in_specs, operands, groups, first = [], [], [], 0
    for h2, slot_t, rows in sources:
        idx = lambda i, first=first, last=rows.grid - 1: jnp.clip(i - first, 0, last)
        in_specs += [pl.BlockSpec((rows.tile, d), lambda i, *_, idx=idx: (idx(i), 0)),
                     pl.BlockSpec((TOP_K, rows.tile), lambda i, *_, idx=idx: (0, idx(i)))]
        operands += [h2, slot_t]
        groups.append((first, rows.grid, _sorted_rows(rows.tile, n_exp)))
        first += rows.grid
    return pl.pallas_call(
        functools.partial(_dispatch_kernel, n_exp=n_exp, n_win=first, tile=tile,
                          groups=tuple(groups), list_rows=plan["list_rows"]),
        grid_spec=pltpu.PrefetchScalarGridSpec(
            num_scalar_prefetch=3 * len(COPY_UNITS) + 3,
            grid=(first,),
            in_specs=in_specs,
            out_specs=pl.BlockSpec(memory_space=pl.ANY),
            scratch_shapes=[pltpu.VMEM((2, plan["list_rows"], d), BF16),
                            pltpu.VMEM((tile, d), BF16),
                            pltpu.SemaphoreType.DMA((2,)), pltpu.SemaphoreType.DMA(())]),
        out_shape=jax.ShapeDtypeStruct((cap, d), BF16),
        compiler_params=_cparams("arbitrary"),
        name="moe_dispatch",
    )(*plan["copies"], plan["pad_start"], plan["pad_chunks"], plan["used_tiles"], *operands)


def _experts_kernel(texp_ref, used_ref, xbuf_ref, wg_ref, wu_ref, wd_ref, ybuf_ref,
                    x_ring, sems, y_ring, y_sems, wgu_scr, wd_scr, *, sub_tiles):
    i = pl.program_id(0)
    used = used_ref[0]
    live = i < used
    de = wd_scr.shape[0]
    depth, tile = x_ring.shape[0], x_ring.shape[1]
    y_depth = y_ring.shape[0]

    def fetch(j):
        slot = j % depth
        return pltpu.make_async_copy(xbuf_ref.at[pl.ds(pl.multiple_of(j * tile, tile), tile)],
                                     x_ring.at[slot], sems.at[slot])

    def put(j):
        slot = j % y_depth
        return pltpu.make_async_copy(y_ring.at[slot],
                                     ybuf_ref.at[pl.ds(pl.multiple_of(j * tile, tile), tile)],
                                     y_sems.at[slot])

    @pl.when(i == 0)
    def _():
        for j in range(depth - 1):
            @pl.when(j < used)
            def _():
                fetch(j).start()

    @pl.when(i + (depth - 1) < used)
    def _():
        fetch(i + (depth - 1)).start()

    @pl.when(jnp.logical_or(i == 0, texp_ref[i] != texp_ref[jnp.maximum(i - 1, 0)]))
    def _():
        wgu_scr[:, :de] = wg_ref[...].astype(BF16)
        wgu_scr[:, de:] = wu_ref[...].astype(BF16)
        wd_scr[...] = wd_ref[...].astype(BF16)

    @pl.when(live)
    def _():
        fetch(i).wait()

        @pl.when(i >= y_depth)
        def _():
            put(i - y_depth).wait()

        x_ref, y_ref = x_ring.at[i % depth], y_ring.at[i % y_depth]
        sub = tile // sub_tiles
        for s in range(sub_tiles):
            rows = slice(s * sub, (s + 1) * sub)
            ab = _dot(x_ref[rows, :], wgu_scr[...])
            mid = (_silu(ab[:, :de]) * ab[:, de:]).astype(BF16)
            y_ref[rows, :] = _dot(mid, wd_scr[...]).astype(BF16)
        put(i).start()

    @pl.when(i == pl.num_programs(0) - 1)
    def _():
        for back in range(y_depth, 0, -1):
            @pl.when(used >= back)
            def _():
                put(used - back).wait()


def _experts(xbuf, plan, layer, w_gate, w_up, w_down, tile):
    cap, d = xbuf.shape
    _, n_exp, _, de = w_gate.shape
    n_tiles = cap // tile
    used = plan["used_tiles"]
    tmap = jnp.minimum(jnp.arange(n_tiles, dtype=I32), used[0] - 1)
    texp = jnp.sum((plan["pend"][None, :] <= (tmap * tile)[:, None]).astype(I32), axis=1)
    texp = jnp.minimum(texp, n_exp - 1)
    return pl.pallas_call(
        functools.partial(_experts_kernel, sub_tiles=max(tile // EXPERT_SUB_TILE, 1)),
        grid_spec=pltpu.PrefetchScalarGridSpec(
            num_scalar_prefetch=2,
            grid=(n_tiles,),
            in_specs=[pl.BlockSpec(memory_space=pl.ANY),
                      pl.BlockSpec((None, None, d, de), lambda i, te, u: (layer, te[i], 0, 0)),
                      pl.BlockSpec((None, None, d, de), lambda i, te, u: (layer, te[i], 0, 0)),
                      pl.BlockSpec((None, None, de, d), lambda i, te, u: (layer, te[i], 0, 0))],
            out_specs=pl.BlockSpec(memory_space=pl.ANY),
            scratch_shapes=[pltpu.VMEM((EXPERT_RING, tile, d), BF16),
                            pltpu.SemaphoreType.DMA((EXPERT_RING,)),
                            pltpu.VMEM((2, tile, d), BF16), pltpu.SemaphoreType.DMA((2,)),
                            pltpu.VMEM((d, 2 * de), BF16), pltpu.VMEM((de, d), BF16)]),
        out_shape=jax.ShapeDtypeStruct((cap, d), BF16),
        input_output_aliases={2: 0},
        compiler_params=_cparams("arbitrary"),
        name="moe_experts",
    )(texp, used, xbuf, w_gate, w_up, w_down)


def _combine_kernel(*refs, n_exp, n_win, final_norm, window_base, list_rows):
    n_lists = 3 * len(COPY_UNITS)
    copy_refs = refs[:n_lists]
    (ybuf_ref, slot_ref, wt_ref, h2_ref, x_ref, g2_ref, sg_ref, su_ref, sd_ref, fin_ref,
     out_ref, ys_scr, sems, acc_scr) = refs[n_lists:]
    w = pl.program_id(0)
    cur = w % 2
    n_rows = ys_scr.shape[1]

    def copies(window, buf, wait):
        def copy(loc, dst, n):
            return pltpu.make_async_copy(ybuf_ref.at[pl.ds(dst, n)],
                                         ys_scr.at[buf, pl.ds(loc, n)], sems.at[buf])
        _window_copies(copy_refs, window_base + window, list_rows, n_exp, copy, wait)

    @pl.when(w == 0)
    def _():
        ys_scr[...] = jnp.zeros_like(ys_scr)
        copies(w, cur, wait=False)

    @pl.when(w + 1 < n_win)
    def _():
        copies(w + 1, 1 - cur, wait=False)

    h2 = h2_ref[...]
    shared = _dot((_silu(_dot(h2, sg_ref[...])) * _dot(h2, su_ref[...])).astype(BF16), sd_ref[...])
    copies(w, cur, wait=True)

    slot, wt = slot_ref[...], wt_ref[...].astype(BF16)
    win = slot.shape[1]
    local = lax.broadcasted_iota(I32, (ONEHOT_BLOCK, win), 0).astype(F32).astype(BF16)
    slot_block = slot // ONEHOT_BLOCK
    slot_local = (slot % ONEHOT_BLOCK).astype(F32)
    def block_sum(blk):
        key = jnp.where(slot_block == blk, slot_local, -1.0).astype(BF16)
        weights = jnp.where(local == key[0:1, :], wt[0:1, :], jnp.zeros((), BF16))
        for k in range(1, TOP_K):
            weights = jnp.where(local == key[k:k + 1, :], wt[k:k + 1, :], weights)
        lo = blk * ONEHOT_BLOCK
        return _dot_tn(weights, ys_scr[cur, lo:lo + ONEHOT_BLOCK, :])

    used_rows = _used_rows(copy_refs, window_base + w, list_rows, n_exp)
    typical = win * TOP_K + n_exp * (RUN_ALIGN - 1) // 2
    n_blocks = n_rows // ONEHOT_BLOCK
    always = [blk for blk in range(n_blocks) if blk * ONEHOT_BLOCK < typical]
    routed = shared
    for blk in always:
        routed = routed + block_sum(blk)
    if len(always) < n_blocks:
        acc_scr[...] = routed
        for blk in range(len(always), n_blocks):
            @pl.when(used_rows > blk * ONEHOT_BLOCK)
            def _(blk=blk):
                acc_scr[...] += block_sum(blk)
        routed = acc_scr[...]
    out = x_ref[...] + g2_ref[...] * routed
    if final_norm:
        out = out * lax.rsqrt(jnp.mean(out * out, axis=-1, keepdims=True) + EPS) * fin_ref[...]
    out_ref[...] = out


def _combine(ybuf, plan, slot, wt, h2, x, mod, rows, sh_bf16, final_g, n_exp, final_norm,
             window_base):
    n, d = x.shape
    win = rows.tile
    sg, su, sd = sh_bf16
    pairs = pl.BlockSpec((TOP_K, win), lambda i, *_: (0, i))
    return pl.pallas_call(
        functools.partial(_combine_kernel, n_exp=n_exp, n_win=rows.grid, final_norm=final_norm,
                          window_base=window_base, list_rows=plan["list_rows"]),
        grid_spec=pltpu.PrefetchScalarGridSpec(
            num_scalar_prefetch=3 * len(COPY_UNITS),
            grid=(rows.grid,),
            in_specs=[pl.BlockSpec(memory_space=pl.ANY), pairs, pairs, rows.rows(d), rows.rows(d),
                      rows.mod(d, 5), _resident(sg.shape), _resident(su.shape),
                      _resident(sd.shape), _resident((1, d))],
            out_specs=rows.rows(d),
            scratch_shapes=[pltpu.VMEM((2, _sorted_rows(win, n_exp), d), BF16),
                            pltpu.SemaphoreType.DMA((2,)), pltpu.VMEM((win, d), F32)]),
        out_shape=jax.ShapeDtypeStruct((n, d), F32),
        compiler_params=_cparams("arbitrary"),
        name="moe_combine",
    )(*plan["copies"], ybuf, slot, wt, h2, x, mod, sg, su, sd, final_g.reshape(1, d))


def _moe(groups, layer, norm_g, w_router, bias, w_gate, w_up, w_down, sh_bf16, final_g,
         final_norm, tile):
    n_exp = w_router.shape[1]
    routed = [_router(x, mod, rows, norm_g, w_router, bias) for x, mod, rows in groups]
    bases, total = [], 0
    for _, _, rows in groups:
        bases.append(total)
        total += rows.grid
    plan = _plan(jnp.concatenate([r[3] for r in routed], axis=0), tile,
                 max(_sorted_rows(rows.tile, n_exp) for _, _, rows in groups))
    cap = _capacity(sum(x.shape[0] for x, _, _ in groups), total, n_exp, tile)
    xbuf = _dispatch([(r[0], r[1], g[2]) for r, g in zip(routed, groups)], plan, n_exp, cap, tile)
    ybuf = _experts(xbuf, plan, layer, w_gate, w_up, w_down, tile)
    return [_combine(ybuf, plan, slot_t, wt_t, h2, x, mod, rows, sh_bf16, final_g, n_exp,
                     final_norm, base)
            for (x, mod, rows), (h2, slot_t, wt_t, _), base in zip(groups, routed, bases)]


def _rope_tables(pos, half):
    inv = ROPE_BASE ** (-jnp.arange(half, dtype=F32) / half)
    ang = pos.astype(F32)[:, None] * inv[None, :]
    return jnp.cos(ang), jnp.sin(ang)


class _Group:
    def __init__(self, x3, mod_all, pos0, s_ret, s_gla):
        self.batch, self.seq, d = x3.shape
        self.n = self.batch * self.seq
        self.x = x3.reshape(self.n, d)
        self.rows = _Rows(self.n, self.seq, min(ROW_TILE, self.n))
        self.decode = self.seq == 1
        self.mod_all, self.pos0, self.s_ret, self.s_gla = mod_all, pos0, s_ret, s_gla
        self.new_ret, self.new_gla = [], []

    def mod(self, layer):
        m = self.mod_all[layer]
        return m if self.decode else m.reshape(self.batch, 1, m.shape[-1])


def _mixer(g, layer, wts):
    d = g.x.shape[1]
    x, mod, rows = g.x, g.mod(layer), g.rows
    ret_dk = d // RET_HEADS
    ret_dv = 2 * ret_dk
    gla_dk = d // (2 * GLA_HEADS)
    gla_dv = d // GLA_HEADS
    j = layer // 2
    if layer % 2 == 0:
        pos = (jnp.full((rows.tile,), g.pos0, I32) if g.decode
               else g.pos0 + jnp.arange(g.seq, dtype=I32))
        cos, sin = _rope_tables(pos, ret_dk // 2)
        q, k, v, gate = _ret_proj(x, mod, rows, wts["norm_mix_g"][layer], wts["ret_w_in"][j],
                                  cos, sin, RET_HEADS, ret_dk, ret_dv)
        log_gamma = jnp.log1p(-jnp.power(2.0, -5.0 - jnp.arange(RET_HEADS, dtype=F32)))
        if g.decode:
            decay = jnp.broadcast_to(jnp.repeat(jnp.exp(log_gamma), ret_dk)[None, :],
                                     (g.n, RET_HEADS * ret_dk))
            o, s_new = _step(g.s_ret, j, q, k, decay, v, RET_HEADS, ret_dk, ret_dv)
            out = _post(o, gate, x, mod, rows, wts["ret_norm_g"][j], wts["ret_w_out"][j],
                        RET_HEADS, ret_dv, center=True)
        else:
            out, s_new = _ret_core(q, k, v, gate, x, mod, wts["ret_norm_g"][j],
                                   wts["ret_w_out"][j], log_gamma, g.batch, g.seq,
                                   RET_HEADS, ret_dk, ret_dv)
        g.new_ret.append(s_new)
        return out
    q, k, v, gate, la = _gla_proj(x, mod, rows, wts["norm_mix_g"][layer], wts["gla_w_in"][j],
                                  wts["gla_w_a1"][j], wts["gla_w_a2"][j], wts["gla_b_a"][j],
                                  GLA_HEADS * gla_dk, GLA_HEADS * gla_dv)
    if g.decode:
        o, s_new = _step(g.s_gla, j, q, k, jnp.exp(la), v, GLA_HEADS, gla_dk, gla_dv)
    else:
        o, s_new = _gla_core(q, k, v, la, g.batch, g.seq, GLA_HEADS, gla_dk, gla_dv)
    g.new_gla.append(s_new)
    return _post(o, gate, x, mod, rows, wts["gla_norm_g"][j], wts["gla_w_out"][j],
                 GLA_HEADS, gla_dv, center=False)


def _trunk(groups, wts):
    depth = wts["ada_w"].shape[0]
    for layer in range(depth):
        mixed = [(_mixer(g, layer, wts), g.mod(layer), g.rows) for g in groups]
        outs = _moe(mixed, layer, wts["norm_ffn_g"][layer], wts["moe_w_router"][layer],
                    wts["moe_router_bias"][layer], wts["moe_w_gate"], wts["moe_w_up"],
                    wts["moe_w_down"], wts["shared"][layer], wts["final_norm_g"],
                    final_norm=layer == depth - 1, tile=EXPERT_TILE)
        for g, x in zip(groups, outs):
            g.x = x
    return [(g.x.reshape(g.batch, g.seq, -1), jnp.stack(g.new_ret), jnp.stack(g.new_gla))
            for g in groups]


def kernel(x_prompt, x_sample, state_ret, state_gla, c_prompt, c_sample, ret_w_in, ret_norm_g, ret_w_out, gla_w_in, gla_w_a1, gla_w_a2, gla_b_a, gla_norm_g, gla_w_out, ada_w, ada_b, norm_mix_g, norm_ffn_g, moe_w_router, moe_router_bias, moe_w_gate, moe_w_up, moe_w_down, sh_w_gate, sh_w_up, sh_w_down, final_norm_g):
    b = x_prompt.shape[0]
    depth = ada_w.shape[0]
    rank = gla_w_a1.shape[-1]
    pad = LANES - rank
    wts = dict(
        ret_w_in=ret_w_in.astype(BF16), ret_norm_g=ret_norm_g, ret_w_out=ret_w_out.astype(BF16),
        gla_w_in=gla_w_in.astype(BF16),
        gla_w_a1=jnp.pad(gla_w_a1, ((0, 0), (0, 0), (0, pad))).astype(BF16),
        gla_w_a2=jnp.pad(gla_w_a2, ((0, 0), (0, pad), (0, 0))).astype(BF16),
        gla_b_a=gla_b_a, gla_norm_g=gla_norm_g, gla_w_out=gla_w_out.astype(BF16),
        ada_w=ada_w, norm_mix_g=norm_mix_g, norm_ffn_g=norm_ffn_g,
        moe_w_router=moe_w_router, moe_router_bias=moe_router_bias,
        moe_w_gate=moe_w_gate, moe_w_up=moe_w_up, moe_w_down=moe_w_down,
        shared=[(sh_w_gate[l].astype(BF16), sh_w_up[l].astype(BF16), sh_w_down[l].astype(BF16))
                for l in range(depth)],
        final_norm_g=final_norm_g)
    mod = _ada(jnp.concatenate([c_prompt, c_sample], axis=0), ada_w, ada_b)
    (y_p, ret_p, gla_p), (y_s, ret_s, gla_s) = _trunk(
        [_Group(x_prompt, mod[:, :b], 0, None, None),
         _Group(x_sample, mod[:, b:], PAST_LEN, state_ret, state_gla)], wts)
    return (y_p, y_s, ret_p, gla_p, ret_s, gla_s)
```

```python
import functools

import jax
import jax.numpy as jnp
from jax import lax
from jax.experimental import pallas as pl
from jax.experimental.pallas import tpu as pltpu

F32, BF16, I32 = jnp.float32, jnp.bfloat16, jnp.int32

EPS = 1e-6
ROPE_BASE = 10000.0
PAST_LEN = 16384
RET_HEADS = 4
GLA_HEADS = 4
GLA_TAU = 16.0
N_GROUPS = 8
TOPK_GROUPS = 4
TOP_K = 8
ROUTED_SCALE = 2.5

LANES = 128
BF16_SUBLANES = 16
VMEM_LIMIT_BYTES = 56 * 1024 * 1024

ROW_TILE = 256
RET_CHUNK = 256
GLA_CHUNK = 128
GLA_KEY_BLOCK = 32
STEP_TOKENS = 16
RUN_ALIGN = BF16_SUBLANES
EXPERT_TILE = 1024
EXPERT_SUB_TILE = 1024
SORT_BLOCK = 512
EXPERT_RING = 3
ONEHOT_BLOCK = 256
PAD_PIECE = 256
COPY_UNITS = (4, 3, 2, 1)


def _cparams(*sem):
    return pltpu.CompilerParams(dimension_semantics=sem, vmem_limit_bytes=VMEM_LIMIT_BYTES)


def _sigmoid(x):
    return 1.0 / (1.0 + jnp.exp(-x))


def _silu(x):
    return x * _sigmoid(x)


def _modulate(x, g, shift, scale):
    y = x * lax.rsqrt(jnp.mean(x * x, axis=-1, keepdims=True) + EPS) * g
    return y * (1.0 + scale) + shift


def _dot(a, b):
    return jnp.dot(a, b, preferred_element_type=F32)


def _dot_nt(a, b):
    return lax.dot_general(a, b, (((1,), (1,)), ((), ())), preferred_element_type=F32)


def _dot_tn(a, b):
    return lax.dot_general(a, b, (((0,), (0,)), ((), ())), preferred_element_type=F32)


def _resident(shape):
    zeros = (0,) * len(shape)
    return pl.BlockSpec(shape, lambda *_: zeros, pipeline_mode=pl.Buffered(1))


class _Rows:
    def __init__(self, n_rows, seq_len, tile):
        self.n, self.tile = n_rows, tile
        self.per_row = seq_len == 1
        self.tiles_per_seq = max(seq_len // tile, 1)
        assert n_rows % tile == 0 and (self.per_row or seq_len % tile == 0)
        self.grid = n_rows // tile

    def rows(self, width, col=0):
        return pl.BlockSpec((self.tile, width), lambda i, *_: (i, col))

    def mod(self, d, col):
        if self.per_row:
            return pl.BlockSpec((self.tile, d), lambda i, *_: (i, col))
        tps = self.tiles_per_seq
        return pl.BlockSpec((None, 1, d), lambda i, *_: (i // tps, 0, col))


def _ada_kernel(c_ref, w_ref, b_ref, o_ref):
    s = _silu(c_ref[...]).astype(BF16)
    o_ref[0] = _dot(s, w_ref[0].astype(BF16)) + b_ref[0]


def _ada(c_all, ada_w, ada_b):
    depth, d, d6 = ada_w.shape
    n = c_all.shape[0]
    tn = d6 // 4
    return pl.pallas_call(
        _ada_kernel,
        grid=(depth, d6 // tn),
        in_specs=[pl.BlockSpec((n, d), lambda l, j: (0, 0)),
                  pl.BlockSpec((1, d, tn), lambda l, j: (l, 0, j)),
                  pl.BlockSpec((1, 1, tn), lambda l, j: (l, 0, j))],
        out_specs=pl.BlockSpec((1, n, tn), lambda l, j: (l, 0, j)),
        out_shape=jax.ShapeDtypeStruct((depth, n, d6), F32),
        compiler_params=_cparams("parallel", "parallel"),
        name="ada_mod",
    )(c_all, ada_w, ada_b.reshape(depth, 1, d6))


def _ret_proj_kernel(x_ref, sh_ref, sc_ref, g_ref, w_ref, cos_ref, sin_ref,
                     q_ref, k_ref, v_ref, gate_ref, *, n_heads, dk, dv):
    h = _modulate(x_ref[...], g_ref[...], sh_ref[...], sc_ref[...]).astype(BF16)
    cos, sin = cos_ref[...], sin_ref[...]
    half, nqk, nv = dk // 2, n_heads * dk, n_heads * dv
    for hd in range(n_heads):
        for dst, base, scale in ((q_ref, 0, None), (k_ref, nqk, dk ** -0.5)):
            p = _dot(h, w_ref[:, base + hd * dk:base + (hd + 1) * dk])
            x1, x2 = p[:, :half], p[:, half:]
            r1, r2 = x1 * cos - x2 * sin, x1 * sin + x2 * cos
            if scale is not None:
                r1, r2 = r1 * scale, r2 * scale
            dst[:, hd * dk:hd * dk + half] = r1.astype(BF16)
            dst[:, hd * dk + half:(hd + 1) * dk] = r2.astype(BF16)
    for hd in range(n_heads):
        v_ref[:, hd * dv:(hd + 1) * dv] = _dot(
            h, w_ref[:, 2 * nqk + hd * dv:2 * nqk + (hd + 1) * dv]).astype(BF16)
        gate_ref[:, hd * dv:(hd + 1) * dv] = _dot(
            h, w_ref[:, 2 * nqk + nv + hd * dv:2 * nqk + nv + (hd + 1) * dv])


def _ret_proj(x, mod, rows, norm_g, w_in_bf16, cos, sin, n_heads, dk, dv):
    n, d = x.shape
    nqk, nv = n_heads * dk, n_heads * dv
    half = dk // 2
    tps = rows.tiles_per_seq
    trig = (pl.BlockSpec((rows.tile, half), lambda i: (0, 0)) if rows.per_row
            else pl.BlockSpec((rows.tile, half), lambda i: (i % tps, 0)))
    return pl.pallas_call(
        functools.partial(_ret_proj_kernel, n_heads=n_heads, dk=dk, dv=dv),
        grid=(rows.grid,),
        in_specs=[rows.rows(d), rows.mod(d, 0), rows.mod(d, 1), _resident((1, d)),
                  _resident(w_in_bf16.shape), trig, trig],
        out_specs=[rows.rows(nqk), rows.rows(nqk), rows.rows(nv), rows.rows(nv)],
        out_shape=[jax.ShapeDtypeStruct((n, nqk), BF16), jax.ShapeDtypeStruct((n, nqk), BF16),
                   jax.ShapeDtypeStruct((n, nv), BF16), jax.ShapeDtypeStruct((n, nv), F32)],
        compiler_params=_cparams("parallel"),
        name="ret_proj",
    )(x, mod, mod, norm_g.reshape(1, d), w_in_bf16, cos, sin)


def _head_out(o, gate, norm_g, w_out, center):
    if center:
        o = o - jnp.mean(o, axis=-1, keepdims=True)
    y = o * lax.rsqrt(jnp.mean(o * o, axis=-1, keepdims=True) + EPS) * norm_g
    return _dot((y * _silu(gate)).astype(BF16), w_out)


def _ret_core_kernel(lg_ref, q_ref, k_ref, v_ref, gate_ref, x_ref, g1_ref, ng_ref, w_ref,
                     out_ref, s_out_ref, s_scr, *, tc, n_chunks, n_heads, dk, dv):
    c = pl.program_id(1)

    @pl.when(c == 0)
    def _():
        s_scr[...] = jnp.zeros_like(s_scr)

    ii = lax.broadcasted_iota(I32, (tc, tc), 0)
    jj = lax.broadcasted_iota(I32, (tc, tc), 1)
    causal = ii >= jj
    lag = (ii - jj).astype(F32)
    row = lax.broadcasted_iota(I32, (tc, 1), 0).astype(F32)
    mixed = None
    for hd in range(n_heads):
        lg = lg_ref[hd]
        q, k = q_ref[:, hd * dk:(hd + 1) * dk], k_ref[:, hd * dk:(hd + 1) * dk]
        cols = slice(hd * dv, (hd + 1) * dv)
        v = v_ref[:, cols]
        dec = jnp.where(causal, jnp.exp(lag * lg), 0.0)
        att = (_dot_nt(q, k) * dec).astype(BF16)
        s_old = s_scr[hd]
        o = _dot(att, v) + jnp.exp((row + 1.0) * lg) * _dot(q, s_old.astype(BF16))
        part = _head_out(o, gate_ref[:, cols], ng_ref[:, cols], w_ref[cols, :], center=True)
        mixed = part if mixed is None else mixed + part
        kd = (k.astype(F32) * jnp.exp((tc - 1.0 - row) * lg)).astype(BF16)
        s_scr[hd] = jnp.exp(jnp.full((1, 1), float(tc), F32) * lg) * s_old + _dot_tn(kd, v)
    out_ref[...] = x_ref[...] + g1_ref[...] * mixed

    @pl.when(c == n_chunks - 1)
    def _():
        s_out_ref[0] = s_scr[...]


def _core_specs(tc, nc, d, widths):
    row = lambda width: pl.BlockSpec((tc, width), lambda b, c: (b * nc + c, 0))
    g1 = pl.BlockSpec((None, 1, d), lambda b, c: (b, 0, 2))
    return [row(w) for w in widths], row(d), g1


def _ret_core(q, k, v, gate, x, mod, norm_g, w_out_bf16, log_gamma, batch, seq, n_heads, dk, dv):
    tc = min(RET_CHUNK, seq)
    nc = seq // tc
    assert seq % tc == 0
    n, d = x.shape
    nqk, nv = n_heads * dk, n_heads * dv
    ins, xrow, g1 = _core_specs(tc, nc, d, (nqk, nqk, nv, nv))
    return pl.pallas_call(
        functools.partial(_ret_core_kernel, tc=tc, n_chunks=nc, n_heads=n_heads, dk=dk, dv=dv),
        grid=(batch, nc),
        in_specs=[pl.BlockSpec(memory_space=pltpu.SMEM), *ins, xrow, g1,
                  _resident((1, nv)), _resident(w_out_bf16.shape)],
        out_specs=[xrow, pl.BlockSpec((1, n_heads, dk, dv), lambda b, c: (b, 0, 0, 0))],
        out_shape=[jax.ShapeDtypeStruct((n, d), F32),
                   jax.ShapeDtypeStruct((batch, n_heads, dk, dv), F32)],
        scratch_shapes=[pltpu.VMEM((n_heads, dk, dv), F32)],
        compiler_params=_cparams("parallel", "arbitrary"),
        name="ret_core",
    )(log_gamma, q, k, v, gate, x, mod, norm_g.reshape(1, nv), w_out_bf16)


def _step_kernel(s_ref, q_ref, k_ref, a_ref, v_ref, o_ref, s_out_ref, *, tb):
    for j in range(tb):
        s_new = (a_ref[0, 0, :, j:j + 1] * s_ref[j, 0]
                 + k_ref[0, 0, :, j:j + 1] * v_ref[j:j + 1, :])
        s_out_ref[j, 0] = s_new
        o_ref[j:j + 1, :] = jnp.sum(q_ref[0, 0, :, j:j + 1] * s_new, axis=0, keepdims=True)


def _columns(x, n_heads, dk, tb):
    n = x.shape[0]
    return x.astype(F32).reshape(n // tb, tb, n_heads, dk).transpose(2, 0, 3, 1)


def _step(states, layer, q, k, a, v, n_heads, dk, dv):
    n = states.shape[1]
    tb = STEP_TOKENS
    assert n % tb == 0
    col = pl.BlockSpec((1, 1, dk, tb), lambda i, h: (h, i, 0, 0))
    st_in = pl.BlockSpec((None, tb, 1, dk, dv), lambda i, h: (layer, i, h, 0, 0))
    st_out = pl.BlockSpec((tb, 1, dk, dv), lambda i, h: (i, h, 0, 0))
    row = pl.BlockSpec((tb, dv), lambda i, h: (i, h))
    return pl.pallas_call(
        functools.partial(_step_kernel, tb=tb),
        grid=(n // tb, n_heads),
        in_specs=[st_in, col, col, col, row],
        out_specs=[row, st_out],
        out_shape=[jax.ShapeDtypeStruct((n, n_heads * dv), F32),
                   jax.ShapeDtypeStruct(states.shape[1:], states.dtype)],
        compiler_params=_cparams("parallel", "parallel"),
        name="state_step",
    )(states, _columns(q, n_heads, dk, tb), _columns(k, n_heads, dk, tb),
      _columns(a, n_heads, dk, tb), v.astype(F32))


def _gla_proj_kernel(x_ref, sh_ref, sc_ref, g_ref, w_ref, a1_ref, a2_ref, ba_ref,
                     q_ref, k_ref, v_ref, r_ref, la_ref, *, nqk, nv):
    h = _modulate(x_ref[...], g_ref[...], sh_ref[...], sc_ref[...]).astype(BF16)
    q_ref[...] = _dot(h, w_ref[:, :nqk]) * ((nqk // GLA_HEADS) ** -0.5)
    k_ref[...] = _dot(h, w_ref[:, nqk:2 * nqk])
    v_ref[...] = _dot(h, w_ref[:, 2 * nqk:2 * nqk + nv]).astype(BF16)
    r_ref[...] = _dot(h, w_ref[:, 2 * nqk + nv:])
    z = _dot(_dot(h, a1_ref[...]).astype(BF16), a2_ref[...]) + ba_ref[...]
    la_ref[...] = (jnp.minimum(z, 0.0) - jnp.log1p(jnp.exp(-jnp.abs(z)))) / GLA_TAU


def _gla_proj(x, mod, rows, norm_g, w_in_bf16, a1_bf16, a2_bf16, b_a, nqk, nv):
    n, d = x.shape
    return pl.pallas_call(
        functools.partial(_gla_proj_kernel, nqk=nqk, nv=nv),
        grid=(rows.grid,),
        in_specs=[rows.rows(d), rows.mod(d, 0), rows.mod(d, 1), _resident((1, d)),
                  _resident(w_in_bf16.shape), _resident(a1_bf16.shape),
                  _resident(a2_bf16.shape), _resident((1, nqk))],
        out_specs=[rows.rows(nqk), rows.rows(nqk), rows.rows(nv), rows.rows(nv), rows.rows(nqk)],
        out_shape=[jax.ShapeDtypeStruct((n, nqk), F32), jax.ShapeDtypeStruct((n, nqk), F32),
                   jax.ShapeDtypeStruct((n, nv), BF16), jax.ShapeDtypeStruct((n, nv), F32),
                   jax.ShapeDtypeStruct((n, nqk), F32)],
        compiler_params=_cparams("parallel"),
        name="gla_proj",
    )(x, mod, mod, norm_g.reshape(1, d), w_in_bf16, a1_bf16, a2_bf16, b_a.reshape(1, nqk))


def _gla_core_kernel(q_ref, k_ref, la_ref, v_ref, o_ref, s_out_ref, s_scr,
                     *, tc, n_chunks, n_heads, dk, dv):
    c = pl.program_id(1)

    @pl.when(c == 0)
    def _():
        s_scr[...] = jnp.zeros_like(s_scr)

    ii = lax.broadcasted_iota(I32, (tc, tc), 0)
    jj = lax.broadcasted_iota(I32, (tc, tc), 1)
    la = la_ref[...]
    la_hi = la.astype(BF16)
    la_mid = (la - la_hi.astype(F32)).astype(BF16)
    la_lo = (la - la_hi.astype(F32) - la_mid.astype(F32)).astype(BF16)
    tri = jnp.where(ii >= jj, 1.0, 0.0).astype(BF16)
    b_all = _dot(tri, la_hi) + _dot(tri, la_mid) + _dot(tri, la_lo)
    ones = jnp.ones((tc, LANES), BF16)
    tot_col_all = (_dot_tn(la_hi, ones) + _dot_tn(la_mid, ones) + _dot_tn(la_lo, ones))[:, 0:1]
    qi = lax.broadcasted_iota(I32, (tc, GLA_KEY_BLOCK), 0)
    kj = lax.broadcasted_iota(I32, (tc, GLA_KEY_BLOCK), 1)
    for hd in range(n_heads):
        cols = slice(hd * dk, (hd + 1) * dk)
        b = b_all[:, cols]
        q, k, v = q_ref[:, cols], k_ref[:, cols], v_ref[:, hd * dv:(hd + 1) * dv]
        s_old = s_scr[hd]
        o = _dot((q * jnp.exp(b)).astype(BF16), s_old.astype(BF16))
        for s in range(tc // GLA_KEY_BLOCK):
            lo = s * GLA_KEY_BLOCK
            blk = slice(lo, lo + GLA_KEY_BLOCK)
            mid = lo + GLA_KEY_BLOCK // 2 - 1
            ref = b[mid:mid + 1, :]
            rel = b - ref
            if s:
                rel = jnp.where(qi[:, 0:1] < lo, 0.0, rel)
            qs = (q * jnp.exp(rel)).astype(BF16)
            ks = (k[blk, :] * jnp.exp(ref - b[blk, :])).astype(BF16)
            att = jnp.where(qi >= kj + lo, _dot_nt(qs, ks), 0.0).astype(BF16)
            o = o + _dot(att, v[blk, :])
        o_ref[:, hd * dv:(hd + 1) * dv] = o
        kd = (k * jnp.exp(b[tc - 1:tc, :] - b)).astype(BF16)
        s_scr[hd] = jnp.exp(tot_col_all[cols, :]) * s_old + _dot_tn(kd, v)

    @pl.when(c == n_chunks - 1)
    def _():
        s_out_ref[0] = s_scr[...]


def _gla_core(q, k, v, la, batch, seq, n_heads, dk, dv):
    tc = min(GLA_CHUNK, seq)
    nc = seq // tc
    assert seq % tc == 0
    n = batch * seq
    nqk, nv = n_heads * dk, n_heads * dv
    (qk, vv), _, _ = _core_specs(tc, nc, nv, (nqk, nv))
    return pl.pallas_call(
        functools.partial(_gla_core_kernel, tc=tc, n_chunks=nc, n_heads=n_heads, dk=dk, dv=dv),
        grid=(batch, nc),
        in_specs=[qk, qk, qk, vv],
        out_specs=[vv, pl.BlockSpec((1, n_heads, dk, dv), lambda b, c: (b, 0, 0, 0))],
        out_shape=[jax.ShapeDtypeStruct((n, nv), F32),
                   jax.ShapeDtypeStruct((batch, n_heads, dk, dv), F32)],
        scratch_shapes=[pltpu.VMEM((n_heads, dk, dv), F32)],
        compiler_params=_cparams("parallel", "arbitrary"),
        name="gla_core",
    )(q, k, la, v)


def _post_kernel(o_ref, gate_ref, x_ref, g1_ref, ng_ref, w_ref, out_ref, *, n_heads, dv, center):
    acc = None
    for hd in range(n_heads):
        sl = slice(hd * dv, (hd + 1) * dv)
        part = _head_out(o_ref[:, sl], gate_ref[:, sl], ng_ref[:, sl], w_ref[sl, :], center)
        acc = part if acc is None else acc + part
    out_ref[...] = x_ref[...] + g1_ref[...] * acc


def _post(o, gate, x, mod, rows, norm_g, w_out_bf16, n_heads, dv, center):
    n, d = x.shape
    nv = n_heads * dv
    return pl.pallas_call(
        functools.partial(_post_kernel, n_heads=n_heads, dv=dv, center=center),
        grid=(rows.grid,),
        in_specs=[rows.rows(nv), rows.rows(nv), rows.rows(d), rows.mod(d, 2),
                  _resident((1, nv)), _resident(w_out_bf16.shape)],
        out_specs=rows.rows(d),
        out_shape=jax.ShapeDtypeStruct((n, d), F32),
        compiler_params=_cparams("parallel"),
        name="mixer_post",
    )(o, gate, x, mod, norm_g.reshape(1, nv), w_out_bf16)


def _first_max(vals, idx_iota, n):
    m = jnp.max(vals, axis=0, keepdims=True)
    i = jnp.min(jnp.where(vals == m, idx_iota, n), axis=0, keepdims=True)
    return m, i


def _router_kernel(x_ref, sh_ref, sc_ref, g_ref, wr_ref, b_ref,
                   h2_ref, slot_ref, wt_ref, cnt_ref):
    h2 = _modulate(x_ref[...], g_ref[...], sh_ref[...], sc_ref[...])
    h2_ref[...] = h2.astype(BF16)
    h_hi = h2.astype(BF16)
    h_lo = (h2 - h_hi.astype(F32)).astype(BF16)
    wr = wr_ref[...]
    w_hi = wr.astype(BF16)
    w_lo = (wr - w_hi.astype(F32)).astype(BF16)
    logits = _dot_nt(w_hi, h_hi) + _dot_nt(w_hi, h_lo) + _dot_nt(w_lo, h_hi)
    scores = _sigmoid(logits)
    biased = scores + b_ref[...]
    n_exp, win = biased.shape
    gsz = n_exp // N_GROUPS
    neg = -jnp.inf

    sub_g = lax.broadcasted_iota(I32, (gsz, win), 0)
    group_rows = []
    for g in range(N_GROUPS):
        blk = biased[g * gsz:(g + 1) * gsz, :]
        m1, i1 = _first_max(blk, sub_g, gsz)
        m2 = jnp.max(jnp.where(sub_g == i1, neg, blk), axis=0, keepdims=True)
        group_rows.append(m1 + m2)
    cur = jnp.concatenate(group_rows, axis=0)
    sub_n = lax.broadcasted_iota(I32, (N_GROUPS, win), 0)
    gmask = jnp.zeros((N_GROUPS, win), jnp.bool_)
    for _ in range(TOPK_GROUPS):
        _, i = _first_max(cur, sub_n, N_GROUPS)
        sel = sub_n == i
        gmask = gmask | sel
        cur = jnp.where(sel, neg, cur)
    emask = jnp.concatenate(
        [jnp.broadcast_to(gmask[g:g + 1, :], (gsz, win)) for g in range(N_GROUPS)], axis=0)
    masked = jnp.where(emask, biased, neg)

    sub_e = lax.broadcasted_iota(I32, (n_exp, win), 0)
    sels, picked = [], []
    for _ in range(TOP_K):
        _, i = _first_max(masked, sub_e, n_exp)
        sel = sub_e == i
        sels.append(sel)
        picked.append(jnp.sum(jnp.where(sel, scores, 0.0), axis=0, keepdims=True))
        masked = jnp.where(sel, neg, masked)
    total = picked[0]
    for p in picked[1:]:
        total = total + p
    wt_ref[...] = jnp.concatenate([p / total * ROUTED_SCALE for p in picked], axis=0)

    chosen = sels[0]
    for s in sels[1:]:
        chosen = chosen | s
    sel_b = jnp.where(chosen, 1.0, 0.0).astype(BF16)
    r = lax.broadcasted_iota(I32, (win, win), 0)
    c = lax.broadcasted_iota(I32, (win, win), 1)
    before = jnp.where(r < c, 1.0, 0.0).astype(BF16)
    rank = _dot(sel_b, before)
    count = _dot(sel_b, jnp.ones((win, win), BF16))
    run_len = jnp.floor((count + (RUN_ALIGN - 1.0)) * (1.0 / RUN_ALIGN)) * RUN_ALIGN
    er = lax.broadcasted_iota(I32, (n_exp, n_exp), 0)
    ec = lax.broadcasted_iota(I32, (n_exp, n_exp), 1)
    lower = jnp.where(ec < er, 1.0, 0.0).astype(BF16)
    slot_all = _dot(lower, run_len.astype(BF16)) + rank
    slot_ref[...] = jnp.concatenate(
        [jnp.sum(jnp.where(s, slot_all, 0.0), axis=0, keepdims=True) for s in sels],
        axis=0).astype(I32)
    cnt_ref[0] = _dot_nt(jnp.ones((8, win), BF16), sel_b)


def _router(x, mod, rows, norm_g, w_router, bias):
    n, d = x.shape
    n_exp = w_router.shape[1]
    win = rows.tile
    lanes = pl.BlockSpec((TOP_K, win), lambda i: (0, i))
    return pl.pallas_call(
        _router_kernel,
        grid=(rows.grid,),
        in_specs=[rows.rows(d), rows.mod(d, 3), rows.mod(d, 4), _resident((1, d)),
                  _resident((n_exp, d)), _resident((n_exp, 1))],
        out_specs=[rows.rows(d), lanes, lanes, pl.BlockSpec((1, 8, n_exp), lambda i: (i, 0, 0))],
        out_shape=[jax.ShapeDtypeStruct((n, d), BF16), jax.ShapeDtypeStruct((TOP_K, n), I32),
                   jax.ShapeDtypeStruct((TOP_K, n), F32),
                   jax.ShapeDtypeStruct((rows.grid, 8, n_exp), F32)],
        compiler_params=_cparams("parallel"),
        name="moe_router",
    )(x, mod, mod, norm_g.reshape(1, d), w_router.T, bias.reshape(n_exp, 1))


def _sorted_rows(win, n_exp):
    return -(-(win * TOP_K + n_exp * (RUN_ALIGN - 1)) // SORT_BLOCK) * SORT_BLOCK


def _copy_list(count, src0, dst0, length, step):
    n_exp = count.shape[1]
    end = jnp.cumsum(count, axis=1)
    idx = jnp.arange(length, dtype=I32)
    owner = jnp.sum((end[:, None, :] <= idx[None, :, None]).astype(I32), axis=2)
    own = (owner[:, :, None] == jnp.arange(n_exp, dtype=I32)[None, None, :]).astype(I32)
    pick = lambda a: jnp.sum(own * a[:, None, :], axis=2)
    off = (idx[None, :] - pick(end - count)) * step
    return (pick(src0) + off).reshape(-1), (pick(dst0) + off).reshape(-1), end[:, -1]


def _plan(cnt, tile, sorted_rows):
    c = cnt[:, 0, :].astype(I32)
    run = (c + RUN_ALIGN - 1) // RUN_ALIGN * RUN_ALIGN
    per_exp = jnp.sum(run, axis=0)
    padded = (per_exp + tile - 1) // tile * tile
    pend = jnp.cumsum(padded)
    pstart = pend - padded
    dest = pstart[None, :] + jnp.cumsum(run, axis=0) - run
    local = jnp.cumsum(run, axis=1) - run
    big = COPY_UNITS[0] * RUN_ALIGN
    n_big = run // big
    covered = n_big * big
    copies = _copy_list(n_big, local, dest, sorted_rows // big, big)
    for units in COPY_UNITS[1:]:
        one = ((run - covered) == units * RUN_ALIGN).astype(I32)
        copies += _copy_list(one, local + covered, dest + covered, c.shape[1], units * RUN_ALIGN)
    return dict(
        copies=copies, list_rows=sorted_rows,
        pad_start=pstart + per_exp, pad_chunks=(padded - per_exp) // RUN_ALIGN,
        pend=pend, used_tiles=(pend[-1] // tile).reshape(1))


def _capacity(n, n_win, n_exp, tile):
    worst = n * TOP_K + n_win * n_exp * (RUN_ALIGN - 1) + n_exp * (tile - RUN_ALIGN)
    return -(-worst // tile) * tile


def _copy_classes(copy_refs, sorted_rows, n_exp):
    for j, units in enumerate(COPY_UNITS):
        n_rows = units * RUN_ALIGN
        length = sorted_rows // n_rows if j == 0 else n_exp
        yield (*copy_refs[3 * j:3 * j + 3], length, n_rows)


def _used_rows(copy_refs, window, sorted_rows, n_exp):
    return sum(n_ref[window] * n_rows
               for _, _, n_ref, _, n_rows in _copy_classes(copy_refs, sorted_rows, n_exp))


def _window_copies(copy_refs, window, sorted_rows, n_exp, make_copy, wait):
    for src_ref, dst_ref, n_ref, length, n_rows in _copy_classes(copy_refs, sorted_rows, n_exp):
        def per_copy(i, carry, src_ref=src_ref, dst_ref=dst_ref, length=length, n_rows=n_rows):
            cp = make_copy(pl.multiple_of(src_ref[window * length + i], RUN_ALIGN),
                           pl.multiple_of(dst_ref[window * length + i], RUN_ALIGN), n_rows)
            if wait:
                cp.wait()
            else:
                cp.start()
            return carry

        lax.fori_loop(0, n_ref[window], per_copy, 0)


def _dispatch_kernel(*refs, n_exp, n_win, tile, groups, list_rows):
    n_lists = 3 * len(COPY_UNITS)
    copy_refs = refs[:n_lists]
    pad_start_ref, pad_chunks_ref, used_ref = refs[n_lists:n_lists + 3]
    refs = refs[n_lists + 3:]
    xbuf_ref, xs_scr, zero_scr, sems, tail_sem = refs[-5:]
    w = pl.program_id(0)
    cur = w % 2
    n_tail = xbuf_ref.shape[0] // tile - used_ref[0]

    def tail_copy(j):
        dst = pl.multiple_of((used_ref[0] + j) * tile, tile)
        return pltpu.make_async_copy(zero_scr, xbuf_ref.at[pl.ds(dst, tile)], tail_sem)

    @pl.when(w == 0)
    def _():
        zero_scr[...] = jnp.zeros_like(zero_scr)
        lax.fori_loop(0, n_tail, lambda j, c: (tail_copy(j).start(), c)[1], 0)

    def sort_window(h2_ref, slot_ref, n_rows):
        slot = slot_ref[...]
        win = slot.shape[1]
        h2 = h2_ref[...]
        local = lax.broadcasted_iota(I32, (ONEHOT_BLOCK, win), 0).astype(F32).astype(BF16)
        slot_block = slot // ONEHOT_BLOCK
        slot_local = (slot % ONEHOT_BLOCK).astype(F32)
        one, zero = jnp.ones((), BF16), jnp.zeros((), BF16)

        def sort_block(blk):
            key = jnp.where(slot_block == blk, slot_local, -1.0).astype(BF16)
            onehot = jnp.where(local == key[0:1, :], one, zero)
            for k in range(1, TOP_K):
                onehot = jnp.where(local == key[k:k + 1, :], one, onehot)
            lo = blk * ONEHOT_BLOCK
            xs_scr[cur, lo:lo + ONEHOT_BLOCK, :] = _dot(onehot, h2).astype(BF16)

        used_rows = _used_rows(copy_refs, w, list_rows, n_exp)
        typical = win * TOP_K + n_exp * (RUN_ALIGN - 1) // 2
        for blk in range(n_rows // ONEHOT_BLOCK):
            if blk * ONEHOT_BLOCK < typical:
                sort_block(blk)
            else:
                pl.when(used_rows > blk * ONEHOT_BLOCK)(functools.partial(sort_block, blk))

    for gi, (first, count, n_rows) in enumerate(groups):
        @pl.when(jnp.logical_and(w >= first, w < first + count))
        def _(gi=gi, n_rows=n_rows):
            sort_window(refs[2 * gi], refs[2 * gi + 1], n_rows)

    def copies(window, buf, wait):
        def copy(src, dst, n):
            return pltpu.make_async_copy(xs_scr.at[buf, pl.ds(src, n)],
                                         xbuf_ref.at[pl.ds(dst, n)], sems.at[buf])
        _window_copies(copy_refs, window, list_rows, n_exp, copy, wait)

    copies(w, cur, wait=False)

    @pl.when(w > 0)
    def _():
        copies(w - 1, 1 - cur, wait=True)

    @pl.when(w == n_win - 1)
    def _():
        copies(w, cur, wait=True)

    def fill_pads_and_finish():
        sem = sems.at[cur]

        big = PAD_PIECE // RUN_ALIGN

        def pad_copy(row, n_rows):
            return pltpu.make_async_copy(zero_scr.at[pl.ds(0, n_rows)],
                                         xbuf_ref.at[pl.ds(pl.multiple_of(row, RUN_ALIGN), n_rows)],
                                         sem)

        def pads(wait):
            def go(cp):
                if wait:
                    cp.wait()
                else:
                    cp.start()

            def per_expert(e, carry):
                n_big = pad_chunks_ref[e] // big
                rest = pad_start_ref[e] + n_big * PAD_PIECE
                lax.fori_loop(0, n_big, lambda j, c: (go(pad_copy(
                    pad_start_ref[e] + j * PAD_PIECE, PAD_PIECE)), c)[1], 0)
                lax.fori_loop(0, pad_chunks_ref[e] - n_big * big, lambda j, c: (go(pad_copy(
                    rest + j * RUN_ALIGN, RUN_ALIGN)), c)[1], 0)
                return carry
            lax.fori_loop(0, n_exp, per_expert, 0)

        pads(False)
        pads(True)
        lax.fori_loop(0, n_tail, lambda j, c: (tail_copy(j).wait(), c)[1], 0)

    pl.when(w == n_win - 1)(fill_pads_and_finish)


def _dispatch(sources, plan, n_exp, cap, tile):
    d = sources[0][0].shape[1]
    in_specs, operands, groups, first = [], [], [], 0
    for h2, slot_t, rows in sources:
        idx = lambda i, first=first, last=rows.grid - 1: jnp.clip(i - first, 0, last)
        in_specs += [pl.BlockSpec((rows.tile, d), lambda i, *_, idx=idx: (idx(i), 0)),
                     pl.BlockSpec((TOP_K, rows.tile), lambda i, *_, idx=idx: (0, idx(i)))]
        operands += [h2, slot_t]
        groups.append((first, rows.grid, _sorted_rows(rows.tile, n_exp)))
        first += rows.grid
    return pl.pallas_call(
        functools.partial(_dispatch_kernel, n_exp=n_exp, n_win=first, tile=tile,
                          groups=tuple(groups), list_rows=plan["list_rows"]),
        grid_spec=pltpu.PrefetchScalarGridSpec(
            num_scalar_prefetch=3 * len(COPY_UNITS) + 3,
            grid=(first,),
            in_specs=in_specs,
            out_specs=pl.BlockSpec(memory_space=pl.ANY),
            scratch_shapes=[pltpu.VMEM((2, plan["list_rows"], d), BF16),
                            pltpu.VMEM((tile, d), BF16),
                            pltpu.SemaphoreType.DMA((2,)), pltpu.SemaphoreType.DMA(())]),
        out_shape=jax.ShapeDtypeStruct((cap, d), BF16),
        compiler_params=_cparams("arbitrary"),
        name="moe_dispatch",
    )(*plan["copies"], plan["pad_start"], plan["pad_chunks"], plan["used_tiles"], *operands)


def _experts_kernel(texp_ref, used_ref, xbuf_ref, wg_ref, wu_ref, wd_ref, ybuf_ref,
                    x_ring, sems, y_ring, y_sems, wgu_scr, wd_scr, *, sub_tiles):
    i = pl.program_id(0)
    used = used_ref[0]
    live = i < used
    de = wd_scr.shape[0]
    depth, tile = x_ring.shape[0], x_ring.shape[1]
    y_depth = y_ring.shape[0]

    def fetch(j):
        slot = j % depth
        return pltpu.make_async_copy(xbuf_ref.at[pl.ds(pl.multiple_of(j * tile, tile), tile)],
                                     x_ring.at[slot], sems.at[slot])

    def put(j):
        slot = j % y_depth
        return pltpu.make_async_copy(y_ring.at[slot],
                                     ybuf_ref.at[pl.ds(pl.multiple_of(j * tile, tile), tile)],
                                     y_sems.at[slot])

    @pl.when(i == 0)
    def _():
        for j in range(depth - 1):
            @pl.when(j < used)
            def _():
                fetch(j).start()

    @pl.when(i + (depth - 1) < used)
    def _():
        fetch(i + (depth - 1)).start()

    @pl.when(jnp.logical_or(i == 0, texp_ref[i] != texp_ref[jnp.maximum(i - 1, 0)]))
    def _():
        wgu_scr[:, :de] = wg_ref[...].astype(BF16)
        wgu_scr[:, de:] = wu_ref[...].astype(BF16)
        wd_scr[...] = wd_ref[...].astype(BF16)

    @pl.when(live)
    def _():
        fetch(i).wait()

        @pl.when(i >= y_depth)
        def _():
            put(i - y_depth).wait()

        x_ref, y_ref = x_ring.at[i % depth], y_ring.at[i % y_depth]
        sub = tile // sub_tiles
        for s in range(sub_tiles):
            rows = slice(s * sub, (s + 1) * sub)
            ab = _dot(x_ref[rows, :], wgu_scr[...])
            mid = (_silu(ab[:, :de]) * ab[:, de:]).astype(BF16)
            y_ref[rows, :] = _dot(mid, wd_scr[...]).astype(BF16)
        put(i).start()

    @pl.when(i == pl.num_programs(0) - 1)
    def _():
        for back in range(y_depth, 0, -1):
            @pl.when(used >= back)
            def _():
                put(used - back).wait()


def _experts(xbuf, plan, layer, w_gate, w_up, w_down, tile):
    cap, d = xbuf.shape
    _, n_exp, _, de = w_gate.shape
    n_tiles = cap // tile
    used = plan["used_tiles"]
    tmap = jnp.minimum(jnp.arange(n_tiles, dtype=I32), used[0] - 1)
    texp = jnp.sum((plan["pend"][None, :] <= (tmap * tile)[:, None]).astype(I32), axis=1)
    texp = jnp.minimum(texp, n_exp - 1)
    return pl.pallas_call(
        functools.partial(_experts_kernel, sub_tiles=max(tile // EXPERT_SUB_TILE, 1)),
        grid_spec=pltpu.PrefetchScalarGridSpec(
            num_scalar_prefetch=2,
            grid=(n_tiles,),
            in_specs=[pl.BlockSpec(memory_space=pl.ANY),
                      pl.BlockSpec((None, None, d, de), lambda i, te, u: (layer, te[i], 0, 0)),
                      pl.BlockSpec((None, None, d, de), lambda i, te, u: (layer, te[i], 0, 0)),
                      pl.BlockSpec((None, None, de, d), lambda i, te, u: (layer, te[i], 0, 0))],
            out_specs=pl.BlockSpec(memory_space=pl.ANY),
            scratch_shapes=[pltpu.VMEM((EXPERT_RING, tile, d), BF16),
                            pltpu.SemaphoreType.DMA((EXPERT_RING,)),
                            pltpu.VMEM((2, tile, d), BF16), pltpu.SemaphoreType.DMA((2,)),
                            pltpu.VMEM((d, 2 * de), BF16), pltpu.VMEM((de, d), BF16)]),
        out_shape=jax.ShapeDtypeStruct((cap, d), BF16),
        input_output_aliases={2: 0},
        compiler_params=_cparams("arbitrary"),
        name="moe_experts",
    )(texp, used, xbuf, w_gate, w_up, w_down)


def _combine_kernel(*refs, n_exp, n_win, final_norm, window_base, list_rows):
    n_lists = 3 * len(COPY_UNITS)
    copy_refs = refs[:n_lists]
    (ybuf_ref, slot_ref, wt_ref, h2_ref, x_ref, g2_ref, sg_ref, su_ref, sd_ref, fin_ref,
     out_ref, ys_scr, sems, acc_scr) = refs[n_lists:]
    w = pl.program_id(0)
    cur = w % 2
    n_rows = ys_scr.shape[1]

    def copies(window, buf, wait):
        def copy(loc, dst, n):
            return pltpu.make_async_copy(ybuf_ref.at[pl.ds(dst, n)],
                                         ys_scr.at[buf, pl.ds(loc, n)], sems.at[buf])
        _window_copies(copy_refs, window_base + window, list_rows, n_exp, copy, wait)

    @pl.when(w == 0)
    def _():
        ys_scr[...] = jnp.zeros_like(ys_scr)
        copies(w, cur, wait=False)

    @pl.when(w + 1 < n_win)
    def _():
        copies(w + 1, 1 - cur, wait=False)

    h2 = h2_ref[...]
    shared = _dot((_silu(_dot(h2, sg_ref[...])) * _dot(h2, su_ref[...])).astype(BF16), sd_ref[...])
    copies(w, cur, wait=True)

    slot, wt = slot_ref[...], wt_ref[...].astype(BF16)
    win = slot.shape[1]
    local = lax.broadcasted_iota(I32, (ONEHOT_BLOCK, win), 0).astype(F32).astype(BF16)
    slot_block = slot // ONEHOT_BLOCK
    slot_local = (slot % ONEHOT_BLOCK).astype(F32)
    def block_sum(blk):
        key = jnp.where(slot_block == blk, slot_local, -1.0).astype(BF16)
        weights = jnp.where(local == key[0:1, :], wt[0:1, :], jnp.zeros((), BF16))
        for k in range(1, TOP_K):
            weights = jnp.where(local == key[k:k + 1, :], wt[k:k + 1, :], weights)
        lo = blk * ONEHOT_BLOCK
        return _dot_tn(weights, ys_scr[cur, lo:lo + ONEHOT_BLOCK, :])

    used_rows = _used_rows(copy_refs, window_base + w, list_rows, n_exp)
    typical = win * TOP_K + n_exp * (RUN_ALIGN - 1) // 2
    n_blocks = n_rows // ONEHOT_BLOCK
    always = [blk for blk in range(n_blocks) if blk * ONEHOT_BLOCK < typical]
    routed = shared
    for blk in always:
        routed = routed + block_sum(blk)
    if len(always) < n_blocks:
        acc_scr[...] = routed
        for blk in range(len(always), n_blocks):
            @pl.when(used_rows > blk * ONEHOT_BLOCK)
            def _(blk=blk):
                acc_scr[...] += block_sum(blk)
        routed = acc_scr[...]
    out = x_ref[...] + g2_ref[...] * routed
    if final_norm:
        out = out * lax.rsqrt(jnp.mean(out * out, axis=-1, keepdims=True) + EPS) * fin_ref[...]
    out_ref[...] = out


def _combine(ybuf, plan, slot, wt, h2, x, mod, rows, sh_bf16, final_g, n_exp, final_norm,
             window_base):
    n, d = x.shape
    win = rows.tile
    sg, su, sd = sh_bf16
    pairs = pl.BlockSpec((TOP_K, win), lambda i, *_: (0, i))
    return pl.pallas_call(
        functools.partial(_combine_kernel, n_exp=n_exp, n_win=rows.grid, final_norm=final_norm,
                          window_base=window_base, list_rows=plan["list_rows"]),
        grid_spec=pltpu.PrefetchScalarGridSpec(
            num_scalar_prefetch=3 * len(COPY_UNITS),
            grid=(rows.grid,),
            in_specs=[pl.BlockSpec(memory_space=pl.ANY), pairs, pairs, rows.rows(d), rows.rows(d),
                      rows.mod(d, 5), _resident(sg.shape), _resident(su.shape),
                      _resident(sd.shape), _resident((1, d))],
            out_specs=rows.rows(d),
            scratch_shapes=[pltpu.VMEM((2, _sorted_rows(win, n_exp), d), BF16),
                            pltpu.SemaphoreType.DMA((2,)), pltpu.VMEM((win, d), F32)]),
        out_shape=jax.ShapeDtypeStruct((n, d), F32),
        compiler_params=_cparams("arbitrary"),
        name="moe_combine",
    )(*plan["copies"], ybuf, slot, wt, h2, x, mod, sg, su, sd, final_g.reshape(1, d))


def _moe(groups, layer, norm_g, w_router, bias, w_gate, w_up, w_down, sh_bf16, final_g,
         final_norm, tile):
    n_exp = w_router.shape[1]
    routed = [_router(x, mod, rows, norm_g, w_router, bias) for x, mod, rows in groups]
    bases, total = [], 0
    for _, _, rows in groups:
        bases.append(total)
        total += rows.grid
    plan = _plan(jnp.concatenate([r[3] for r in routed], axis=0), tile,
                 max(_sorted_rows(rows.tile, n_exp) for _, _, rows in groups))
    cap = _capacity(sum(x.shape[0] for x, _, _ in groups), total, n_exp, tile)
    xbuf = _dispatch([(r[0], r[1], g[2]) for r, g in zip(routed, groups)], plan, n_exp, cap, tile)
    ybuf = _experts(xbuf, plan, layer, w_gate, w_up, w_down, tile)
    return [_combine(ybuf, plan, slot_t, wt_t, h2, x, mod, rows, sh_bf16, final_g, n_exp,
                     final_norm, base)
            for (x, mod, rows), (h2, slot_t, wt_t, _), base in zip(groups, routed, bases)]


def _rope_tables(pos, half):
    inv = ROPE_BASE ** (-jnp.arange(half, dtype=F32) / half)
    ang = pos.astype(F32)[:, None] * inv[None, :]
    return jnp.cos(ang), jnp.sin(ang)


class _Group:
    def __init__(self, x3, mod_all, pos0, s_ret, s_gla):
        self.batch, self.seq, d = x3.shape
        self.n = self.batch * self.seq
        self.x = x3.reshape(self.n, d)
        self.rows = _Rows(self.n, self.seq, min(ROW_TILE, self.n))
        self.decode = self.seq == 1
        self.mod_all, self.pos0, self.s_ret, self.s_gla = mod_all, pos0, s_ret, s_gla
        self.new_ret, self.new_gla = [], []

    def mod(self, layer):
        m = self.mod_all[layer]
        return m if self.decode else m.reshape(self.batch, 1, m.shape[-1])


def _mixer(g, layer, wts):
    d = g.x.shape[1]
    x, mod, rows = g.x, g.mod(layer), g.rows
    ret_dk = d // RET_HEADS
    ret_dv = 2 * ret_dk
    gla_dk = d // (2 * GLA_HEADS)
    gla_dv = d // GLA_HEADS
    j = layer // 2
    if layer % 2 == 0:
        pos = (jnp.full((rows.tile,), g.pos0, I32) if g.decode
               else g.pos0 + jnp.arange(g.seq, dtype=I32))
        cos, sin = _rope_tables(pos, ret_dk // 2)
        q, k, v, gate = _ret_proj(x, mod, rows, wts["norm_mix_g"][layer], wts["ret_w_in"][j],
                                  cos, sin, RET_HEADS, ret_dk, ret_dv)
        log_gamma = jnp.log1p(-jnp.power(2.0, -5.0 - jnp.arange(RET_HEADS, dtype=F32)))
        if g.decode:
            decay = jnp.broadcast_to(jnp.repeat(jnp.exp(log_gamma), ret_dk)[None, :],
                                     (g.n, RET_HEADS * ret_dk))
            o, s_new = _step(g.s_ret, j, q, k, decay, v, RET_HEADS, ret_dk, ret_dv)
            out = _post(o, gate, x, mod, rows, wts["ret_norm_g"][j], wts["ret_w_out"][j],
                        RET_HEADS, ret_dv, center=True)
        else:
            out, s_new = _ret_core(q, k, v, gate, x, mod, wts["ret_norm_g"][j],
                                   wts["ret_w_out"][j], log_gamma, g.batch, g.seq,
                                   RET_HEADS, ret_dk, ret_dv)
        g.new_ret.append(s_new)
        return out
    q, k, v, gate, la = _gla_proj(x, mod, rows, wts["norm_mix_g"][layer], wts["gla_w_in"][j],
                                  wts["gla_w_a1"][j], wts["gla_w_a2"][j], wts["gla_b_a"][j],
                                  GLA_HEADS * gla_dk, GLA_HEADS * gla_dv)
    if g.decode:
        o, s_new = _step(g.s_gla, j, q, k, jnp.exp(la), v, GLA_HEADS, gla_dk, gla_dv)
    else:
        o, s_new = _gla_core(q, k, v, la, g.batch, g.seq, GLA_HEADS, gla_dk, gla_dv)
    g.new_gla.append(s_new)
    return _post(o, gate, x, mod, rows, wts["gla_norm_g"][j], wts["gla_w_out"][j],
                 GLA_HEADS, gla_dv, center=False)


def _trunk(groups, wts):
    depth = wts["ada_w"].shape[0]
    for layer in range(depth):
        mixed = [(_mixer(g, layer, wts), g.mod(layer), g.rows) for g in groups]
        outs = _moe(mixed, layer, wts["norm_ffn_g"][layer], wts["moe_w_router"][layer],
                    wts["moe_router_bias"][layer], wts["moe_w_gate"], wts["moe_w_up"],
                    wts["moe_w_down"], wts["shared"][layer], wts["final_norm_g"],
                    final_norm=layer == depth - 1, tile=EXPERT_TILE)
        for g, x in zip(groups, outs):
            g.x = x
    return [(g.x.reshape(g.batch, g.seq, -1), jnp.stack(g.new_ret), jnp.stack(g.new_gla))
            for g in groups]


def kernel(x_prompt, x_sample, state_ret, state_gla, c_prompt, c_sample, ret_w_in, ret_norm_g, ret_w_out, gla_w_in, gla_w_a1, gla_w_a2, gla_b_a, gla_norm_g, gla_w_out, ada_w, ada_b, norm_mix_g, norm_ffn_g, moe_w_router, moe_router_bias, moe_w_gate, moe_w_up, moe_w_down, sh_w_gate, sh_w_up, sh_w_down, final_norm_g):
    b = x_prompt.shape[0]
    depth = ada_w.shape[0]
    rank = gla_w_a1.shape[-1]
    pad = LANES - rank
    wts = dict(
        ret_w_in=ret_w_in.astype(BF16), ret_norm_g=ret_norm_g, ret_w_out=ret_w_out.astype(BF16),
        gla_w_in=gla_w_in.astype(BF16),
        gla_w_a1=jnp.pad(gla_w_a1, ((0, 0), (0, 0), (0, pad))).astype(BF16),
        gla_w_a2=jnp.pad(gla_w_a2, ((0, 0), (0, pad), (0, 0))).astype(BF16),
        gla_b_a=gla_b_a, gla_norm_g=gla_norm_g, gla_w_out=gla_w_out.astype(BF16),
        ada_w=ada_w, norm_mix_g=norm_mix_g, norm_ffn_g=norm_ffn_g,
        moe_w_router=moe_w_router, moe_router_bias=moe_router_bias,
        moe_w_gate=moe_w_gate, moe_w_up=moe_w_up, moe_w_down=moe_w_down,
        shared=[(sh_w_gate[l].astype(BF16), sh_w_up[l].astype(BF16), sh_w_down[l].astype(BF16))
                for l in range(depth)],
        final_norm_g=final_norm_g)
    mod = _ada(jnp.concatenate([c_prompt, c_sample], axis=0), ada_w, ada_b)
    (y_p, ret_p, gla_p), (y_s, ret_s, gla_s) = _trunk(
        [_Group(x_prompt, mod[:, :b], 0, None, None),
         _Group(x_sample, mod[:, b:], PAST_LEN, state_ret, state_gla)], wts)
    return (y_p, y_s, ret_p, gla_p, ret_s, gla_s)
```
